```python
import math
import jax, jax.numpy as jnp
from jax import lax
import numpy as np

D_MODEL = 1024
BATCH = 2
SEQ = 8192
DEPTH = 2

GRID_W = 64
ROPE_BASE = 10000.0
EPS = 1e-6
Q_BLOCK = 128

N_GROUPS = 4
GROUP_W = D_MODEL // N_GROUPS
MIX_W = N_GROUPS * GROUP_W

A_HEADS = 4
A_NOPE = 64
A_ROPE = 32
A_VDIM = GROUP_W // A_HEADS
A_Q_LORA = 192
A_KV_LORA = 128
A_COLS = A_Q_LORA + A_KV_LORA + A_ROPE

B_HEADS = 4
B_KV_HEADS = 2
B_HDIM = GROUP_W // B_HEADS
B_COLS = (B_HEADS + 2 * B_KV_HEADS) * B_HDIM

C_HEADS = 4
C_HDIM = GROUP_W // C_HEADS
C_NGROUPS = 2
C_STATE = 64
C_CHUNK = 128
C_XBC = GROUP_W + 2 * C_NGROUPS * C_STATE
C_COLS = GROUP_W + C_XBC + 2 * C_HEADS

D_HEADS = 4
D_HDIM = GROUP_W // D_HEADS
D_CHUNK = 64
D_QKV = 3 * GROUP_W
D_COLS = D_QKV + GROUP_W + 4 * D_HEADS

IN_COLS = A_COLS + B_COLS + C_COLS + D_COLS
CONV_W = 3

D_FF = 2816
FFN_CONV_W = 3

kernel_name = "bidir_hybrid_parallel_heads_mla_gqa_ssd_deltanet"


def split_last(t, sizes):
    return jnp.split(t, [int(s) for s in np.cumsum(sizes)[:-1]], axis=-1)


def rms_norm(x, w, eps=EPS):
    x32 = x.astype(jnp.float32)
    y = x32 * lax.rsqrt(jnp.mean(x32 * x32, axis=-1, keepdims=True) + eps)
    return (y * w.astype(jnp.float32)).astype(x.dtype)


def l2_normalize(x, eps=1e-6):
    x32 = x.astype(jnp.float32)
    return (x32 * lax.rsqrt(jnp.sum(x32 * x32, axis=-1, keepdims=True) + eps)).astype(x.dtype)


def dwconv_centred(x, w, b=None):
    K = w.shape[0]
    L = x.shape[1]
    pad = K // 2
    xp = jnp.pad(x, ((0, 0), (pad, pad), (0, 0)))
    y = sum(xp[:, k:k + L] * w[k] for k in range(K))
    return y if b is None else y + b


def axial_rope_tables(seq_len, rot_dim):
    rows = seq_len // GRID_W
    row = jnp.repeat(jnp.arange(rows), GRID_W).astype(jnp.float32)
    col = jnp.tile(jnp.arange(GRID_W), rows).astype(jnp.float32)
    sec = rot_dim // 2
    inv_freq = ROPE_BASE ** (-jnp.arange(0, sec, 2, dtype=jnp.float32) / sec)
    ang_r = row[:, None] * inv_freq
    ang_c = col[:, None] * inv_freq
    ang = jnp.concatenate([ang_r, ang_r, ang_c, ang_c], axis=-1)
    return jnp.cos(ang), jnp.sin(ang)


def apply_axial_rope(x, cos, sin):
    r = x.shape[-1]
    xs = x.reshape(x.shape[:-1] + (2, 2, r // 4))
    rot = jnp.stack([-xs[..., 1, :], xs[..., 0, :]], axis=-2).reshape(x.shape)
    return (x * cos + rot * sin).astype(x.dtype)


def mla_attention(q_nope, q_rope, k_nope, k_rope, v):
    Bsz, L, H, dn = q_nope.shape
    dr = q_rope.shape[-1]
    nb = L // Q_BLOCK
    scale = (dn + dr) ** -0.5
    qn_b = q_nope.reshape(Bsz, nb, Q_BLOCK, H, dn).swapaxes(0, 1)
    qr_b = q_rope.reshape(Bsz, nb, Q_BLOCK, H, dr).swapaxes(0, 1)

    def block(qs):
        qn, qr = qs
        s = jnp.einsum('bqhd,bkhd->bhqk', qn, k_nope) + jnp.einsum('bqhr,bkr->bhqk', qr, k_rope)
        p = jax.nn.softmax(s.astype(jnp.float32) * scale, axis=-1).astype(v.dtype)
        return jnp.einsum('bhqk,bkhd->bqhd', p, v)

    o = lax.map(block, (qn_b, qr_b))
    return o.swapaxes(0, 1).reshape(Bsz, L, H * v.shape[-1])


def gqa_attention(q, k, v):
    Bsz, L, Hq, hd = q.shape
    Hkv = k.shape[2]
    rep = Hq // Hkv
    nb = L // Q_BLOCK
    qb = q.reshape(Bsz, nb, Q_BLOCK, Hkv, rep, hd).swapaxes(0, 1)
    scale = hd ** -0.5

    def block(qi):
        s = jnp.einsum('bqgrd,bkgd->bgrqk', qi, k)
        p = jax.nn.softmax(s.astype(jnp.float32) * scale, axis=-1).astype(v.dtype)
        return jnp.einsum('bgrqk,bkgd->bqgrd', p, v)

    o = lax.map(block, qb)
    return o.swapaxes(0, 1).reshape(Bsz, L, Hq * hd)


def ssd_scan(x, dt, A, Bm, Cm):
    Bsz, L, H, P = x.shape
    N = Bm.shape[-1]
    Q = C_CHUNK
    nc = L // Q
    xc = (x * dt[..., None]).reshape(Bsz, nc, Q, H, P)
    Bc = Bm.reshape(Bsz, nc, Q, H, N)
    Cc = Cm.reshape(Bsz, nc, Q, H, N)
    acum = jnp.cumsum((dt * A).reshape(Bsz, nc, Q, H).transpose(0, 1, 3, 2), axis=-1)
    idx = jnp.arange(Q)
    causal = idx[:, None] >= idx[None, :]
    seg = acum[..., :, None] - acum[..., None, :]
    decay = jnp.exp(jnp.where(causal, seg, -jnp.inf))
    scores = jnp.einsum('bcihn,bcjhn->bchij', Cc, Bc) * decay
    y_diag = jnp.einsum('bchij,bcjhp->bcihp', scores, xc)
    decay_to_end = jnp.exp(acum[..., -1:] - acum)
    states = jnp.einsum('bchj,bcjhn,bcjhp->bchpn', decay_to_end, Bc, xc)
    chunk_decay = jnp.exp(acum[..., -1])

    def step(S, inp):
        st, dec = inp
        return S * dec[..., None, None] + st, S

    S0 = jnp.zeros((Bsz, H, P, N), x.dtype)
    _, S_in = lax.scan(step, S0, (states.swapaxes(0, 1), chunk_decay.swapaxes(0, 1)))
    S_in = S_in.swapaxes(0, 1)
    y_off = jnp.einsum('bcihn,bchpn,bchi->bcihp', Cc, S_in, jnp.exp(acum))
    return (y_diag + y_off).reshape(Bsz, L, H, P)


def gated_delta_rule(q, k, v, g, beta):
    Bsz, L, H, dk = q.shape
    dv = v.shape[-1]
    Q = D_CHUNK
    nc = L // Q

    def chunks(t):
        return t.reshape(Bsz, nc, Q, H, -1).transpose(0, 1, 3, 2, 4)

    qc = chunks(q) * dk ** -0.5
    kc = chunks(k)
    vc = chunks(v)
    bc = beta.reshape(Bsz, nc, Q, H).transpose(0, 1, 3, 2)
    G = jnp.cumsum(g.reshape(Bsz, nc, Q, H).transpose(0, 1, 3, 2), axis=-1)
    idx = jnp.arange(Q)
    lower_strict = idx[:, None] > idx[None, :]
    lower_incl = idx[:, None] >= idx[None, :]
    seg = G[..., :, None] - G[..., None, :]
    decay = jnp.exp(jnp.where(lower_incl, seg, -jnp.inf))
    kb = kc * bc[..., None]
    Lmat = jnp.where(lower_strict, jnp.einsum('bchid,bchjd->bchij', kb, kc) * decay, 0.0)
    eye = jnp.eye(Q, dtype=Lmat.dtype)
    T = lax.linalg.triangular_solve(Lmat + eye, jnp.broadcast_to(eye, Lmat.shape),
                                    left_side=True, lower=True, unit_diagonal=True)
    u_base = jnp.einsum('bchij,bchjd->bchid', T, vc * bc[..., None])
    w = jnp.einsum('bchij,bchjd->bchid', T, kb * jnp.exp(G)[..., None])
    qk = jnp.einsum('bchid,bchjd->bchij', qc, kc) * decay
    q_dec = qc * jnp.exp(G)[..., None]
    k_dec = kc * jnp.exp(G[..., -1:] - G)[..., None]
    g_end = jnp.exp(G[..., -1])

    def step(S, inp):
        u_c, w_c, qk_c, qd_c, kd_c, ge_c = inp
        v_new = u_c - jnp.einsum('bhid,bhdv->bhiv', w_c, S)
        o = jnp.einsum('bhid,bhdv->bhiv', qd_c, S) + jnp.einsum('bhij,bhjv->bhiv', qk_c, v_new)
        S = S * ge_c[..., None, None] + jnp.einsum('bhjd,bhjv->bhdv', kd_c, v_new)
        return S, o

    xs = tuple(t.swapaxes(0, 1) for t in (u_base, w, qk, q_dec, k_dec, g_end))
    S0 = jnp.zeros((Bsz, H, dk, dv), q.dtype)
    _, o = lax.scan(step, S0, xs)
    return o.transpose(1, 0, 3, 2, 4).reshape(Bsz, L, H, dv)


def mla_mixer(p, q_norm, w_uq, kv_norm, w_ukv, out_norm, cos, sin):
    Bsz, L, _ = p.shape
    cq, ckv, kr = split_last(p, [A_Q_LORA, A_KV_LORA, A_ROPE])
    q = (rms_norm(cq, q_norm) @ w_uq).reshape(Bsz, L, A_HEADS, A_NOPE + A_ROPE)
    kv = (rms_norm(ckv, kv_norm) @ w_ukv).reshape(Bsz, L, A_HEADS, A_NOPE + A_VDIM)
    q_nope, q_rope = q[..., :A_NOPE], q[..., A_NOPE:]
    k_nope, v = kv[..., :A_NOPE], kv[..., A_NOPE:]
    q_rope = apply_axial_rope(q_rope, cos[:, None, :], sin[:, None, :])
    k_rope = apply_axial_rope(kr, cos, sin)
    o = mla_attention(q_nope, q_rope, k_nope, k_rope, v)
    return rms_norm(o, out_norm)


def gqa_mixer(p, q_norm, k_norm, out_norm, cos, sin):
    Bsz, L, _ = p.shape
    q, k, v = split_last(p, [B_HEADS * B_HDIM, B_KV_HEADS * B_HDIM, B_KV_HEADS * B_HDIM])
    q = rms_norm(q.reshape(Bsz, L, B_HEADS, B_HDIM), q_norm)
    k = rms_norm(k.reshape(Bsz, L, B_KV_HEADS, B_HDIM), k_norm)
    v = v.reshape(Bsz, L, B_KV_HEADS, B_HDIM)
    q = apply_axial_rope(q, cos[:, None, :], sin[:, None, :])
    k = apply_axial_rope(k, cos[:, None, :], sin[:, None, :])
    return rms_norm(gqa_attention(q, k, v), out_norm)


def mamba2_mixer(p, conv_w, conv_b, a_log, dt_bias, d_skip, out_norm):
    Bsz, L, _ = p.shape
    f32 = jnp.float32
    z, xbc, dt_raw = split_last(p, [GROUP_W, C_XBC, 2 * C_HEADS])
    xbc = jax.nn.silu(dwconv_centred(xbc, conv_w, conv_b))
    xs, Bm, Cm = split_last(xbc, [GROUP_W, C_NGROUPS * C_STATE, C_NGROUPS * C_STATE])
    rep = C_HEADS // C_NGROUPS
    xs = xs.reshape(Bsz, L, C_HEADS, C_HDIM).astype(f32)
    Bm = jnp.repeat(Bm.reshape(Bsz, L, C_NGROUPS, C_STATE), rep, axis=2).astype(f32)
    Cm = jnp.repeat(Cm.reshape(Bsz, L, C_NGROUPS, C_STATE), rep, axis=2).astype(f32)
    dt = jax.nn.softplus((dt_raw.reshape(Bsz, L, 2, C_HEADS) + dt_bias).astype(f32))
    A = -jnp.exp(a_log.astype(f32))
    fl = lambda t: jnp.flip(t, axis=1)
    y_f = ssd_scan(xs, dt[:, :, 0], A[0], Bm, Cm)
    y_b = fl(ssd_scan(fl(xs), fl(dt[:, :, 1]), A[1], fl(Bm), fl(Cm)))
    y = y_f + y_b + xs * d_skip.astype(f32)[:, None]
    y = y.reshape(Bsz, L, GROUP_W).astype(p.dtype)
    return rms_norm(y * jax.nn.silu(z), out_norm)


def deltanet_mixer(p, conv_w, a_log, dt_bias, out_norm):
    Bsz, L, _ = p.shape
    f32 = jnp.float32
    qkv, z, ab = split_last(p, [D_QKV, GROUP_W, 4 * D_HEADS])
    qkv = jax.nn.silu(dwconv_centred(qkv, conv_w))
    q, k, v = split_last(qkv, [GROUP_W, GROUP_W, GROUP_W])
    q = l2_normalize(q.reshape(Bsz, L, D_HEADS, D_HDIM)).astype(f32)
    k = l2_normalize(k.reshape(Bsz, L, D_HEADS, D_HDIM)).astype(f32)
    v = v.reshape(Bsz, L, D_HEADS, D_HDIM).astype(f32)
    ab = ab.reshape(Bsz, L, 4, D_HEADS).astype(f32)
    beta = jax.nn.sigmoid(ab[:, :, 0:2])
    g = -jnp.exp(a_log.astype(f32)) * jax.nn.softplus(ab[:, :, 2:4] + dt_bias.astype(f32))
    fl = lambda t: jnp.flip(t, axis=1)
    o_f = gated_delta_rule(q, k, v, g[:, :, 0], beta[:, :, 0])
    o_b = fl(gated_delta_rule(fl(q), fl(k), fl(v), fl(g[:, :, 1]), fl(beta[:, :, 1])))
    o = rms_norm((o_f + o_b).astype(p.dtype), out_norm)
    o = o * jax.nn.silu(z.reshape(Bsz, L, D_HEADS, D_HDIM))
    return o.reshape(Bsz, L, GROUP_W)


def conv_ffn(h, w_in, conv_w, conv_b, w_out):
    gu = dwconv_centred(h @ w_in, conv_w, conv_b)
    gate, up = split_last(gu, [D_FF, D_FF])
    return (jax.nn.silu(gate) * up) @ w_out


def setup_inputs(seed: int = 0) -> dict:
    key = jax.random.key(seed)
    ks = iter(jax.random.split(key, 40))
    nrm = lambda shape, scale: jax.random.normal(next(ks), shape, jnp.float32) * scale
    gain = lambda shape: 1.0 + nrm(shape, 0.05)
    log_a = lambda h: jnp.log(jax.random.uniform(next(ks), (DEPTH, 2, h), jnp.float32, 1.0, 16.0))

    def dt_bias(h):
        dt = jnp.exp(jax.random.uniform(next(ks), (DEPTH, 2, h), jnp.float32, math.log(1e-3), math.log(1e-1)))
        return dt + jnp.log(-jnp.expm1(-dt))

    return {
        "x": nrm((BATCH, SEQ, D_MODEL), 1.0),
        "pre_mix_norm": gain((DEPTH, D_MODEL)),
        "w_in": nrm((DEPTH, D_MODEL, IN_COLS), D_MODEL ** -0.5),
        "a_q_norm": gain((DEPTH, A_Q_LORA)),
        "a_w_uq": nrm((DEPTH, A_Q_LORA, A_HEADS * (A_NOPE + A_ROPE)), A_Q_LORA ** -0.5),
        "a_kv_norm": gain((DEPTH, A_KV_LORA)),
        "a_w_ukv": nrm((DEPTH, A_KV_LORA, A_HEADS * (A_NOPE + A_VDIM)), A_KV_LORA ** -0.5),
        "a_out_norm": gain((DEPTH, GROUP_W)),
        "b_q_norm": gain((DEPTH, B_HDIM)),
        "b_k_norm": gain((DEPTH, B_HDIM)),
        "b_out_norm": gain((DEPTH, GROUP_W)),
        "c_conv_w": nrm((DEPTH, CONV_W, C_XBC), CONV_W ** -0.5),
        "c_conv_b": nrm((DEPTH, C_XBC), 0.02),
        "c_a_log": log_a(C_HEADS),
        "c_dt_bias": dt_bias(C_HEADS),
        "c_d_skip": 1.0 + nrm((DEPTH, C_HEADS), 0.1),
        "c_out_norm": gain((DEPTH, GROUP_W)),
        "d_conv_w": nrm((DEPTH, CONV_W, D_QKV), CONV_W ** -0.5),
        "d_a_log": log_a(D_HEADS),
        "d_dt_bias": dt_bias(D_HEADS),
        "d_out_norm": gain((DEPTH, D_HDIM)),
        "w_out": nrm((DEPTH, MIX_W, D_MODEL), MIX_W ** -0.5),
        "post_mix_norm": gain((DEPTH, D_MODEL)),
        "pre_ffn_norm": gain((DEPTH, D_MODEL)),
        "f_w_in": nrm((DEPTH, D_MODEL, 2 * D_FF), D_MODEL ** -0.5),
        "f_conv_w": nrm((DEPTH, FFN_CONV_W, 2 * D_FF), FFN_CONV_W ** -0.5),
        "f_conv_b": nrm((DEPTH, 2 * D_FF), 0.02),
        "f_w_out": nrm((DEPTH, D_FF, D_MODEL), D_FF ** -0.5),
        "post_ffn_norm": gain((DEPTH, D_MODEL)),
    }


def reference(x, pre_mix_norm, w_in, a_q_norm, a_w_uq, a_kv_norm, a_w_ukv, a_out_norm,
              b_q_norm, b_k_norm, b_out_norm, c_conv_w, c_conv_b, c_a_log, c_dt_bias, c_d_skip,
              c_out_norm, d_conv_w, d_a_log, d_dt_bias, d_out_norm, w_out, post_mix_norm,
              pre_ffn_norm, f_w_in, f_conv_w, f_conv_b, f_w_out, post_ffn_norm):
    L = x.shape[1]
    cos_a, sin_a = axial_rope_tables(L, A_ROPE)
    cos_b, sin_b = axial_rope_tables(L, B_HDIM)
    for l in range(DEPTH):
        h = rms_norm(x, pre_mix_norm[l])
        p = h @ w_in[l]
        pa, pb, pc, pd = split_last(p, [A_COLS, B_COLS, C_COLS, D_COLS])
        o_a = mla_mixer(pa, a_q_norm[l], a_w_uq[l], a_kv_norm[l], a_w_ukv[l], a_out_norm[l], cos_a, sin_a)
        o_b = gqa_mixer(pb, b_q_norm[l], b_k_norm[l], b_out_norm[l], cos_b, sin_b)
        o_c = mamba2_mixer(pc, c_conv_w[l], c_conv_b[l], c_a_log[l], c_dt_bias[l], c_d_skip[l], c_out_norm[l])
        o_d = deltanet_mixer(pd, d_conv_w[l], d_a_log[l], d_dt_bias[l], d_out_norm[l])
        o = jnp.concatenate([o_a, o_b, o_c, o_d], axis=-1)
        x = x + rms_norm(o @ w_out[l], post_mix_norm[l])
        h = rms_norm(x, pre_ffn_norm[l])
        x = x + rms_norm(conv_ffn(h, f_w_in[l], f_conv_w[l], f_conv_b[l], f_w_out[l]), post_ffn_norm[l])
    return x
```

```python
import functools
import math

import numpy as np
import jax
import jax.numpy as jnp
from jax import lax
from jax.experimental import pallas as pl
from jax.experimental.pallas import tpu as pltpu

F32 = jnp.float32
BF16 = jnp.bfloat16

LANES = 128
SUBLANES = 8
BF16_ROWS = 16
VMEM_LIMIT = 56 * 1024 * 1024

EPS = 1e-6
ROPE_BASE = 10000.0
GRID_W = 64
D_MODEL = 1024
GROUP_W = 256
HEADS = 4
HDIM = 64

A_NOPE, A_ROPE, A_Q_LORA, A_KV_LORA = 64, 32, 192, 128
B_KV_HEADS = 2
C_STATE, C_CHUNK, C_XBC = 64, 128, 512
D_CHUNK, D_QKV = 64, 768
D_FF = 2816
FF_TILE = 256

OFF_C_XBC, OFF_A_CQ, OFF_D_QKV = 0, 512, 768
OFF_B_Q, OFF_B_QROT, OFF_C_Z, OFF_D_Z = 1536, 1792, 2048, 2304
OFF_A_CKV, OFF_SMALL, OFF_B_K, OFF_B_KROT, OFF_B_V = 2560, 2688, 2816, 2944, 3072
IN_COLS_PADDED = 3200
SM_DT, SM_AB, SM_KR, SM_KRROT = 0, 16, 64, 96

LOG2E = math.log2(math.e)


def _cparams(*sem):
    return pltpu.CompilerParams(dimension_semantics=sem, vmem_limit_bytes=VMEM_LIMIT)


def _dot(a, b):
    return jnp.dot(a.astype(BF16), b.astype(BF16), preferred_element_type=F32)


def _dot_nt(a, b):
    return lax.dot_general(a.astype(BF16), b.astype(BF16), (((1,), (1,)), ((), ())),
                           preferred_element_type=F32)


def _dot_tn(a, b):
    return lax.dot_general(a.astype(BF16), b.astype(BF16), (((0,), (0,)), ((), ())),
                           preferred_element_type=F32)


def _split3(x):
    hi = x.astype(BF16)
    r1 = x - hi.astype(F32)
    mid = r1.astype(BF16)
    lo = (r1 - mid.astype(F32)).astype(BF16)
    return hi, mid, lo


def _dot_sel_rhs(x, sel):
    hi, mid, lo = _split3(x)
    d = lambda p: jnp.dot(p, sel, preferred_element_type=F32)
    return d(hi) + d(mid) + d(lo)


def _dot_sel_lhs(sel, x):
    hi, mid, lo = _split3(x)
    d = lambda p: jnp.dot(sel, p, preferred_element_type=F32)
    return d(hi) + d(mid) + d(lo)


def _softplus(x):
    return jnp.maximum(x, 0.0) + jnp.log1p(jnp.exp(-jnp.abs(x)))


def _silu(x):
    return x * jax.nn.sigmoid(x)


def _rms(x, n):
    return x * lax.rsqrt(jnp.sum(x * x, axis=-1, keepdims=True) * (1.0 / n) + EPS)


def _conv3(x, prev_row, next_row, w_ref, bias):
    n = x.shape[0]
    row = lax.broadcasted_iota(jnp.int32, x.shape, 0)
    xm1 = jnp.where(row == 0, prev_row, pltpu.roll(x, 1, axis=0))
    xp1 = jnp.where(row == n - 1, next_row, pltpu.roll(x, n - 1, axis=0))
    y = xm1 * w_ref[0:1, :] + x * w_ref[1:2, :] + xp1 * w_ref[2:3, :]
    return y if bias is None else y + bias


def _block_diag(x, mask):
    return jnp.where(mask, jnp.concatenate([x] * HEADS, axis=0), 0.0).astype(BF16)


def _np_bf16(a):
    return jnp.asarray(np.asarray(a, np.float32), BF16)


def _head_expand(first_lane, width):
    e = np.zeros((LANES, HEADS * width), np.float32)
    for h in range(HEADS):
        e[first_lane + h, h * width:(h + 1) * width] = 1.0
    return e


def _tri_blocks(n, blk, upper):
    i = np.arange(n)
    same = (i[:, None] // blk) == (i[None, :] // blk)
    tri = (i[:, None] <= i[None, :]) if upper else (i[:, None] >= i[None, :])
    return (same & tri).astype(np.float32)


def _rot_last(w):
    r = w.shape[-1]
    xs = w.reshape(w.shape[:-1] + (2, 2, r // 4))
    return jnp.stack([-xs[..., 1, :], xs[..., 0, :]], axis=-2).reshape(w.shape)


def _swap_last(w):
    r = w.shape[-1]
    xs = w.reshape(w.shape[:-1] + (2, 2, r // 4))
    return jnp.stack([xs[..., 1, :], xs[..., 0, :]], axis=-2).reshape(w.shape)


def _rope_tables(seq_len, rot_dim):
    rows = seq_len // GRID_W
    row = jnp.repeat(jnp.arange(rows), GRID_W).astype(F32)
    col = jnp.tile(jnp.arange(GRID_W), rows).astype(F32)
    sec = rot_dim // 2
    inv_freq = ROPE_BASE ** (-jnp.arange(0, sec, 2, dtype=F32) / sec)
    ang_r = row[:, None] * inv_freq
    ang_c = col[:, None] * inv_freq
    ang = jnp.concatenate([ang_r, ang_r, ang_c, ang_c], axis=-1)
    return jnp.cos(ang), jnp.sin(ang)


def _pad_heads(a, axis, n_heads, real, padded):
    shp = list(a.shape)
    a = a.reshape(shp[:axis] + [n_heads, real] + shp[axis + 1:])
    pad = [(0, 0)] * a.ndim
    pad[axis + 1] = (0, padded - real)
    a = jnp.pad(a, pad)
    return a.reshape(shp[:axis] + [n_heads * padded] + shp[axis + 1:])


def _arrange_w_in(w):
    a0, b0, c0, d0 = 0, 352, 864, 1640
    zeros = lambda n: jnp.zeros((w.shape[0], n), w.dtype)
    cq, ckv, kr = w[:, a0:a0 + 192], w[:, a0 + 192:a0 + 320], w[:, a0 + 320:a0 + 352]
    bq, bk, bv = w[:, b0:b0 + 256], w[:, b0 + 256:b0 + 384], w[:, b0 + 384:b0 + 512]
    cz, cxbc, cdt = w[:, c0:c0 + 256], w[:, c0 + 256:c0 + 768], w[:, c0 + 768:c0 + 776]
    dqkv, dz, dab = w[:, d0:d0 + 768], w[:, d0 + 768:d0 + 1024], w[:, d0 + 1024:d0 + 1040]
    bq_rot = _rot_last(bq.reshape(-1, HEADS, HDIM)).reshape(-1, 256)
    bk_rot = _rot_last(bk.reshape(-1, B_KV_HEADS, HDIM)).reshape(-1, 128)
    small = jnp.concatenate([cdt, zeros(8), dab, zeros(32), kr, _rot_last(kr)], axis=1)
    cols = [cxbc, cq, zeros(64), dqkv, bq, bq_rot, cz, dz, ckv, small, bk, bk_rot, bv]
    out = jnp.concatenate(cols, axis=1)
    assert out.shape[1] == IN_COLS_PADDED
    return out.astype(BF16)


def _in_proj_kernel(x_ref, g_ref, w_ref, o_ref):
    h = (_rms(x_ref[...], D_MODEL) * g_ref[...]).astype(BF16)
    o_ref[...] = jnp.dot(h, w_ref[...], preferred_element_type=F32)


def _in_proj(x2d, gain, w):
    m = x2d.shape[0]
    tm = 256
    return pl.pallas_call(
        _in_proj_kernel,
        grid=(m // tm,),
        in_specs=[pl.BlockSpec((tm, D_MODEL), lambda i: (i, 0)),
                  pl.BlockSpec((1, D_MODEL), lambda i: (0, 0)),
                  pl.BlockSpec((D_MODEL, IN_COLS_PADDED), lambda i: (0, 0))],
        out_specs=pl.BlockSpec((tm, IN_COLS_PADDED), lambda i: (i, 0)),
        out_shape=jax.ShapeDtypeStruct((m, IN_COLS_PADDED), F32),
        compiler_params=_cparams("parallel"),
        name="in_proj",
    )(x2d, gain.reshape(1, -1), w)


def _mla_prep_kernel(cq_ref, ckv_ref, sm_ref, cqt_ref, sqt_ref, ckt_ref, skt_ref, gq_ref, gkv_ref,
                     w1_ref, w2_ref, wk_ref, wv_ref, q_ref, k_ref, v_ref):
    cqn = (_rms(cq_ref[...], A_Q_LORA) * gq_ref[...]).astype(BF16)
    q1 = jnp.dot(cqn, w1_ref[...], preferred_element_type=F32)
    q2 = jnp.dot(cqn, w2_ref[...], preferred_element_type=F32)
    kvn = (_rms(ckv_ref[...], A_KV_LORA) * gkv_ref[...]).astype(BF16)
    k1 = jnp.dot(kvn, wk_ref[...], preferred_element_type=F32)
    v1 = jnp.dot(kvn, wv_ref[...], preferred_element_type=F32)
    sm = sm_ref[...]
    k_rope = sm * ckt_ref[...] + pltpu.roll(sm, LANES - A_ROPE, axis=1) * skt_ref[...]
    cqt, sqt = cqt_ref[...], sqt_ref[...]
    for h in range(HEADS):
        sl = slice(LANES * h, LANES * (h + 1))
        q_ref[h] = (q1[:, sl] * cqt + q2[:, sl] * sqt).astype(BF16)
        k_ref[h] = (k1[:, sl] + k_rope).astype(BF16)
        v_ref[h] = v1[:, sl].astype(BF16)


def _mla_prep(p3, tabs, gq, gkv, w1, w2, wk, wv):
    bsz, seq, _ = p3.shape
    tm = min(512, seq)
    col = lambda off, w: pl.BlockSpec((None, tm, w), lambda b, i, o=off // w: (b, i, o))
    tab = pl.BlockSpec((tm, LANES), lambda b, i: (i, 0))
    full = lambda a: pl.BlockSpec(a.shape, lambda b, i: (0,) * a.ndim)
    head_out = pl.BlockSpec((None, HEADS, tm, LANES), lambda b, i: (b, 0, i, 0))
    shp = jax.ShapeDtypeStruct((bsz, HEADS, seq, LANES), BF16)
    return pl.pallas_call(
        _mla_prep_kernel,
        grid=(bsz, seq // tm),
        in_specs=[col(OFF_A_CQ, 256), col(OFF_A_CKV, 128), col(OFF_SMALL, 128),
                  tab, tab, tab, tab, full(gq), full(gkv), full(w1), full(w2), full(wk), full(wv)],
        out_specs=[head_out, head_out, head_out],
        out_shape=[shp, shp, shp],
        compiler_params=_cparams("parallel", "parallel"),
        name="mla_prep",
    )(p3, p3, p3, *tabs, gq, gkv, w1, w2, wk, wv)


def _head_slot(x, h, lo):
    grp = x[:, LANES * (h // 2):LANES * (h // 2 + 1)]
    if h % 2:
        grp = pltpu.roll(grp, HDIM, axis=1)
    return jnp.where(lo, grp, 0.0)


def _gqa_prep_kernel(q_ref, qr_ref, k_ref, kr_ref, v_ref, cos_ref, sin_ref, gq_ref, gqs_ref, gk_ref, gks_ref,
                     qo_ref, ko_ref, vo_ref):
    cos, sin = cos_ref[...], sin_ref[...]
    lo = lax.broadcasted_iota(jnp.int32, cos.shape, 1) < HDIM

    def normed_rope(x, xr, h, g, gs):
        xh, xrh = _head_slot(x, h, lo), _head_slot(xr, h, lo)
        r = lax.rsqrt(jnp.sum(xh * xh, axis=-1, keepdims=True) * (1.0 / HDIM) + EPS)
        return (r * (xh * (cos * g) + xrh * (sin * gs))).astype(BF16)

    q, qr, k, kr, v = q_ref[...], qr_ref[...], k_ref[...], kr_ref[...], v_ref[...]
    for h in range(HEADS):
        qo_ref[h] = normed_rope(q, qr, h, gq_ref[...], gqs_ref[...])
    for h in range(B_KV_HEADS):
        ko_ref[h] = normed_rope(k, kr, h, gk_ref[...], gks_ref[...])
        vo_ref[h] = _head_slot(v, h, lo).astype(BF16)


def _gqa_prep(p3, cos, sin, gq, gqs, gk, gks):
    bsz, seq, _ = p3.shape
    tm = min(512, seq)
    col = lambda off, w: pl.BlockSpec((None, tm, w), lambda b, i, o=off // w: (b, i, o))
    tab = pl.BlockSpec((tm, LANES), lambda b, i: (i, 0))
    vec = pl.BlockSpec((1, LANES), lambda b, i: (0, 0))
    out = lambda n: pl.BlockSpec((None, n, tm, LANES), lambda b, i: (b, 0, i, 0))
    shp = lambda n: jax.ShapeDtypeStruct((bsz, n, seq, LANES), BF16)
    return pl.pallas_call(
        _gqa_prep_kernel,
        grid=(bsz, seq // tm),
        in_specs=[col(OFF_B_Q, 256), col(OFF_B_QROT, 256), col(OFF_B_K, 128), col(OFF_B_KROT, 128),
                  col(OFF_B_V, 128), tab, tab, vec, vec, vec, vec],
        out_specs=[out(HEADS), out(B_KV_HEADS), out(B_KV_HEADS)],
        out_shape=[shp(HEADS), shp(B_KV_HEADS), shp(B_KV_HEADS)],
        compiler_params=_cparams("parallel", "parallel"),
        name="gqa_prep",
    )(p3, p3, p3, p3, p3, cos, sin, gq, gqs, gk, gks)


def _attn_kernel(q_ref, k_ref, v_ref, o_ref, *, tk, nk):
    q = q_ref[...]
    tq = q.shape[0]

    def body(c, carry):
        m, l, acc = carry
        start = pl.multiple_of(c * tk, tk)
        ks = k_ref[pl.ds(start, tk), :]
        vs = v_ref[pl.ds(start, tk), :]
        s = lax.dot_general(q, ks, (((1,), (1,)), ((), ())), preferred_element_type=F32)
        m_new = jnp.maximum(m, jnp.max(s, axis=-1, keepdims=True))
        alpha = jnp.exp2(m - m_new)
        p = jnp.exp2(s - m_new)
        l = alpha * l + jnp.sum(p, axis=-1, keepdims=True)
        acc = alpha * acc + jnp.dot(p.astype(BF16), vs, preferred_element_type=F32)
        return m_new, l, acc

    init = (jnp.full((tq, 1), -1e30, F32), jnp.zeros((tq, 1), F32), jnp.zeros((tq, LANES), F32))
    _, l, acc = lax.fori_loop(0, nk, body, init)
    o_ref[...] = (acc / l).astype(BF16)


def _attention(q, k, v):
    bsz, nh, seq, _ = q.shape
    rep = nh // k.shape[1]
    tq = min(512, seq)
    tk = min(512, seq)
    kern = functools.partial(_attn_kernel, tk=tk, nk=seq // tk)
    kv_spec = pl.BlockSpec((None, None, seq, LANES), lambda b, h, i: (b, h // rep, 0, 0))
    return pl.pallas_call(
        kern,
        grid=(bsz, nh, seq // tq),
        in_specs=[pl.BlockSpec((None, None, tq, LANES), lambda b, h, i: (b, h, i, 0)), kv_spec, kv_spec],
        out_specs=pl.BlockSpec((None, tq, LANES), lambda b, h, i: (b, i, h)),
        out_shape=jax.ShapeDtypeStruct((bsz, seq, nh * LANES), BF16),
        compiler_params=_cparams("parallel", "parallel", "arbitrary"),
        name="attention",
    )(q, k, v)


def _ssd_kernel(*refs, reverse, nc):
    if reverse:
        (xp_ref, x_ref, xn_ref, sm_ref, z_ref, yf_ref, cw_ref, cb_ref, dtb_ref, na_ref, tri_ref, e2_ref, ew_ref,
         smask_ref, dsk_ref, gout_ref, o_ref, s_ref) = refs
    else:
        (xp_ref, x_ref, xn_ref, sm_ref, cw_ref, cb_ref, dtb_ref, na_ref, tri_ref, e2_ref, ew_ref,
         smask_ref, o_ref, s_ref) = refs
    step = pl.program_id(1)
    chunk = (nc - 1 - step) if reverse else step
    q_len = C_CHUNK

    @pl.when(step == 0)
    def _():
        s_ref[...] = jnp.zeros_like(s_ref)

    x = x_ref[...]
    prev_row = xp_ref[SUBLANES - 1:SUBLANES, :] * jnp.where(chunk > 0, 1.0, 0.0)
    next_row = xn_ref[0:1, :] * jnp.where(chunk < nc - 1, 1.0, 0.0)
    act = _silu(_conv3(x, prev_row, next_row, cw_ref, cb_ref[...]))
    xs, bm, cm = act[:, :GROUP_W], act[:, GROUP_W:GROUP_W + LANES], act[:, GROUP_W + LANES:]

    dt_slot = _softplus(sm_ref[...] + dtb_ref[...])
    a_slot = dt_slot * na_ref[...]
    cum_slot = _dot_sel_lhs(tri_ref[...], a_slot)
    cum_w = _dot_sel_rhs(cum_slot, ew_ref[...])
    cum_2 = _dot_sel_rhs(cum_slot, e2_ref[...])
    dt_w = _dot_sel_rhs(dt_slot, ew_ref[...])
    edge = 0 if reverse else q_len - 1
    cum_edge = cum_w[edge:edge + 1, :]
    xdt = xs * dt_w
    xdt_end = xdt * jnp.exp(cum_edge - cum_w)

    lane = lax.broadcasted_iota(jnp.int32, (q_len, LANES), 1)
    ri = lax.broadcasted_iota(jnp.int32, (q_len, q_len), 0)
    ci = lax.broadcasted_iota(jnp.int32, (q_len, q_len), 1)
    causal = (ri <= ci) if reverse else (ri >= ci)
    cm_lo = jnp.where(lane < C_STATE, cm, 0.0)
    cm_hi = jnp.where(lane >= C_STATE, cm, 0.0)
    cb = _dot_nt(jnp.concatenate([cm_lo, cm_hi], axis=0), bm)
    lane_w = lax.broadcasted_iota(jnp.int32, (q_len, GROUP_W), 1)
    xdt_b = xdt.astype(BF16)
    y = jnp.zeros((q_len, GROUP_W), F32)
    for h in range(HEADS):
        a_bc = cum_2[:, LANES * h:LANES * (h + 1)]
        decay = jnp.exp(jnp.where(causal, a_bc - a_bc.T, -1e30))
        g = h // 2
        scores = cb[q_len * g:q_len * (g + 1), :] * decay
        yh = jnp.dot(scores.astype(BF16), xdt_b, preferred_element_type=F32)
        y = jnp.where((lane_w >= HDIM * h) & (lane_w < HDIM * (h + 1)), yh, y)

    s_in = s_ref[...]
    y = y + _dot(cm, s_in) * jnp.exp(cum_w)
    s_new = jnp.where(smask_ref[...] > 0, _dot_tn(bm, xdt_end), 0.0)
    s_ref[...] = s_in * jnp.exp(cum_edge) + s_new

    if reverse:
        yt = yf_ref[...] + y + xs * dsk_ref[...]
        gated = yt * _silu(z_ref[...])
        o_ref[...] = (_rms(gated, GROUP_W) * gout_ref[...]).astype(BF16)
    else:
        o_ref[...] = y


def _ssd_pass(p3, yf, z_params, conv_w, conv_b, dtb_slot, na_slot, reverse):
    bsz, seq, _ = p3.shape
    q_len = C_CHUNK
    nc = seq // q_len
    hb = q_len // SUBLANES
    nblk8 = seq // SUBLANES
    ch = (lambda s: nc - 1 - s) if reverse else (lambda s: s)
    d = 1 if reverse else 0
    tri = _np_bf16(_tri_blocks(q_len, q_len, upper=reverse))
    e2 = _np_bf16(_head_expand(SM_DT + HEADS * d, LANES))
    ew = _np_bf16(_head_expand(SM_DT + HEADS * d, HDIM))
    gi = np.arange(LANES)[:, None] // C_STATE
    hi = np.arange(GROUP_W)[None, :] // HDIM
    smask = jnp.asarray((gi == hi // 2).astype(np.float32))
    xbc_w = C_XBC
    x_spec = pl.BlockSpec((None, q_len, xbc_w), lambda b, s: (b, ch(s), OFF_C_XBC // xbc_w))
    xp_spec = pl.BlockSpec((None, SUBLANES, xbc_w), lambda b, s: (b, jnp.maximum(ch(s) * hb - 1, 0), 0))
    xn_spec = pl.BlockSpec((None, SUBLANES, xbc_w), lambda b, s: (b, jnp.minimum((ch(s) + 1) * hb, nblk8 - 1), 0))
    sm_spec = pl.BlockSpec((None, q_len, LANES), lambda b, s: (b, ch(s), OFF_SMALL // LANES))
    full = lambda a: pl.BlockSpec(a.shape, lambda b, s: (0,) * a.ndim)
    row_blk = pl.BlockSpec((None, q_len, GROUP_W), lambda b, s: (b, ch(s), 0))
    consts = [conv_w, conv_b, dtb_slot, na_slot, tri, e2, ew, smask]
    if reverse:
        dsk_w, gout = z_params
        z_spec = pl.BlockSpec((None, q_len, GROUP_W), lambda b, s: (b, ch(s), OFF_C_Z // GROUP_W))
        args = [p3, p3, p3, p3, p3, yf] + consts + [dsk_w, gout]
        in_specs = [xp_spec, x_spec, xn_spec, sm_spec, z_spec, row_blk] + [full(a) for a in consts] + [full(dsk_w), full(gout)]
        out_dtype = BF16
    else:
        args = [p3, p3, p3, p3] + consts
        in_specs = [xp_spec, x_spec, xn_spec, sm_spec] + [full(a) for a in consts]
        out_dtype = F32
    return pl.pallas_call(
        functools.partial(_ssd_kernel, reverse=reverse, nc=nc),
        grid=(bsz, nc),
        in_specs=in_specs,
        out_specs=row_blk,
        out_shape=jax.ShapeDtypeStruct((bsz, seq, GROUP_W), out_dtype),
        scratch_shapes=[pltpu.VMEM((LANES, GROUP_W), F32)],
        compiler_params=_cparams("parallel", "arbitrary"),
        name="ssd_bwd" if reverse else "ssd_fwd",
    )(*args)


def _delta_local_kernel(xp_ref, x_ref, xn_ref, sm_ref, cw_ref, na_ref, dtb_ref, ones_ref, trif_ref, trib_ref,
                        eg_ref, eb_ref, bd_ref, u_ref, w_ref, qk_ref, qd_ref, kd_ref, ge_ref, *, nblk, cb):
    blk = pl.program_id(1)
    q_len = D_CHUNK
    x = x_ref[...]
    prev_row = xp_ref[SUBLANES - 1:SUBLANES, :] * jnp.where(blk > 0, 1.0, 0.0)
    next_row = xn_ref[0:1, :] * jnp.where(blk < nblk - 1, 1.0, 0.0)
    act = _silu(_conv3(x, prev_row, next_row, cw_ref, None))
    ones_bd = ones_ref[...]

    def l2n(t):
        ss = _dot_sel_rhs(t * t, ones_bd)
        return t * lax.rsqrt(ss + 1e-6)

    q_all = l2n(act[:, :GROUP_W]) * (HDIM ** -0.5)
    k_all = l2n(act[:, GROUP_W:2 * GROUP_W])
    v_all = act[:, 2 * GROUP_W:]
    sm = sm_ref[...]
    beta_slot = jax.nn.sigmoid(sm)
    g_slot = na_ref[...] * _softplus(sm + dtb_ref[...])
    bd_mask = bd_ref[...] > 0

    lane = lax.broadcasted_iota(jnp.int32, (q_len, GROUP_W), 1) & (q_len - 1)
    row = lax.broadcasted_iota(jnp.int32, (q_len, GROUP_W), 0)
    eye_w = row == lane
    ones_q = jnp.ones((q_len, q_len), BF16)

    for d in range(2):
        tri = (trib_ref if d else trif_ref)[...]
        g_cum = _dot_sel_rhs(_dot_sel_lhs(tri, g_slot), eg_ref[d])
        beta_w = _dot_sel_rhs(beta_slot, eb_ref[d])
        incl = (row <= lane) if d else (row >= lane)
        strict = (row < lane) if d else (row > lane)
        edge = 0 if d else q_len - 1
        for c in range(cb):
            rs = slice(q_len * c, q_len * (c + 1))
            gw, bw = g_cum[rs], beta_w[rs]
            qc, kc, vc = q_all[rs], k_all[rs], v_all[rs]
            g_cols = _dot_sel_lhs(ones_q, jnp.where(eye_w, gw, 0.0))
            decay = jnp.exp(jnp.where(incl, gw - g_cols, -1e30))
            exp_g = jnp.exp(gw)
            g_edge = gw[edge:edge + 1, :]
            kb = kc * bw
            km = _block_diag(kc, bd_mask)
            kq = lax.dot_general(jnp.concatenate([kb, qc], axis=0).astype(BF16), km,
                                 (((1,), (1,)), ((), ())), preferred_element_type=F32)
            n_w = jnp.where(strict, -(kq[:q_len] * decay), 0.0)
            qk = kq[q_len:] * decay
            p_w = jnp.where(eye_w, 1.0, 0.0) + n_w
            m_w = _dot(n_w, _block_diag(n_w, bd_mask))
            for _ in range(4):
                r = _dot(jnp.concatenate([m_w, p_w], axis=0), _block_diag(m_w, bd_mask))
                m_w, p_w = r[:q_len], p_w + r[q_len:]
            t_w = p_w + _dot(p_w, _block_diag(m_w, bd_mask))
            rhs = jnp.concatenate([_block_diag(vc * bw, bd_mask), _block_diag(kb * exp_g, bd_mask)], axis=1)
            uw = _dot(t_w, rhs)
            u_ref[d, rs, :] = uw[:, :GROUP_W]
            w_ref[d, rs, :] = uw[:, GROUP_W:].astype(BF16)
            qk_ref[d, rs, :] = qk.astype(BF16)
            qd_ref[d, rs, :] = (qc * exp_g).astype(BF16)
            kd_ref[d, rs, :] = (kc * jnp.exp(g_edge - gw)).astype(BF16)
            ge_ref[d, SUBLANES * c:SUBLANES * (c + 1), :] = jnp.broadcast_to(jnp.exp(g_edge), (SUBLANES, GROUP_W))


def _delta_local(p3, conv_w, na_slot, dtb_slot):
    bsz, seq, _ = p3.shape
    q_len = D_CHUNK
    rows = min(128, seq)
    cb = rows // q_len
    nblk = seq // rows
    hb = rows // SUBLANES
    nblk8 = seq // SUBLANES
    ones_bd = _np_bf16(np.kron(np.eye(HEADS), np.ones((HDIM, HDIM))))
    trif = _np_bf16(_tri_blocks(rows, q_len, upper=False))
    trib = _np_bf16(_tri_blocks(rows, q_len, upper=True))
    eg = _np_bf16(np.stack([_head_expand(SM_AB + 8 + HEADS * d, HDIM) for d in range(2)]))
    eb = _np_bf16(np.stack([_head_expand(SM_AB + HEADS * d, HDIM) for d in range(2)]))
    bd = jnp.asarray(np.kron(np.eye(HEADS), np.ones((q_len, HDIM))).astype(np.float32))
    x_spec = pl.BlockSpec((None, rows, D_QKV), lambda b, i: (b, i, OFF_D_QKV // D_QKV))
    xp_spec = pl.BlockSpec((None, SUBLANES, D_QKV), lambda b, i: (b, jnp.maximum(i * hb - 1, 0), OFF_D_QKV // D_QKV))
    xn_spec = pl.BlockSpec((None, SUBLANES, D_QKV), lambda b, i: (b, jnp.minimum((i + 1) * hb, nblk8 - 1), OFF_D_QKV // D_QKV))
    sm_spec = pl.BlockSpec((None, rows, LANES), lambda b, i: (b, i, OFF_SMALL // LANES))
    full = lambda a: pl.BlockSpec(a.shape, lambda b, i: (0,) * a.ndim)
    consts = [conv_w, na_slot, dtb_slot, ones_bd, trif, trib, eg, eb, bd]
    out_blk = pl.BlockSpec((None, 2, rows, GROUP_W), lambda b, i: (b, 0, i, 0))
    ge_blk = pl.BlockSpec((None, 2, SUBLANES * cb, GROUP_W), lambda b, i: (b, 0, i, 0))
    shp = lambda dt: jax.ShapeDtypeStruct((bsz, 2, seq, GROUP_W), dt)
    return pl.pallas_call(
        functools.partial(_delta_local_kernel, nblk=nblk, cb=cb),
        grid=(bsz, nblk),
        in_specs=[xp_spec, x_spec, xn_spec, sm_spec] + [full(a) for a in consts],
        out_specs=[out_blk] * 5 + [ge_blk],
        out_shape=[shp(F32), shp(BF16), shp(BF16), shp(BF16), shp(BF16),
                   jax.ShapeDtypeStruct((bsz, 2, seq // q_len * SUBLANES, GROUP_W), F32)],
        compiler_params=_cparams("parallel", "parallel"),
        name="delta_local",
    )(p3, p3, p3, p3, *consts)


def _delta_scan_kernel(u_ref, w_ref, qk_ref, qd_ref, kd_ref, ge_ref, bd_ref, o_ref, s_ref):
    @pl.when(pl.program_id(2) == 0)
    def _():
        s_ref[...] = jnp.zeros_like(s_ref)

    q_len = D_CHUNK
    bd_mask = bd_ref[...] > 0
    s = s_ref[...]
    r = jnp.dot(jnp.concatenate([w_ref[...], qd_ref[...]], axis=0), s.astype(BF16), preferred_element_type=F32)
    v_new = u_ref[...] - r[:q_len]
    o_ref[...] = r[q_len:] + jnp.dot(qk_ref[...], _block_diag(v_new, bd_mask), preferred_element_type=F32)
    upd = lax.dot_general(kd_ref[...], v_new.astype(BF16), (((0,), (0,)), ((), ())), preferred_element_type=F32)
    s_ref[...] = s * ge_ref[0:1, :] + jnp.where(bd_mask, upd, 0.0)


def _delta_scan(u, w, qk, qd, kd, ge):
    bsz, _, seq, _ = u.shape
    q_len = D_CHUNK
    nc = seq // q_len
    bd = jnp.asarray(np.kron(np.eye(HEADS), np.ones((q_len, HDIM))).astype(np.float32))
    ch = lambda d, c: jnp.where(d == 0, c, nc - 1 - c)
    blk = pl.BlockSpec((None, None, q_len, GROUP_W), lambda b, d, c: (b, d, ch(d, c), 0))
    ge_blk = pl.BlockSpec((None, None, SUBLANES, GROUP_W), lambda b, d, c: (b, d, ch(d, c), 0))
    return pl.pallas_call(
        _delta_scan_kernel,
        grid=(bsz, 2, nc),
        in_specs=[blk] * 5 + [ge_blk, pl.BlockSpec(bd.shape, lambda b, d, c: (0, 0))],
        out_specs=blk,
        out_shape=jax.ShapeDtypeStruct((bsz, 2, seq, GROUP_W), F32),
        scratch_shapes=[pltpu.VMEM((GROUP_W, GROUP_W), F32)],
        compiler_params=_cparams("parallel", "parallel", "arbitrary"),
        name="delta_scan",
    )(u, w, qk, qd, kd, ge, bd)


def _out_proj_kernel(oa_ref, ob_ref, oc_ref, odf_ref, odb_ref, zd_ref, x_ref, wa_ref, wb_ref, wc_ref, wd_ref,
                     ga_ref, gb_ref, gd_ref, ones_ref, gp_ref, o_ref):
    oa = (_rms(oa_ref[...].astype(F32), GROUP_W) * ga_ref[...]).astype(BF16)
    ob = (_rms(ob_ref[...].astype(F32), GROUP_W) * gb_ref[...]).astype(BF16)
    od = odf_ref[...] + odb_ref[...]
    ms = _dot_sel_rhs(od * od, ones_ref[...]) * (1.0 / HDIM)
    od = od * lax.rsqrt(ms + EPS) * gd_ref[...] * _silu(zd_ref[...])
    acc = jnp.dot(oa, wa_ref[...], preferred_element_type=F32)
    acc += jnp.dot(ob, wb_ref[...], preferred_element_type=F32)
    acc += jnp.dot(oc_ref[...], wc_ref[...], preferred_element_type=F32)
    acc += jnp.dot(od.astype(BF16), wd_ref[...], preferred_element_type=F32)
    o_ref[...] = x_ref[...] + _rms(acc, D_MODEL) * gp_ref[...]


def _out_proj(oa, ob, oc, od, p3, x3, wa, wb, wc, wd, ga, gb, gd, gpost):
    bsz, seq, _ = x3.shape
    tm = min(512, seq)
    ones_bd = _np_bf16(np.kron(np.eye(HEADS), np.ones((HDIM, HDIM))))
    rows = lambda w: pl.BlockSpec((None, tm, w), lambda b, i: (b, i, 0))
    full = lambda a: pl.BlockSpec(a.shape, lambda b, i: (0,) * a.ndim)
    od_spec = lambda d: pl.BlockSpec((None, None, tm, GROUP_W), lambda b, i: (b, d, i, 0))
    zd_spec = pl.BlockSpec((None, tm, GROUP_W), lambda b, i: (b, i, OFF_D_Z // GROUP_W))
    consts = [wa, wb, wc, wd, ga, gb, gd, ones_bd, gpost]
    return pl.pallas_call(
        _out_proj_kernel,
        grid=(bsz, seq // tm),
        in_specs=[rows(HEADS * LANES), rows(HEADS * LANES), rows(GROUP_W), od_spec(0), od_spec(1), zd_spec,
                  rows(D_MODEL)] + [full(a) for a in consts],
        out_specs=rows(D_MODEL),
        out_shape=jax.ShapeDtypeStruct(x3.shape, F32),
        compiler_params=_cparams("parallel", "parallel"),
        name="out_proj",
    )(oa, ob, oc, od, od, p3, x3, *consts)


def _ffn_kernel(xp_ref, x_ref, xn_ref, gpre_ref, win_ref, cw_ref, cb_ref, wout_ref, gpost_ref, o_ref,
                h_ref, acc_ref, *, tiles_per_seq, nj):
    i, j = pl.program_id(0), pl.program_id(1)
    tm = x_ref.shape[0]
    halo = BF16_ROWS

    @pl.when(j == 0)
    def _():
        pos = lax.rem(i, tiles_per_seq)
        norm = lambda t: _rms(t, D_MODEL) * gpre_ref[...]
        h_ref[0:halo, :] = (norm(xp_ref[...]) * jnp.where(pos == 0, 0.0, 1.0)).astype(BF16)
        h_ref[halo:halo + tm, :] = norm(x_ref[...]).astype(BF16)
        h_ref[halo + tm:, :] = (norm(xn_ref[...]) * jnp.where(pos == tiles_per_seq - 1, 0.0, 1.0)).astype(BF16)
        acc_ref[...] = jnp.zeros_like(acc_ref)

    gu = jnp.dot(h_ref[...], win_ref[...], preferred_element_type=F32)
    n = gu.shape[0]
    body = slice(halo, halo + tm)
    conv = (pltpu.roll(gu, 1, axis=0)[body] * cw_ref[0:1, :] + gu[body] * cw_ref[1:2, :]
            + pltpu.roll(gu, n - 1, axis=0)[body] * cw_ref[2:3, :] + cb_ref[...])
    act = _silu(conv[:, :FF_TILE]) * conv[:, FF_TILE:]
    acc_ref[...] += jnp.dot(act.astype(BF16), wout_ref[...], preferred_element_type=F32)

    @pl.when(j == nj - 1)
    def _():
        o_ref[...] = x_ref[...] + _rms(acc_ref[...], D_MODEL) * gpost_ref[...]


def _ffn(x2d, seq, gpre, w_in, conv_w, conv_b, w_out, gpost):
    m = x2d.shape[0]
    tm = min(1024, seq)
    nj = D_FF // FF_TILE
    hb = tm // BF16_ROWS
    nblk = m // BF16_ROWS
    kern = functools.partial(_ffn_kernel, tiles_per_seq=seq // tm, nj=nj)
    vec = lambda a: pl.BlockSpec(a.shape, lambda i, j: (0, 0))
    return pl.pallas_call(
        kern,
        grid=(m // tm, nj),
        in_specs=[pl.BlockSpec((BF16_ROWS, D_MODEL), lambda i, j: (jnp.maximum(i * hb - 1, 0), 0)),
                  pl.BlockSpec((tm, D_MODEL), lambda i, j: (i, 0)),
                  pl.BlockSpec((BF16_ROWS, D_MODEL), lambda i, j: (jnp.minimum((i + 1) * hb, nblk - 1), 0)),
                  vec(gpre),
                  pl.BlockSpec((D_MODEL, 2 * FF_TILE), lambda i, j: (0, j)),
                  pl.BlockSpec((3, 2 * FF_TILE), lambda i, j: (0, j)),
                  pl.BlockSpec((1, 2 * FF_TILE), lambda i, j: (0, j)),
                  pl.BlockSpec((FF_TILE, D_MODEL), lambda i, j: (j, 0)),
                  vec(gpost)],
        out_specs=pl.BlockSpec((tm, D_MODEL), lambda i, j: (i, 0)),
        out_shape=jax.ShapeDtypeStruct(x2d.shape, F32),
        scratch_shapes=[pltpu.VMEM((tm + 2 * BF16_ROWS, D_MODEL), BF16), pltpu.VMEM((tm, D_MODEL), F32)],
        compiler_params=_cparams("parallel", "arbitrary"),
        name="conv_ffn",
    )(x2d, x2d, x2d, gpre, w_in, conv_w, conv_b, w_out, gpost)


def _pair_tiles(a):
    lead = a.shape[:-1]
    nj = D_FF // FF_TILE
    return a.reshape(lead + (2, nj, FF_TILE)).swapaxes(-3, -2).reshape(lead + (2 * D_FF,))


def _slot_vec(pairs):
    v = jnp.zeros((1, LANES), F32)
    for off, vals in pairs:
        v = v.at[0, off:off + vals.shape[0]].set(vals.astype(F32))
    return v


def _layer(x3, tabs_a, tabs_b, prm):
    bsz, seq, _ = x3.shape
    m = bsz * seq
    x2d = x3.reshape(m, D_MODEL)
    p3 = _in_proj(x2d, prm["pre_mix_norm"], _arrange_w_in(prm["w_in"])).reshape(bsz, seq, IN_COLS_PADDED)

    wuq = prm["a_w_uq"].reshape(A_Q_LORA, HEADS, A_NOPE + A_ROPE)
    rope_w = wuq[..., A_NOPE:]
    zpad = jnp.zeros((A_Q_LORA, HEADS, LANES - A_NOPE - A_ROPE), F32)
    w1 = jnp.concatenate([wuq[..., :A_NOPE], rope_w, zpad], axis=-1).reshape(A_Q_LORA, HEADS * LANES)
    w2 = jnp.concatenate([jnp.zeros_like(wuq[..., :A_NOPE]), _rot_last(rope_w), zpad], axis=-1).reshape(A_Q_LORA, HEADS * LANES)
    rpad = ((0, 256 - A_Q_LORA), (0, 0))
    w1, w2 = jnp.pad(w1, rpad).astype(BF16), jnp.pad(w2, rpad).astype(BF16)
    wukv = prm["a_w_ukv"].reshape(A_KV_LORA, HEADS, A_NOPE + HDIM)
    hpad = jnp.zeros((A_KV_LORA, HEADS, LANES - HDIM), F32)
    wk = jnp.concatenate([wukv[..., :A_NOPE], hpad], axis=-1).reshape(A_KV_LORA, HEADS * LANES).astype(BF16)
    wv = jnp.concatenate([wukv[..., A_NOPE:], hpad], axis=-1).reshape(A_KV_LORA, HEADS * LANES).astype(BF16)
    gq = jnp.pad(prm["a_q_norm"], (0, 256 - A_Q_LORA)).reshape(1, 256)
    gkv = prm["a_kv_norm"].reshape(1, A_KV_LORA)
    qa, ka, va = _mla_prep(p3, tabs_a, gq, gkv, w1, w2, wk, wv)
    oa = _attention(qa, ka, va)

    cos_b, sin_b = tabs_b
    c_b = (HDIM ** -0.5) * LOG2E
    pad64 = lambda g: jnp.pad(g, (0, LANES - HDIM)).reshape(1, LANES)
    qb, kb, vb = _gqa_prep(p3, cos_b, sin_b,
                           pad64(prm["b_q_norm"] * c_b), pad64(_swap_last(prm["b_q_norm"]) * c_b),
                           pad64(prm["b_k_norm"]), pad64(_swap_last(prm["b_k_norm"])))
    ob = _attention(qb, kb, vb)

    na_c = -jnp.exp(prm["c_a_log"].astype(F32)).reshape(-1)
    dtb_slot = _slot_vec([(SM_DT, prm["c_dt_bias"].reshape(-1))])
    na_slot = _slot_vec([(SM_DT, na_c)])
    conv_b = prm["c_conv_b"].reshape(1, -1)
    dsk_w = jnp.repeat(prm["c_d_skip"].astype(F32), HDIM).reshape(1, GROUP_W)
    yf = _ssd_pass(p3, None, None, prm["c_conv_w"], conv_b, dtb_slot, na_slot, reverse=False)
    oc = _ssd_pass(p3, yf, (dsk_w, prm["c_out_norm"].reshape(1, -1)), prm["c_conv_w"], conv_b, dtb_slot, na_slot,
                   reverse=True)

    na_d = -jnp.exp(prm["d_a_log"].astype(F32)).reshape(-1)
    d_na_slot = _slot_vec([(SM_AB + 8, na_d)])
    d_dtb_slot = _slot_vec([(SM_AB + 8, prm["d_dt_bias"].reshape(-1))])
    u, w, qk, qd, kd, ge = _delta_local(p3, prm["d_conv_w"], d_na_slot, d_dtb_slot)
    od = _delta_scan(u, w, qk, qd, kd, ge)

    wo = prm["w_out"]
    wa = _pad_heads(wo[0:256], 0, HEADS, HDIM, LANES).astype(BF16)
    wb = _pad_heads(wo[256:512], 0, HEADS, HDIM, LANES).astype(BF16)
    wc, wd = wo[512:768].astype(BF16), wo[768:1024].astype(BF16)
    ga = _pad_heads(prm["a_out_norm"], 0, HEADS, HDIM, LANES).reshape(1, -1)
    gb = _pad_heads(prm["b_out_norm"], 0, HEADS, HDIM, LANES).reshape(1, -1)
    gd = jnp.tile(prm["d_out_norm"], HEADS).reshape(1, -1)
    x3 = _out_proj(oa, ob, oc, od, p3, x3, wa, wb, wc, wd, ga, gb, gd, prm["post_mix_norm"].reshape(1, -1))

    x2d = _ffn(x3.reshape(m, D_MODEL), seq, prm["pre_ffn_norm"].reshape(1, -1),
               _pair_tiles(prm["f_w_in"]).astype(BF16), _pair_tiles(prm["f_conv_w"]),
               _pair_tiles(prm["f_conv_b"]).reshape(1, -1), prm["f_w_out"].astype(BF16),
               prm["post_ffn_norm"].reshape(1, -1))
    return x2d.reshape(bsz, seq, D_MODEL)


def _attn_tables(seq):
    cos_a, sin_a = _rope_tables(seq, A_ROPE)
    cos_b, sin_b = _rope_tables(seq, HDIM)
    c_a = ((A_NOPE + A_ROPE) ** -0.5) * LOG2E
    ones = jnp.ones((seq, A_NOPE), F32)
    z64 = jnp.zeros((seq, A_NOPE), F32)
    z32 = jnp.zeros((seq, LANES - A_NOPE - A_ROPE), F32)
    cq = c_a * jnp.concatenate([ones, cos_a, z32], axis=1)
    sq = c_a * jnp.concatenate([z64, sin_a, z32], axis=1)
    ck = jnp.concatenate([z64, cos_a, z32], axis=1)
    sk = jnp.concatenate([z64, sin_a, z32], axis=1)
    padb = lambda t: jnp.pad(t, ((0, 0), (0, LANES - HDIM)))
    return (cq, sq, ck, sk), (padb(cos_b), padb(sin_b))


def kernel(x, pre_mix_norm, w_in, a_q_norm, a_w_uq, a_kv_norm, a_w_ukv, a_out_norm, b_q_norm, b_k_norm, b_out_norm, c_conv_w, c_conv_b, c_a_log, c_dt_bias, c_d_skip, c_out_norm, d_conv_w, d_a_log, d_dt_bias, d_out_norm, w_out, post_mix_norm, pre_ffn_norm, f_w_in, f_conv_w, f_conv_b, f_w_out, post_ffn_norm):
    params = dict(pre_mix_norm=pre_mix_norm, w_in=w_in, a_q_norm=a_q_norm, a_w_uq=a_w_uq, a_kv_norm=a_kv_norm,
                  a_w_ukv=a_w_ukv, a_out_norm=a_out_norm, b_q_norm=b_q_norm, b_k_norm=b_k_norm,
                  b_out_norm=b_out_norm, c_conv_w=c_conv_w, c_conv_b=c_conv_b, c_a_log=c_a_log,
                  c_dt_bias=c_dt_bias, c_d_skip=c_d_skip, c_out_norm=c_out_norm, d_conv_w=d_conv_w,
                  d_a_log=d_a_log, d_dt_bias=d_dt_bias, d_out_norm=d_out_norm, w_out=w_out,
                  post_mix_norm=post_mix_norm, pre_ffn_norm=pre_ffn_norm, f_w_in=f_w_in, f_conv_w=f_conv_w,
                  f_conv_b=f_conv_b, f_w_out=f_w_out, post_ffn_norm=post_ffn_norm)
    tabs_a, tabs_b = _attn_tables(x.shape[1])
    for layer in range(w_in.shape[0]):
        x = _layer(x, tabs_a, tabs_b, {k: v[layer] for k, v in params.items()})
    return x
```

```python
import functools
import math

import numpy as np
import jax
import jax.numpy as jnp
from jax import lax
from jax.experimental import pallas as pl
from jax.experimental.pallas import tpu as pltpu

F32 = jnp.float32
BF16 = jnp.bfloat16

LANES = 128
SUBLANES = 8
BF16_ROWS = 16
VMEM_LIMIT = 56 * 1024 * 1024

EPS = 1e-6
ROPE_BASE = 10000.0
GRID_W = 64
D_MODEL = 1024
GROUP_W = 256
HEADS = 4
HDIM = 64

A_NOPE, A_ROPE, A_Q_LORA, A_KV_LORA = 64, 32, 192, 128
B_KV_HEADS = 2
C_STATE, C_CHUNK, C_XBC = 64, 128, 512
D_CHUNK, D_QKV = 64, 768
D_FF = 2816
FF_TILE = 256

OFF_C_XBC, OFF_A_CQ, OFF_D_QKV = 0, 512, 768
OFF_B_Q, OFF_B_QROT, OFF_C_Z, OFF_D_Z = 1536, 1792, 2048, 2304
OFF_A_CKV, OFF_SMALL, OFF_B_K, OFF_B_KROT, OFF_B_V = 2560, 2688, 2816, 2944, 3072
IN_COLS_PADDED = 3200
SM_DT, SM_AB, SM_KR, SM_KRROT = 0, 16, 64, 96

LOG2E = math.log2(math.e)


def _cparams(*sem):
    return pltpu.CompilerParams(dimension_semantics=sem, vmem_limit_bytes=VMEM_LIMIT)


def _dot(a, b):
    return jnp.dot(a.astype(BF16), b.astype(BF16), preferred_element_type=F32)


def _dot_nt(a, b):
    return lax.dot_general(a.astype(BF16), b.astype(BF16), (((1,), (1,)), ((), ())),
                           preferred_element_type=F32)


def _dot_tn(a, b):
    return lax.dot_general(a.astype(BF16), b.astype(BF16), (((0,), (0,)), ((), ())),
                           preferred_element_type=F32)


def _split3(x):
    hi = x.astype(BF16)
    r1 = x - hi.astype(F32)
    mid = r1.astype(BF16)
    lo = (r1 - mid.astype(F32)).astype(BF16)
    return hi, mid, lo


def _dot_sel_rhs(x, sel):
    hi, mid, lo = _split3(x)
    d = lambda p: jnp.dot(p, sel, preferred_element_type=F32)
    return d(hi) + d(mid) + d(lo)


def _dot_sel_lhs(sel, x):
    hi, mid, lo = _split3(x)
    d = lambda p: jnp.dot(sel, p, preferred_element_type=F32)
    return d(hi) + d(mid) + d(lo)


def _softplus(x):
    return jnp.maximum(x, 0.0) + jnp.log1p(jnp.exp(-jnp.abs(x)))


def _silu(x):
    return x * jax.nn.sigmoid(x)


def _rms(x, n):
    return x * lax.rsqrt(jnp.sum(x * x, axis=-1, keepdims=True) * (1.0 / n) + EPS)


def _conv3(x, prev_row, next_row, w_ref, bias):
    n = x.shape[0]
    row = lax.broadcasted_iota(jnp.int32, x.shape, 0)
    xm1 = jnp.where(row == 0, prev_row, pltpu.roll(x, 1, axis=0))
    xp1 = jnp.where(row == n - 1, next_row, pltpu.roll(x, n - 1, axis=0))
    y = xm1 * w_ref[0:1, :] + x * w_ref[1:2, :] + xp1 * w_ref[2:3, :]
    return y if bias is None else y + bias


def _block_diag(x, mask):
    return jnp.where(mask, jnp.concatenate([x] * HEADS, axis=0), 0.0).astype(BF16)


def _np_bf16(a):
    return jnp.asarray(np.asarray(a, np.float32), BF16)


def _head_expand(first_lane, width):
    e = np.zeros((LANES, HEADS * width), np.float32)
    for h in range(HEADS):
        e[first_lane + h, h * width:(h + 1) * width] = 1.0
    return e


def _tri_blocks(n, blk, upper):
    i = np.arange(n)
    same = (i[:, None] // blk) == (i[None, :] // blk)
    tri = (i[:, None] <= i[None, :]) if upper else (i[:, None] >= i[None, :])
    return (same & tri).astype(np.float32)


def _rot_last(w):
    r = w.shape[-1]
    xs = w.reshape(w.shape[:-1] + (2, 2, r // 4))
    return jnp.stack([-xs[..., 1, :], xs[..., 0, :]], axis=-2).reshape(w.shape)


def _swap_last(w):
    r = w.shape[-1]
    xs = w.reshape(w.shape[:-1] + (2, 2, r // 4))
    return jnp.stack([xs[..., 1, :], xs[..., 0, :]], axis=-2).reshape(w.shape)


def _rope_tables(seq_len, rot_dim):
    rows = seq_len // GRID_W
    row = jnp.repeat(jnp.arange(rows), GRID_W).astype(F32)
    col = jnp.tile(jnp.arange(GRID_W), rows).astype(F32)
    sec = rot_dim // 2
    inv_freq = ROPE_BASE ** (-jnp.arange(0, sec, 2, dtype=F32) / sec)
    ang_r = row[:, None] * inv_freq
    ang_c = col[:, None] * inv_freq
    ang = jnp.concatenate([ang_r, ang_r, ang_c, ang_c], axis=-1)
    return jnp.cos(ang), jnp.sin(ang)


def _pad_heads(a, axis, n_heads, real, padded):
    shp = list(a.shape)
    a = a.reshape(shp[:axis] + [n_heads, real] + shp[axis + 1:])
    pad = [(0, 0)] * a.ndim
    pad[axis + 1] = (0, padded - real)
    a = jnp.pad(a, pad)
    return a.reshape(shp[:axis] + [n_heads * padded] + shp[axis + 1:])


def _arrange_w_in(w):
    a0, b0, c0, d0 = 0, 352, 864, 1640
    zeros = lambda n: jnp.zeros((w.shape[0], n), w.dtype)
    cq, ckv, kr = w[:, a0:a0 + 192], w[:, a0 + 192:a0 + 320], w[:, a0 + 320:a0 + 352]
    bq, bk, bv = w[:, b0:b0 + 256], w[:, b0 + 256:b0 + 384], w[:, b0 + 384:b0 + 512]
    cz, cxbc, cdt = w[:, c0:c0 + 256], w[:, c0 + 256:c0 + 768], w[:, c0 + 768:c0 + 776]
    dqkv, dz, dab = w[:, d0:d0 + 768], w[:, d0 + 768:d0 + 1024], w[:, d0 + 1024:d0 + 1040]
    bq_rot = _rot_last(bq.reshape(-1, HEADS, HDIM)).reshape(-1, 256)
    bk_rot = _rot_last(bk.reshape(-1, B_KV_HEADS, HDIM)).reshape(-1, 128)
    small = jnp.concatenate([cdt, zeros(8), dab, zeros(32), kr, _rot_last(kr)], axis=1)
    cols = [cxbc, cq, zeros(64), dqkv, bq, bq_rot, cz, dz, ckv, small, bk, bk_rot, bv]
    out = jnp.concatenate(cols, axis=1)
    assert out.shape[1] == IN_COLS_PADDED
    return out.astype(BF16)


def _in_proj_kernel(x_ref, g_ref, w_ref, o_ref):
    h = (_rms(x_ref[...], D_MODEL) * g_ref[...]).astype(BF16)
    o_ref[...] = jnp.dot(h, w_ref[...], preferred_element_type=F32)


def _in_proj(x2d, gain, w):
    m = x2d.shape[0]
    tm = 256
    return pl.pallas_call(
        _in_proj_kernel,
        grid=(m // tm,),
        in_specs=[pl.BlockSpec((tm, D_MODEL), lambda i: (i, 0)),
                  pl.BlockSpec((1, D_MODEL), lambda i: (0, 0)),
                  pl.BlockSpec((D_MODEL, IN_COLS_PADDED), lambda i: (0, 0))],
        out_specs=pl.BlockSpec((tm, IN_COLS_PADDED), lambda i: (i, 0)),
        out_shape=jax.ShapeDtypeStruct((m, IN_COLS_PADDED), F32),
        compiler_params=_cparams("parallel"),
        name="in_proj",
    )(x2d, gain.reshape(1, -1), w)


def _mla_prep_kernel(cq_ref, ckv_ref, sm_ref, cqt_ref, sqt_ref, ckt_ref, skt_ref, gq_ref, gkv_ref,
                     w1_ref, w2_ref, wk_ref, wv_ref, q_ref, k_ref, v_ref):
    cqn = (_rms(cq_ref[...], A_Q_LORA) * gq_ref[...]).astype(BF16)
    q1 = jnp.dot(cqn, w1_ref[...], preferred_element_type=F32)
    q2 = jnp.dot(cqn, w2_ref[...], preferred_element_type=F32)
    kvn = (_rms(ckv_ref[...], A_KV_LORA) * gkv_ref[...]).astype(BF16)
    k1 = jnp.dot(kvn, wk_ref[...], preferred_element_type=F32)
    v1 = jnp.dot(kvn, wv_ref[...], preferred_element_type=F32)
    sm = sm_ref[...]
    k_rope = sm * ckt_ref[...] + pltpu.roll(sm, LANES - A_ROPE, axis=1) * skt_ref[...]
    cqt, sqt = cqt_ref[...], sqt_ref[...]
    ones_lane = lax.broadcasted_iota(jnp.int32, sm.shape, 1) == HDIM
    for h in range(HEADS):
        sl = slice(LANES * h, LANES * (h + 1))
        q_ref[h] = (q1[:, sl] * cqt + q2[:, sl] * sqt).astype(BF16)
        k_ref[h] = (k1[:, sl] + k_rope).astype(BF16)
        v_ref[h] = jnp.where(ones_lane, 1.0, v1[:, sl]).astype(BF16)


def _mla_prep(p3, tabs, gq, gkv, w1, w2, wk, wv):
    bsz, seq, _ = p3.shape
    tm = min(512, seq)
    col = lambda off, w: pl.BlockSpec((None, tm, w), lambda b, i, o=off // w: (b, i, o))
    tab = pl.BlockSpec((tm, LANES), lambda b, i: (i, 0))
    full = lambda a: pl.BlockSpec(a.shape, lambda b, i: (0,) * a.ndim)
    head_out = pl.BlockSpec((None, HEADS, tm, LANES), lambda b, i: (b, 0, i, 0))
    shp = jax.ShapeDtypeStruct((bsz, HEADS, seq, LANES), BF16)
    return pl.pallas_call(
        _mla_prep_kernel,
        grid=(bsz, seq // tm),
        in_specs=[col(OFF_A_CQ, 256), col(OFF_A_CKV, 128), col(OFF_SMALL, 128),
                  tab, tab, tab, tab, full(gq), full(gkv), full(w1), full(w2), full(wk), full(wv)],
        out_specs=[head_out, head_out, head_out],
        out_shape=[shp, shp, shp],
        compiler_params=_cparams("parallel", "parallel"),
        name="mla_prep",
    )(p3, p3, p3, *tabs, gq, gkv, w1, w2, wk, wv)


def _head_slot(x, h, lo):
    grp = x[:, LANES * (h // 2):LANES * (h // 2 + 1)]
    if h % 2:
        grp = pltpu.roll(grp, HDIM, axis=1)
    return jnp.where(lo, grp, 0.0)


def _gqa_prep_kernel(q_ref, qr_ref, k_ref, kr_ref, v_ref, cos_ref, sin_ref, gq_ref, gqs_ref, gk_ref, gks_ref,
                     qo_ref, ko_ref, vo_ref):
    cos, sin = cos_ref[...], sin_ref[...]
    lane = lax.broadcasted_iota(jnp.int32, cos.shape, 1)
    lo = lane < HDIM
    ones_lane = lane == HDIM

    def normed_rope(x, xr, h, g, gs):
        xh, xrh = _head_slot(x, h, lo), _head_slot(xr, h, lo)
        r = lax.rsqrt(jnp.sum(xh * xh, axis=-1, keepdims=True) * (1.0 / HDIM) + EPS)
        return (r * (xh * (cos * g) + xrh * (sin * gs))).astype(BF16)

    q, qr, k, kr, v = q_ref[...], qr_ref[...], k_ref[...], kr_ref[...], v_ref[...]
    for h in range(HEADS):
        qo_ref[h] = normed_rope(q, qr, h, gq_ref[...], gqs_ref[...])
    for h in range(B_KV_HEADS):
        ko_ref[h] = normed_rope(k, kr, h, gk_ref[...], gks_ref[...])
        vo_ref[h] = jnp.where(ones_lane, 1.0, _head_slot(v, h, lo)).astype(BF16)


def _gqa_prep(p3, cos, sin, gq, gqs, gk, gks):
    bsz, seq, _ = p3.shape
    tm = min(512, seq)
    col = lambda off, w: pl.BlockSpec((None, tm, w), lambda b, i, o=off // w: (b, i, o))
    tab = pl.BlockSpec((tm, LANES), lambda b, i: (i, 0))
    vec = pl.BlockSpec((1, LANES), lambda b, i: (0, 0))
    out = lambda n: pl.BlockSpec((None, n, tm, LANES), lambda b, i: (b, 0, i, 0))
    shp = lambda n: jax.ShapeDtypeStruct((bsz, n, seq, LANES), BF16)
    return pl.pallas_call(
        _gqa_prep_kernel,
        grid=(bsz, seq // tm),
        in_specs=[col(OFF_B_Q, 256), col(OFF_B_QROT, 256), col(OFF_B_K, 128), col(OFF_B_KROT, 128),
                  col(OFF_B_V, 128), tab, tab, vec, vec, vec, vec],
        out_specs=[out(HEADS), out(B_KV_HEADS), out(B_KV_HEADS)],
        out_shape=[shp(HEADS), shp(B_KV_HEADS), shp(B_KV_HEADS)],
        compiler_params=_cparams("parallel", "parallel"),
        name="gqa_prep",
    )(p3, p3, p3, p3, p3, cos, sin, gq, gqs, gk, gks)


def _attn_kernel(q_ref, k_ref, v_ref, o_ref, s0_ref, s1_ref, p0_ref, p1_ref, acc_ref, *, tk, nk):
    q = q_ref[...]
    tq = q.shape[0]

    s_refs, p_refs = (s0_ref, s1_ref), (p0_ref, p1_ref)

    def scores(c):
        s_refs[c % 2][...] = lax.dot_general(q, k_ref[tk * c:tk * (c + 1), :], (((1,), (1,)), ((), ())),
                                             preferred_element_type=F32)

    def probs(c, m):
        s = s_refs[c % 2][...]
        m_new = jnp.maximum(m, jnp.max(s, axis=-1, keepdims=True))
        p_refs[c % 2][...] = jnp.exp2(s - m_new).astype(BF16)
        return jnp.exp2(m - m_new), m_new

    def accumulate(c, alpha):
        pv = jnp.dot(p_refs[c % 2][...], v_ref[tk * c:tk * (c + 1), :], preferred_element_type=F32)
        acc_ref[...] = pv if c == 0 else alpha * acc_ref[...] + pv

    m = jnp.full((tq, 1), -1e30, F32)
    alphas = {}
    scores(0)
    for c in range(nk):
        if c + 1 < nk:
            scores(c + 1)
        alphas[c], m = probs(c, m)
        if c >= 1:
            accumulate(c - 1, alphas[c - 1])
    accumulate(nk - 1, alphas[nk - 1])
    acc = acc_ref[...]
    lane = lax.broadcasted_iota(jnp.int32, acc.shape, 1)
    o_ref[...] = jnp.where(lane < HDIM, acc / acc[:, HDIM:HDIM + 1], 0.0).astype(BF16)


def _attention(q, k, v):
    bsz, nh, seq, _ = q.shape
    rep = nh // k.shape[1]
    tq = min(1024, seq)
    tk = min(512, seq // 2)
    nk = seq // tk
    kern = functools.partial(_attn_kernel, tk=tk, nk=nk)
    kv_spec = pl.BlockSpec((None, None, seq, LANES), lambda b, h, i: (b, h // rep, 0, 0))
    return pl.pallas_call(
        kern,
        grid=(bsz, nh, seq // tq),
        in_specs=[pl.BlockSpec((None, None, tq, LANES), lambda b, h, i: (b, h, i, 0)), kv_spec, kv_spec],
        out_specs=pl.BlockSpec((None, tq, LANES), lambda b, h, i: (b, i, h)),
        out_shape=jax.ShapeDtypeStruct((bsz, seq, nh * LANES), BF16),
        scratch_shapes=[pltpu.VMEM((tq, tk), F32), pltpu.VMEM((tq, tk), F32),
                        pltpu.VMEM((tq, tk), BF16), pltpu.VMEM((tq, tk), BF16),
                        pltpu.VMEM((tq, LANES), F32)],
        compiler_params=_cparams("parallel", "parallel", "arbitrary"),
        name="attention",
    )(q, k, v)


def _ssd_kernel(*refs, reverse, nc):
    if reverse:
        (xp_ref, x_ref, xn_ref, sm_ref, z_ref, yf_ref, cw_ref, cb_ref, dtb_ref, na_ref, tri_ref, e2_ref, ew_ref,
         smask_ref, dsk_ref, gout_ref, o_ref, s_ref) = refs
    else:
        (xp_ref, x_ref, xn_ref, sm_ref, cw_ref, cb_ref, dtb_ref, na_ref, tri_ref, e2_ref, ew_ref,
         smask_ref, o_ref, s_ref) = refs
    step = pl.program_id(1)
    chunk = (nc - 1 - step) if reverse else step
    q_len = C_CHUNK

    @pl.when(step == 0)
    def _():
        s_ref[...] = jnp.zeros_like(s_ref)

    x = x_ref[...]
    prev_row = xp_ref[SUBLANES - 1:SUBLANES, :] * jnp.where(chunk > 0, 1.0, 0.0)
    next_row = xn_ref[0:1, :] * jnp.where(chunk < nc - 1, 1.0, 0.0)
    act = _silu(_conv3(x, prev_row, next_row, cw_ref, cb_ref[...]))
    xs, bm, cm = act[:, :GROUP_W], act[:, GROUP_W:GROUP_W + LANES], act[:, GROUP_W + LANES:]

    dt_slot = _softplus(sm_ref[...] + dtb_ref[...])
    a_slot = dt_slot * na_ref[...]
    cum_slot = _dot_sel_lhs(tri_ref[...], a_slot)
    cum_w = _dot_sel_rhs(cum_slot, ew_ref[...])
    cum_2 = _dot_sel_rhs(cum_slot, e2_ref[...])
    dt_w = _dot_sel_rhs(dt_slot, ew_ref[...])
    edge = 0 if reverse else q_len - 1
    cum_edge = cum_w[edge:edge + 1, :]
    xdt = xs * dt_w
    xdt_end = xdt * jnp.exp(cum_edge - cum_w)

    lane = lax.broadcasted_iota(jnp.int32, (q_len, LANES), 1)
    ri = lax.broadcasted_iota(jnp.int32, (q_len, q_len), 0)
    ci = lax.broadcasted_iota(jnp.int32, (q_len, q_len), 1)
    causal = (ri <= ci) if reverse else (ri >= ci)
    cm_lo = jnp.where(lane < C_STATE, cm, 0.0)
    cm_hi = jnp.where(lane >= C_STATE, cm, 0.0)
    cb = _dot_nt(jnp.concatenate([cm_lo, cm_hi], axis=0), bm)
    lane_w = lax.broadcasted_iota(jnp.int32, (q_len, GROUP_W), 1)
    xdt_b = xdt.astype(BF16)
    y = jnp.zeros((q_len, GROUP_W), F32)
    for h in range(HEADS):
        a_bc = cum_2[:, LANES * h:LANES * (h + 1)]
        decay = jnp.exp(jnp.where(causal, a_bc - a_bc.T, -1e30))
        g = h // 2
        scores = cb[q_len * g:q_len * (g + 1), :] * decay
        yh = jnp.dot(scores.astype(BF16), xdt_b, preferred_element_type=F32)
        y = jnp.where((lane_w >= HDIM * h) & (lane_w < HDIM * (h + 1)), yh, y)

    s_in = s_ref[...]
    y = y + _dot(cm, s_in) * jnp.exp(cum_w)
    s_new = jnp.where(smask_ref[...] > 0, _dot_tn(bm, xdt_end), 0.0)
    s_ref[...] = s_in * jnp.exp(cum_edge) + s_new

    if reverse:
        yt = yf_ref[...] + y + xs * dsk_ref[...]
        gated = yt * _silu(z_ref[...])
        o_ref[...] = (_rms(gated, GROUP_W) * gout_ref[...]).astype(BF16)
    else:
        o_ref[...] = y


def _ssd_pass(p3, yf, z_params, conv_w, conv_b, dtb_slot, na_slot, reverse):
    bsz, seq, _ = p3.shape
    q_len = C_CHUNK
    nc = seq // q_len
    hb = q_len // SUBLANES
    nblk8 = seq // SUBLANES
    ch = (lambda s: nc - 1 - s) if reverse else (lambda s: s)
    d = 1 if reverse else 0
    tri = _np_bf16(_tri_blocks(q_len, q_len, upper=reverse))
    e2 = _np_bf16(_head_expand(SM_DT + HEADS * d, LANES))
    ew = _np_bf16(_head_expand(SM_DT + HEADS * d, HDIM))
    gi = np.arange(LANES)[:, None] // C_STATE
    hi = np.arange(GROUP_W)[None, :] // HDIM
    smask = jnp.asarray((gi == hi // 2).astype(np.float32))
    xbc_w = C_XBC
    x_spec = pl.BlockSpec((None, q_len, xbc_w), lambda b, s: (b, ch(s), OFF_C_XBC // xbc_w))
    xp_spec = pl.BlockSpec((None, SUBLANES, xbc_w), lambda b, s: (b, jnp.maximum(ch(s) * hb - 1, 0), 0))
    xn_spec = pl.BlockSpec((None, SUBLANES, xbc_w), lambda b, s: (b, jnp.minimum((ch(s) + 1) * hb, nblk8 - 1), 0))
    sm_spec = pl.BlockSpec((None, q_len, LANES), lambda b, s: (b, ch(s), OFF_SMALL // LANES))
    full = lambda a: pl.BlockSpec(a.shape, lambda b, s: (0,) * a.ndim)
    row_blk = pl.BlockSpec((None, q_len, GROUP_W), lambda b, s: (b, ch(s), 0))
    consts = [conv_w, conv_b, dtb_slot, na_slot, tri, e2, ew, smask]
    if reverse:
        dsk_w, gout = z_params
        z_spec = pl.BlockSpec((None, q_len, GROUP_W), lambda b, s: (b, ch(s), OFF_C_Z // GROUP_W))
        args = [p3, p3, p3, p3, p3, yf] + consts + [dsk_w, gout]
        in_specs = [xp_spec, x_spec, xn_spec, sm_spec, z_spec, row_blk] + [full(a) for a in consts] + [full(dsk_w), full(gout)]
        out_dtype = BF16
    else:
        args = [p3, p3, p3, p3] + consts
        in_specs = [xp_spec, x_spec, xn_spec, sm_spec] + [full(a) for a in consts]
        out_dtype = F32
    return pl.pallas_call(
        functools.partial(_ssd_kernel, reverse=reverse, nc=nc),
        grid=(bsz, nc),
        in_specs=in_specs,
        out_specs=row_blk,
        out_shape=jax.ShapeDtypeStruct((bsz, seq, GROUP_W), out_dtype),
        scratch_shapes=[pltpu.VMEM((LANES, GROUP_W), F32)],
        compiler_params=_cparams("parallel", "arbitrary"),
        name="ssd_bwd" if reverse else "ssd_fwd",
    )(*args)


def _delta_local_kernel(xp_ref, x_ref, xn_ref, sm_ref, cw_ref, na_ref, dtb_ref, ones_ref, trif_ref, trib_ref,
                        eg_ref, eb_ref, bd_ref, u_ref, w_ref, qk_ref, qd_ref, kd_ref, ge_ref, *, nblk, cb):
    blk = pl.program_id(1)
    q_len = D_CHUNK
    x = x_ref[...]
    prev_row = xp_ref[SUBLANES - 1:SUBLANES, :] * jnp.where(blk > 0, 1.0, 0.0)
    next_row = xn_ref[0:1, :] * jnp.where(blk < nblk - 1, 1.0, 0.0)
    act = _silu(_conv3(x, prev_row, next_row, cw_ref, None))
    ones_bd = ones_ref[...]

    def l2n(t):
        ss = _dot_sel_rhs(t * t, ones_bd)
        return t * lax.rsqrt(ss + 1e-6)

    q_all = l2n(act[:, :GROUP_W]) * (HDIM ** -0.5)
    k_all = l2n(act[:, GROUP_W:2 * GROUP_W])
    v_all = act[:, 2 * GROUP_W:]
    sm = sm_ref[...]
    beta_slot = jax.nn.sigmoid(sm)
    g_slot = na_ref[...] * _softplus(sm + dtb_ref[...])
    bd_mask = bd_ref[...] > 0

    lane = lax.broadcasted_iota(jnp.int32, (q_len, GROUP_W), 1) & (q_len - 1)
    row = lax.broadcasted_iota(jnp.int32, (q_len, GROUP_W), 0)
    eye_w = jnp.where(row == lane, 1.0, 0.0)
    incl = (row >= lane, row <= lane)
    strict = (row > lane, row < lane)
    tri = (trif_ref[...], trib_ref[...])
    g_w = [_dot_sel_rhs(g_slot, eg_ref[d]) for d in range(2)]
    beta_w = [_dot_sel_rhs(beta_slot, eb_ref[d]) for d in range(2)]

    units = [(d, c) for d in range(2) for c in range(cb)]
    rows_of = lambda c: slice(q_len * c, q_len * (c + 1))
    seg, g_cum = [], []
    for d, c in units:
        gu = g_w[d][rows_of(c)]
        sg = _dot_sel_lhs(tri[d], jnp.concatenate([jnp.where(strict[d], gu, 0.0), gu], axis=1))
        seg.append(sg[:, :GROUP_W])
        g_cum.append(sg[:, GROUP_W:])
    kq = []
    for d, c in units:
        kc = k_all[rows_of(c)]
        lhs = jnp.concatenate([kc * beta_w[d][rows_of(c)], q_all[rows_of(c)]], axis=0)
        kq.append(_dot_nt(lhs, _block_diag(kc, bd_mask)))
    decay = [jnp.exp(jnp.where(incl[d], seg[i], -1e30)) for i, (d, c) in enumerate(units)]
    n_w = [jnp.where(strict[d], -(kq[i][:q_len] * decay[i]), 0.0) for i, (d, c) in enumerate(units)]
    p_w = [eye_w + n for n in n_w]
    m_w = [_dot(n, _block_diag(n, bd_mask)) for n in n_w]
    for _ in range(4):
        r = [_dot(jnp.concatenate([m, p], axis=0), _block_diag(m, bd_mask)) for m, p in zip(m_w, p_w)]
        m_w = [ri[:q_len] for ri in r]
        p_w = [p + ri[q_len:] for p, ri in zip(p_w, r)]
    t_w = [p + _dot(p, _block_diag(m, bd_mask)) for m, p in zip(m_w, p_w)]
    for i, (d, c) in enumerate(units):
        rs = rows_of(c)
        kc, bw = k_all[rs], beta_w[d][rs]
        exp_g = jnp.exp(g_cum[i])
        edge = 0 if d else q_len - 1
        g_edge = g_cum[i][edge:edge + 1, :]
        rhs = jnp.concatenate([_block_diag(v_all[rs] * bw, bd_mask), _block_diag(kc * bw * exp_g, bd_mask)], axis=1)
        uw = _dot(t_w[i], rhs)
        u_ref[d, rs, :] = uw[:, :GROUP_W]
        w_ref[d, rs, :] = uw[:, GROUP_W:].astype(BF16)
        qk_ref[d, rs, :] = (kq[i][q_len:] * decay[i]).astype(BF16)
        qd_ref[d, rs, :] = (q_all[rs] * exp_g).astype(BF16)
        kd_ref[d, rs, :] = (kc * jnp.exp(g_edge - g_cum[i])).astype(BF16)
        ge_ref[d, SUBLANES * c:SUBLANES * (c + 1), :] = jnp.broadcast_to(jnp.exp(g_edge), (SUBLANES, GROUP_W))


def _delta_local(p3, conv_w, na_slot, dtb_slot):
    bsz, seq, _ = p3.shape
    q_len = D_CHUNK
    rows = min(256, seq)
    cb = rows // q_len
    nblk = seq // rows
    hb = rows // SUBLANES
    nblk8 = seq // SUBLANES
    ones_bd = _np_bf16(np.kron(np.eye(HEADS), np.ones((HDIM, HDIM))))
    trif = _np_bf16(_tri_blocks(q_len, q_len, upper=False))
    trib = _np_bf16(_tri_blocks(q_len, q_len, upper=True))
    eg = _np_bf16(np.stack([_head_expand(SM_AB + 8 + HEADS * d, HDIM) for d in range(2)]))
    eb = _np_bf16(np.stack([_head_expand(SM_AB + HEADS * d, HDIM) for d in range(2)]))
    bd = jnp.asarray(np.kron(np.eye(HEADS), np.ones((q_len, HDIM))).astype(np.float32))
    x_spec = pl.BlockSpec((None, rows, D_QKV), lambda b, i: (b, i, OFF_D_QKV // D_QKV))
    xp_spec = pl.BlockSpec((None, SUBLANES, D_QKV), lambda b, i: (b, jnp.maximum(i * hb - 1, 0), OFF_D_QKV // D_QKV))
    xn_spec = pl.BlockSpec((None, SUBLANES, D_QKV), lambda b, i: (b, jnp.minimum((i + 1) * hb, nblk8 - 1), OFF_D_QKV // D_QKV))
    sm_spec = pl.BlockSpec((None, rows, LANES), lambda b, i: (b, i, OFF_SMALL // LANES))
    full = lambda a: pl.BlockSpec(a.shape, lambda b, i: (0,) * a.ndim)
    consts = [conv_w, na_slot, dtb_slot, ones_bd, trif, trib, eg, eb, bd]
    out_blk = pl.BlockSpec((None, 2, rows, GROUP_W), lambda b, i: (b, 0, i, 0))
    ge_blk = pl.BlockSpec((None, 2, SUBLANES * cb, GROUP_W), lambda b, i: (b, 0, i, 0))
    shp = lambda dt: jax.ShapeDtypeStruct((bsz, 2, seq, GROUP_W), dt)
    return pl.pallas_call(
        functools.partial(_delta_local_kernel, nblk=nblk, cb=cb),
        grid=(bsz, nblk),
        in_specs=[xp_spec, x_spec, xn_spec, sm_spec] + [full(a) for a in consts],
        out_specs=[out_blk] * 5 + [ge_blk],
        out_shape=[shp(F32), shp(BF16), shp(BF16), shp(BF16), shp(BF16),
                   jax.ShapeDtypeStruct((bsz, 2, seq // q_len * SUBLANES, GROUP_W), F32)],
        compiler_params=_cparams("parallel", "parallel"),
        name="delta_local",
    )(p3, p3, p3, p3, *consts)


def _delta_scan_kernel(*refs, bsz):
    ins, (bd_ref, of_ref, ob_ref, s_ref) = refs[:12], refs[12:]
    o_refs = (of_ref, ob_ref)

    @pl.when(pl.program_id(0) == 0)
    def _():
        s_ref[...] = jnp.zeros_like(s_ref)

    q_len = D_CHUNK
    bd_mask = bd_ref[...] > 0
    chains = [(d, b) for d in range(2) for b in range(bsz)]
    get = lambda k, d, b: ins[6 * d + k][b]
    s = [s_ref[bsz * d + b] for d, b in chains]
    r = [jnp.dot(jnp.concatenate([get(1, d, b), get(3, d, b)], axis=0), s[i].astype(BF16),
                 preferred_element_type=F32) for i, (d, b) in enumerate(chains)]
    v_new = [get(0, d, b) - r[i][:q_len] for i, (d, b) in enumerate(chains)]
    upd = [lax.dot_general(get(4, d, b), v_new[i].astype(BF16), (((0,), (0,)), ((), ())),
                           preferred_element_type=F32) for i, (d, b) in enumerate(chains)]
    for i, (d, b) in enumerate(chains):
        s_ref[bsz * d + b] = s[i] * get(5, d, b)[0:1, :] + jnp.where(bd_mask, upd[i], 0.0)
    for i, (d, b) in enumerate(chains):
        o_refs[d][b] = r[i][q_len:] + jnp.dot(get(2, d, b), _block_diag(v_new[i], bd_mask),
                                              preferred_element_type=F32)


def _delta_scan(u, w, qk, qd, kd, ge):
    bsz, _, seq, _ = u.shape
    q_len = D_CHUNK
    nc = seq // q_len
    bd = jnp.asarray(np.kron(np.eye(HEADS), np.ones((q_len, HDIM))).astype(np.float32))
    ch = (lambda c: c, lambda c: nc - 1 - c)
    blk = lambda d, rows: pl.BlockSpec((bsz, None, rows, GROUP_W), lambda c: (0, d, ch[d](c), 0))
    in_specs = [blk(d, rows) for d in range(2) for rows in (q_len,) * 5 + (SUBLANES,)]
    out_blk = lambda d: pl.BlockSpec((bsz, q_len, GROUP_W), lambda c: (0, ch[d](c), 0))
    out_shape = jax.ShapeDtypeStruct((bsz, seq, GROUP_W), F32)
    return pl.pallas_call(
        functools.partial(_delta_scan_kernel, bsz=bsz),
        grid=(nc,),
        in_specs=in_specs + [pl.BlockSpec(bd.shape, lambda c: (0, 0))],
        out_specs=[out_blk(0), out_blk(1)],
        out_shape=[out_shape, out_shape],
        scratch_shapes=[pltpu.VMEM((2 * bsz, GROUP_W, GROUP_W), F32)],
        compiler_params=_cparams("arbitrary"),
        name="delta_scan",
    )(*([u, w, qk, qd, kd, ge] * 2), bd)


def _out_proj_kernel(oa_ref, ob_ref, oc_ref, odf_ref, odb_ref, zd_ref, x_ref, wa_ref, wb_ref, wc_ref, wd_ref,
                     ga_ref, gb_ref, gd_ref, ones_ref, gp_ref, o_ref):
    oa = (_rms(oa_ref[...].astype(F32), GROUP_W) * ga_ref[...]).astype(BF16)
    ob = (_rms(ob_ref[...].astype(F32), GROUP_W) * gb_ref[...]).astype(BF16)
    od = odf_ref[...] + odb_ref[...]
    ms = _dot_sel_rhs(od * od, ones_ref[...]) * (1.0 / HDIM)
    od = od * lax.rsqrt(ms + EPS) * gd_ref[...] * _silu(zd_ref[...])
    acc = jnp.dot(oa, wa_ref[...], preferred_element_type=F32)
    acc += jnp.dot(ob, wb_ref[...], preferred_element_type=F32)
    acc += jnp.dot(oc_ref[...], wc_ref[...], preferred_element_type=F32)
    acc += jnp.dot(od.astype(BF16), wd_ref[...], preferred_element_type=F32)
    o_ref[...] = x_ref[...] + _rms(acc, D_MODEL) * gp_ref[...]


def _out_proj(oa, ob, oc, odf, odb, p3, x3, wa, wb, wc, wd, ga, gb, gd, gpost):
    bsz, seq, _ = x3.shape
    tm = min(512, seq)
    ones_bd = _np_bf16(np.kron(np.eye(HEADS), np.ones((HDIM, HDIM))))
    rows = lambda w: pl.BlockSpec((None, tm, w), lambda b, i: (b, i, 0))
    full = lambda a: pl.BlockSpec(a.shape, lambda b, i: (0,) * a.ndim)
    zd_spec = pl.BlockSpec((None, tm, GROUP_W), lambda b, i: (b, i, OFF_D_Z // GROUP_W))
    consts = [wa, wb, wc, wd, ga, gb, gd, ones_bd, gpost]
    return pl.pallas_call(
        _out_proj_kernel,
        grid=(bsz, seq // tm),
        in_specs=[rows(HEADS * LANES), rows(HEADS * LANES), rows(GROUP_W), rows(GROUP_W), rows(GROUP_W), zd_spec,
                  rows(D_MODEL)] + [full(a) for a in consts],
        out_specs=rows(D_MODEL),
        out_shape=jax.ShapeDtypeStruct(x3.shape, F32),
        compiler_params=_cparams("parallel", "parallel"),
        name="out_proj",
    )(oa, ob, oc, odf, odb, p3, x3, *consts)


def _ffn_kernel(xp_ref, x_ref, xn_ref, gpre_ref, win_ref, cw_ref, cb_ref, wout_ref, gpost_ref, o_ref,
                *, tiles_per_seq, nj):
    tm = x_ref.shape[0]
    halo = BF16_ROWS
    pos = lax.rem(pl.program_id(0), tiles_per_seq)
    norm = lambda t: _rms(t, D_MODEL) * gpre_ref[...]
    h = jnp.concatenate([(norm(xp_ref[...]) * jnp.where(pos == 0, 0.0, 1.0)).astype(BF16),
                         norm(x_ref[...]).astype(BF16),
                         (norm(xn_ref[...]) * jnp.where(pos == tiles_per_seq - 1, 0.0, 1.0)).astype(BF16)], axis=0)
    n = tm + 2 * halo
    body = slice(halo, halo + tm)
    acc = None
    for j in range(nj):
        cs = slice(2 * FF_TILE * j, 2 * FF_TILE * (j + 1))
        gu = jnp.dot(h, win_ref[:, cs], preferred_element_type=F32)
        conv = (pltpu.roll(gu, 1, axis=0)[body] * cw_ref[0:1, cs] + gu[body] * cw_ref[1:2, cs]
                + pltpu.roll(gu, n - 1, axis=0)[body] * cw_ref[2:3, cs] + cb_ref[:, cs])
        act = _silu(conv[:, :FF_TILE]) * conv[:, FF_TILE:]
        part = jnp.dot(act.astype(BF16), wout_ref[FF_TILE * j:FF_TILE * (j + 1), :], preferred_element_type=F32)
        acc = part if acc is None else acc + part
    o_ref[...] = x_ref[...] + _rms(acc, D_MODEL) * gpost_ref[...]


def _ffn(x2d, seq, gpre, w_in, conv_w, conv_b, w_out, gpost):
    m = x2d.shape[0]
    tm = min(512, seq)
    nj = D_FF // FF_TILE
    hb = tm // BF16_ROWS
    nblk = m // BF16_ROWS
    kern = functools.partial(_ffn_kernel, tiles_per_seq=seq // tm, nj=nj)
    const = lambda a: pl.BlockSpec(a.shape, lambda i: (0, 0), pipeline_mode=pl.Buffered(1))
    return pl.pallas_call(
        kern,
        grid=(m // tm,),
        in_specs=[pl.BlockSpec((BF16_ROWS, D_MODEL), lambda i: (jnp.maximum(i * hb - 1, 0), 0)),
                  pl.BlockSpec((tm, D_MODEL), lambda i: (i, 0)),
                  pl.BlockSpec((BF16_ROWS, D_MODEL), lambda i: (jnp.minimum((i + 1) * hb, nblk - 1), 0)),
                  const(gpre), const(w_in), const(conv_w), const(conv_b), const(w_out), const(gpost)],
        out_specs=pl.BlockSpec((tm, D_MODEL), lambda i: (i, 0)),
        out_shape=jax.ShapeDtypeStruct(x2d.shape, F32),
        compiler_params=_cparams("parallel"),
        name="conv_ffn",
    )(x2d, x2d, x2d, gpre, w_in, conv_w, conv_b, w_out, gpost)


def _pair_tiles(a):
    lead = a.shape[:-1]
    nj = D_FF // FF_TILE
    return a.reshape(lead + (2, nj, FF_TILE)).swapaxes(-3, -2).reshape(lead + (2 * D_FF,))


def _slot_vec(pairs):
    v = jnp.zeros((1, LANES), F32)
    for off, vals in pairs:
        v = v.at[0, off:off + vals.shape[0]].set(vals.astype(F32))
    return v


def _layer(x3, tabs_a, tabs_b, prm):
    bsz, seq, _ = x3.shape
    m = bsz * seq
    x2d = x3.reshape(m, D_MODEL)
    p3 = _in_proj(x2d, prm["pre_mix_norm"], _arrange_w_in(prm["w_in"])).reshape(bsz, seq, IN_COLS_PADDED)

    wuq = prm["a_w_uq"].reshape(A_Q_LORA, HEADS, A_NOPE + A_ROPE)
    rope_w = wuq[..., A_NOPE:]
    zpad = jnp.zeros((A_Q_LORA, HEADS, LANES - A_NOPE - A_ROPE), F32)
    w1 = jnp.concatenate([wuq[..., :A_NOPE], rope_w, zpad], axis=-1).reshape(A_Q_LORA, HEADS * LANES)
    w2 = jnp.concatenate([jnp.zeros_like(wuq[..., :A_NOPE]), _rot_last(rope_w), zpad], axis=-1).reshape(A_Q_LORA, HEADS * LANES)
    rpad = ((0, 256 - A_Q_LORA), (0, 0))
    w1, w2 = jnp.pad(w1, rpad).astype(BF16), jnp.pad(w2, rpad).astype(BF16)
    wukv = prm["a_w_ukv"].reshape(A_KV_LORA, HEADS, A_NOPE + HDIM)
    hpad = jnp.zeros((A_KV_LORA, HEADS, LANES - HDIM), F32)
    wk = jnp.concatenate([wukv[..., :A_NOPE], hpad], axis=-1).reshape(A_KV_LORA, HEADS * LANES).astype(BF16)
    wv = jnp.concatenate([wukv[..., A_NOPE:], hpad], axis=-1).reshape(A_KV_LORA, HEADS * LANES).astype(BF16)
    gq = jnp.pad(prm["a_q_norm"], (0, 256 - A_Q_LORA)).reshape(1, 256)
    gkv = prm["a_kv_norm"].reshape(1, A_KV_LORA)
    qa, ka, va = _mla_prep(p3, tabs_a, gq, gkv, w1, w2, wk, wv)
    oa = _attention(qa, ka, va)

    cos_b, sin_b = tabs_b
    c_b = (HDIM ** -0.5) * LOG2E
    pad64 = lambda g: jnp.pad(g, (0, LANES - HDIM)).reshape(1, LANES)
    qb, kb, vb = _gqa_prep(p3, cos_b, sin_b,
                           pad64(prm["b_q_norm"] * c_b), pad64(_swap_last(prm["b_q_norm"]) * c_b),
                           pad64(prm["b_k_norm"]), pad64(_swap_last(prm["b_k_norm"])))
    ob = _attention(qb, kb, vb)

    na_c = -jnp.exp(prm["c_a_log"].astype(F32)).reshape(-1)
    dtb_slot = _slot_vec([(SM_DT, prm["c_dt_bias"].reshape(-1))])
    na_slot = _slot_vec([(SM_DT, na_c)])
    conv_b = prm["c_conv_b"].reshape(1, -1)
    dsk_w = jnp.repeat(prm["c_d_skip"].astype(F32), HDIM).reshape(1, GROUP_W)
    yf = _ssd_pass(p3, None, None, prm["c_conv_w"], conv_b, dtb_slot, na_slot, reverse=False)
    oc = _ssd_pass(p3, yf, (dsk_w, prm["c_out_norm"].reshape(1, -1)), prm["c_conv_w"], conv_b, dtb_slot, na_slot,
                   reverse=True)

    na_d = -jnp.exp(prm["d_a_log"].astype(F32)).reshape(-1)
    d_na_slot = _slot_vec([(SM_AB + 8, na_d)])
    d_dtb_slot = _slot_vec([(SM_AB + 8, prm["d_dt_bias"].reshape(-1))])
    u, w, qk, qd, kd, ge = _delta_local(p3, prm["d_conv_w"], d_na_slot, d_dtb_slot)
    odf, odb = _delta_scan(u, w, qk, qd, kd, ge)

    wo = prm["w_out"]
    wa = _pad_heads(wo[0:256], 0, HEADS, HDIM, LANES).astype(BF16)
    wb = _pad_heads(wo[256:512], 0, HEADS, HDIM, LANES).astype(BF16)
    wc, wd = wo[512:768].astype(BF16), wo[768:1024].astype(BF16)
    ga = _pad_heads(prm["a_out_norm"], 0, HEADS, HDIM, LANES).reshape(1, -1)
    gb = _pad_heads(prm["b_out_norm"], 0, HEADS, HDIM, LANES).reshape(1, -1)
    gd = jnp.tile(prm["d_out_norm"], HEADS).reshape(1, -1)
    x3 = _out_proj(oa, ob, oc, odf, odb, p3, x3, wa, wb, wc, wd, ga, gb, gd, prm["post_mix_norm"].reshape(1, -1))

    x2d = _ffn(x3.reshape(m, D_MODEL), seq, prm["pre_ffn_norm"].reshape(1, -1),
               _pair_tiles(prm["f_w_in"]).astype(BF16), _pair_tiles(prm["f_conv_w"]),
               _pair_tiles(prm["f_conv_b"]).reshape(1, -1), prm["f_w_out"].astype(BF16),
               prm["post_ffn_norm"].reshape(1, -1))
    return x2d.reshape(bsz, seq, D_MODEL)


def _attn_tables(seq):
    cos_a, sin_a = _rope_tables(seq, A_ROPE)
    cos_b, sin_b = _rope_tables(seq, HDIM)
    c_a = ((A_NOPE + A_ROPE) ** -0.5) * LOG2E
    ones = jnp.ones((seq, A_NOPE), F32)
    z64 = jnp.zeros((seq, A_NOPE), F32)
    z32 = jnp.zeros((seq, LANES - A_NOPE - A_ROPE), F32)
    cq = c_a * jnp.concatenate([ones, cos_a, z32], axis=1)
    sq = c_a * jnp.concatenate([z64, sin_a, z32], axis=1)
    ck = jnp.concatenate([z64, cos_a, z32], axis=1)
    sk = jnp.concatenate([z64, sin_a, z32], axis=1)
    padb = lambda t: jnp.pad(t, ((0, 0), (0, LANES - HDIM)))
    return (cq, sq, ck, sk), (padb(cos_b), padb(sin_b))


def kernel(x, pre_mix_norm, w_in, a_q_norm, a_w_uq, a_kv_norm, a_w_ukv, a_out_norm, b_q_norm, b_k_norm, b_out_norm, c_conv_w, c_conv_b, c_a_log, c_dt_bias, c_d_skip, c_out_norm, d_conv_w, d_a_log, d_dt_bias, d_out_norm, w_out, post_mix_norm, pre_ffn_norm, f_w_in, f_conv_w, f_conv_b, f_w_out, post_ffn_norm):
    params = dict(pre_mix_norm=pre_mix_norm, w_in=w_in, a_q_norm=a_q_norm, a_w_uq=a_w_uq, a_kv_norm=a_kv_norm,
                  a_w_ukv=a_w_ukv, a_out_norm=a_out_norm, b_q_norm=b_q_norm, b_k_norm=b_k_norm,
                  b_out_norm=b_out_norm, c_conv_w=c_conv_w, c_conv_b=c_conv_b, c_a_log=c_a_log,
                  c_dt_bias=c_dt_bias, c_d_skip=c_d_skip, c_out_norm=c_out_norm, d_conv_w=d_conv_w,
                  d_a_log=d_a_log, d_dt_bias=d_dt_bias, d_out_norm=d_out_norm, w_out=w_out,
                  post_mix_norm=post_mix_norm, pre_ffn_norm=pre_ffn_norm, f_w_in=f_w_in, f_conv_w=f_conv_w,
                  f_conv_b=f_conv_b, f_w_out=f_w_out, post_ffn_norm=post_ffn_norm)
    tabs_a, tabs_b = _attn_tables(x.shape[1])
    for layer in range(w_in.shape[0]):
        x = _layer(x, tabs_a, tabs_b, {k: v[layer] for k, v in params.items()})
    return x
```

```python
import functools
import math

import numpy as np
import jax
import jax.numpy as jnp
from jax import lax
from jax.experimental import pallas as pl
from jax.experimental.pallas import tpu as pltpu

F32 = jnp.float32
BF16 = jnp.bfloat16

LANES = 128
SUBLANES = 8
BF16_ROWS = 16
VMEM_LIMIT = 56 * 1024 * 1024

EPS = 1e-6
ROPE_BASE = 10000.0
GRID_W = 64
D_MODEL = 1024
GROUP_W = 256
HEADS = 4
HDIM = 64

A_NOPE, A_ROPE, A_Q_LORA, A_KV_LORA = 64, 32, 192, 128
B_KV_HEADS = 2
C_STATE, C_CHUNK, C_XBC = 64, 128, 512
D_CHUNK, D_QKV = 64, 768
D_FF = 2816
FF_TILE = 256

OFF_C_XBC, OFF_A_CQ, OFF_D_QKV = 0, 512, 768
OFF_B_Q, OFF_B_QROT, OFF_C_Z, OFF_D_Z = 1536, 1792, 2048, 2304
OFF_A_CKV, OFF_SMALL, OFF_B_K, OFF_B_KROT, OFF_B_V = 2560, 2688, 2816, 2944, 3072
IN_COLS_PADDED = 3200
SM_DT, SM_AB, SM_KR, SM_KRROT = 0, 16, 64, 96

LOG2E = math.log2(math.e)


def _cparams(*sem):
    return pltpu.CompilerParams(dimension_semantics=sem, vmem_limit_bytes=VMEM_LIMIT)


def _dot(a, b):
    return jnp.dot(a.astype(BF16), b.astype(BF16), preferred_element_type=F32)


def _dot_nt(a, b):
    return lax.dot_general(a.astype(BF16), b.astype(BF16), (((1,), (1,)), ((), ())),
                           preferred_element_type=F32)


def _dot_tn(a, b):
    return lax.dot_general(a.astype(BF16), b.astype(BF16), (((0,), (0,)), ((), ())),
                           preferred_element_type=F32)


def _split3(x):
    hi = x.astype(BF16)
    r1 = x - hi.astype(F32)
    mid = r1.astype(BF16)
    lo = (r1 - mid.astype(F32)).astype(BF16)
    return hi, mid, lo


def _dot_sel_rhs(x, sel):
    hi, mid, lo = _split3(x)
    d = lambda p: jnp.dot(p, sel, preferred_element_type=F32)
    return d(hi) + d(mid) + d(lo)


def _dot_sel_lhs(sel, x):
    hi, mid, lo = _split3(x)
    d = lambda p: jnp.dot(sel, p, preferred_element_type=F32)
    return d(hi) + d(mid) + d(lo)


def _softplus(x):
    return jnp.maximum(x, 0.0) + jnp.log1p(jnp.exp(-jnp.abs(x)))


def _silu(x):
    return x * jax.nn.sigmoid(x)


def _rms(x, n):
    return x * lax.rsqrt(jnp.sum(x * x, axis=-1, keepdims=True) * (1.0 / n) + EPS)


def _conv3(x, prev_row, next_row, w_ref, bias):
    n = x.shape[0]
    row = lax.broadcasted_iota(jnp.int32, x.shape, 0)
    xm1 = jnp.where(row == 0, prev_row, pltpu.roll(x, 1, axis=0))
    xp1 = jnp.where(row == n - 1, next_row, pltpu.roll(x, n - 1, axis=0))
    y = xm1 * w_ref[0:1, :] + x * w_ref[1:2, :] + xp1 * w_ref[2:3, :]
    return y if bias is None else y + bias


def _block_diag(x, mask):
    return jnp.where(mask, jnp.concatenate([x] * HEADS, axis=0), 0.0).astype(BF16)


def _np_bf16(a):
    return jnp.asarray(np.asarray(a, np.float32), BF16)


def _head_expand(first_lane, width):
    e = np.zeros((LANES, HEADS * width), np.float32)
    for h in range(HEADS):
        e[first_lane + h, h * width:(h + 1) * width] = 1.0
    return e


def _tri_blocks(n, blk, upper):
    i = np.arange(n)
    same = (i[:, None] // blk) == (i[None, :] // blk)
    tri = (i[:, None] <= i[None, :]) if upper else (i[:, None] >= i[None, :])
    return (same & tri).astype(np.float32)


def _rot_last(w):
    r = w.shape[-1]
    xs = w.reshape(w.shape[:-1] + (2, 2, r // 4))
    return jnp.stack([-xs[..., 1, :], xs[..., 0, :]], axis=-2).reshape(w.shape)


def _swap_last(w):
    r = w.shape[-1]
    xs = w.reshape(w.shape[:-1] + (2, 2, r // 4))
    return jnp.stack([xs[..., 1, :], xs[..., 0, :]], axis=-2).reshape(w.shape)


def _rope_tables(seq_len, rot_dim):
    rows = seq_len // GRID_W
    row = jnp.repeat(jnp.arange(rows), GRID_W).astype(F32)
    col = jnp.tile(jnp.arange(GRID_W), rows).astype(F32)
    sec = rot_dim // 2
    inv_freq = ROPE_BASE ** (-jnp.arange(0, sec, 2, dtype=F32) / sec)
    ang_r = row[:, None] * inv_freq
    ang_c = col[:, None] * inv_freq
    ang = jnp.concatenate([ang_r, ang_r, ang_c, ang_c], axis=-1)
    return jnp.cos(ang), jnp.sin(ang)


def _pad_heads(a, axis, n_heads, real, padded):
    shp = list(a.shape)
    a = a.reshape(shp[:axis] + [n_heads, real] + shp[axis + 1:])
    pad = [(0, 0)] * a.ndim
    pad[axis + 1] = (0, padded - real)
    a = jnp.pad(a, pad)
    return a.reshape(shp[:axis] + [n_heads * padded] + shp[axis + 1:])


def _arrange_w_in(w):
    a0, b0, c0, d0 = 0, 352, 864, 1640
    zeros = lambda n: jnp.zeros((w.shape[0], n), w.dtype)
    cq, ckv, kr = w[:, a0:a0 + 192], w[:, a0 + 192:a0 + 320], w[:, a0 + 320:a0 + 352]
    bq, bk, bv = w[:, b0:b0 + 256], w[:, b0 + 256:b0 + 384], w[:, b0 + 384:b0 + 512]
    cz, cxbc, cdt = w[:, c0:c0 + 256], w[:, c0 + 256:c0 + 768], w[:, c0 + 768:c0 + 776]
    dqkv, dz, dab = w[:, d0:d0 + 768], w[:, d0 + 768:d0 + 1024], w[:, d0 + 1024:d0 + 1040]
    bq_rot = _rot_last(bq.reshape(-1, HEADS, HDIM)).reshape(-1, 256)
    bk_rot = _rot_last(bk.reshape(-1, B_KV_HEADS, HDIM)).reshape(-1, 128)
    small = jnp.concatenate([cdt, zeros(8), dab, zeros(32), kr, _rot_last(kr)], axis=1)
    cols = [cxbc, cq, zeros(64), dqkv, bq, bq_rot, cz, dz, ckv, small, bk, bk_rot, bv]
    out = jnp.concatenate(cols, axis=1)
    assert out.shape[1] == IN_COLS_PADDED
    return out.astype(BF16)


def _in_proj_kernel(x_ref, g_ref, w_ref, o_ref):
    h = (_rms(x_ref[...], D_MODEL) * g_ref[...]).astype(BF16)
    o_ref[...] = jnp.dot(h, w_ref[...], preferred_element_type=F32)


def _in_proj(x2d, gain, w):
    m = x2d.shape[0]
    tm = 512
    return pl.pallas_call(
        _in_proj_kernel,
        grid=(m // tm,),
        in_specs=[pl.BlockSpec((tm, D_MODEL), lambda i: (i, 0)),
                  pl.BlockSpec((1, D_MODEL), lambda i: (0, 0)),
                  pl.BlockSpec((D_MODEL, IN_COLS_PADDED), lambda i: (0, 0))],
        out_specs=pl.BlockSpec((tm, IN_COLS_PADDED), lambda i: (i, 0)),
        out_shape=jax.ShapeDtypeStruct((m, IN_COLS_PADDED), F32),
        compiler_params=_cparams("parallel"),
        name="in_proj",
    )(x2d, gain.reshape(1, -1), w)


def _mla_prep_kernel(cq_ref, ckv_ref, sm_ref, cqt_ref, sqt_ref, ckt_ref, skt_ref, gq_ref, gkv_ref,
                     w1_ref, w2_ref, wk_ref, wv_ref, q_ref, k_ref, v_ref):
    cqn = (_rms(cq_ref[...], A_Q_LORA) * gq_ref[...]).astype(BF16)
    q1 = jnp.dot(cqn, w1_ref[...], preferred_element_type=F32)
    q2 = jnp.dot(cqn, w2_ref[...], preferred_element_type=F32)
    kvn = (_rms(ckv_ref[...], A_KV_LORA) * gkv_ref[...]).astype(BF16)
    k1 = jnp.dot(kvn, wk_ref[...], preferred_element_type=F32)
    v1 = jnp.dot(kvn, wv_ref[...], preferred_element_type=F32)
    sm = sm_ref[...]
    k_rope = sm * ckt_ref[...] + pltpu.roll(sm, LANES - A_ROPE, axis=1) * skt_ref[...]
    cqt, sqt = cqt_ref[...], sqt_ref[...]
    ones_lane = lax.broadcasted_iota(jnp.int32, sm.shape, 1) == HDIM
    for h in range(HEADS):
        sl = slice(LANES * h, LANES * (h + 1))
        q_ref[h] = (q1[:, sl] * cqt + q2[:, sl] * sqt).astype(BF16)
        k_ref[h] = (k1[:, sl] + k_rope).astype(BF16)
        v_ref[h] = jnp.where(ones_lane, 1.0, v1[:, sl]).astype(BF16)


def _mla_prep(p3, tabs, gq, gkv, w1, w2, wk, wv):
    bsz, seq, _ = p3.shape
    tm = min(512, seq)
    col = lambda off, w: pl.BlockSpec((None, tm, w), lambda b, i, o=off // w: (b, i, o))
    tab = pl.BlockSpec((tm, LANES), lambda b, i: (i, 0))
    full = lambda a: pl.BlockSpec(a.shape, lambda b, i: (0,) * a.ndim)
    head_out = pl.BlockSpec((None, HEADS, tm, LANES), lambda b, i: (b, 0, i, 0))
    shp = jax.ShapeDtypeStruct((bsz, HEADS, seq, LANES), BF16)
    return pl.pallas_call(
        _mla_prep_kernel,
        grid=(bsz, seq // tm),
        in_specs=[col(OFF_A_CQ, 256), col(OFF_A_CKV, 128), col(OFF_SMALL, 128),
                  tab, tab, tab, tab, full(gq), full(gkv), full(w1), full(w2), full(wk), full(wv)],
        out_specs=[head_out, head_out, head_out],
        out_shape=[shp, shp, shp],
        compiler_params=_cparams("parallel", "parallel"),
        name="mla_prep",
    )(p3, p3, p3, *tabs, gq, gkv, w1, w2, wk, wv)


def _head_slot(x, h, lo):
    grp = x[:, LANES * (h // 2):LANES * (h // 2 + 1)]
    if h % 2:
        grp = pltpu.roll(grp, HDIM, axis=1)
    return jnp.where(lo, grp, 0.0)


def _gqa_prep_kernel(q_ref, qr_ref, k_ref, kr_ref, v_ref, cos_ref, sin_ref, gq_ref, gqs_ref, gk_ref, gks_ref,
                     qo_ref, ko_ref, vo_ref):
    cos, sin = cos_ref[...], sin_ref[...]
    lane = lax.broadcasted_iota(jnp.int32, cos.shape, 1)
    lo = lane < HDIM
    ones_lane = lane == HDIM

    def normed_rope(x, xr, h, g, gs):
        xh, xrh = _head_slot(x, h, lo), _head_slot(xr, h, lo)
        r = lax.rsqrt(jnp.sum(xh * xh, axis=-1, keepdims=True) * (1.0 / HDIM) + EPS)
        return (r * (xh * (cos * g) + xrh * (sin * gs))).astype(BF16)

    q, qr, k, kr, v = q_ref[...], qr_ref[...], k_ref[...], kr_ref[...], v_ref[...]
    for h in range(HEADS):
        qo_ref[h] = normed_rope(q, qr, h, gq_ref[...], gqs_ref[...])
    for h in range(B_KV_HEADS):
        ko_ref[h] = normed_rope(k, kr, h, gk_ref[...], gks_ref[...])
        vo_ref[h] = jnp.where(ones_lane, 1.0, _head_slot(v, h, lo)).astype(BF16)


def _gqa_prep(p3, cos, sin, gq, gqs, gk, gks):
    bsz, seq, _ = p3.shape
    tm = min(512, seq)
    col = lambda off, w: pl.BlockSpec((None, tm, w), lambda b, i, o=off // w: (b, i, o))
    tab = pl.BlockSpec((tm, LANES), lambda b, i: (i, 0))
    vec = pl.BlockSpec((1, LANES), lambda b, i: (0, 0))
    out = lambda n: pl.BlockSpec((None, n, tm, LANES), lambda b, i: (b, 0, i, 0))
    shp = lambda n: jax.ShapeDtypeStruct((bsz, n, seq, LANES), BF16)
    return pl.pallas_call(
        _gqa_prep_kernel,
        grid=(bsz, seq // tm),
        in_specs=[col(OFF_B_Q, 256), col(OFF_B_QROT, 256), col(OFF_B_K, 128), col(OFF_B_KROT, 128),
                  col(OFF_B_V, 128), tab, tab, vec, vec, vec, vec],
        out_specs=[out(HEADS), out(B_KV_HEADS), out(B_KV_HEADS)],
        out_shape=[shp(HEADS), shp(B_KV_HEADS), shp(B_KV_HEADS)],
        compiler_params=_cparams("parallel", "parallel"),
        name="gqa_prep",
    )(p3, p3, p3, p3, p3, cos, sin, gq, gqs, gk, gks)


def _attn_kernel(q_ref, k_ref, v_ref, o_ref, s0_ref, s1_ref, p0_ref, p1_ref, acc_ref, *, tk, nk):
    q = q_ref[...]
    tq = q.shape[0]

    s_refs, p_refs = (s0_ref, s1_ref), (p0_ref, p1_ref)

    def scores(c):
        s_refs[c % 2][...] = lax.dot_general(q, k_ref[tk * c:tk * (c + 1), :], (((1,), (1,)), ((), ())),
                                             preferred_element_type=F32)

    def probs(c, m):
        s = s_refs[c % 2][...]
        m_new = jnp.maximum(m, jnp.max(s, axis=-1, keepdims=True))
        p_refs[c % 2][...] = jnp.exp2(s - m_new).astype(BF16)
        return jnp.exp2(m - m_new), m_new

    def accumulate(c, alpha):
        pv = jnp.dot(p_refs[c % 2][...], v_ref[tk * c:tk * (c + 1), :], preferred_element_type=F32)
        acc_ref[...] = pv if c == 0 else alpha * acc_ref[...] + pv

    m = jnp.full((tq, 1), -1e30, F32)
    alphas = {}
    scores(0)
    for c in range(nk):
        if c + 1 < nk:
            scores(c + 1)
        alphas[c], m = probs(c, m)
        if c >= 1:
            accumulate(c - 1, alphas[c - 1])
    accumulate(nk - 1, alphas[nk - 1])
    acc = acc_ref[...]
    lane = lax.broadcasted_iota(jnp.int32, acc.shape, 1)
    o_ref[...] = jnp.where(lane < HDIM, acc / acc[:, HDIM:HDIM + 1], 0.0).astype(BF16)


def _attention(q, k, v):
    bsz, nh, seq, _ = q.shape
    rep = nh // k.shape[1]
    tq = min(1024, seq)
    tk = min(512, seq // 2)
    nk = seq // tk
    kern = functools.partial(_attn_kernel, tk=tk, nk=nk)
    kv_spec = pl.BlockSpec((None, None, seq, LANES), lambda b, h, i: (b, h // rep, 0, 0))
    return pl.pallas_call(
        kern,
        grid=(bsz, nh, seq // tq),
        in_specs=[pl.BlockSpec((None, None, tq, LANES), lambda b, h, i: (b, h, i, 0)), kv_spec, kv_spec],
        out_specs=pl.BlockSpec((None, tq, LANES), lambda b, h, i: (b, i, h)),
        out_shape=jax.ShapeDtypeStruct((bsz, seq, nh * LANES), BF16),
        scratch_shapes=[pltpu.VMEM((tq, tk), F32), pltpu.VMEM((tq, tk), F32),
                        pltpu.VMEM((tq, tk), BF16), pltpu.VMEM((tq, tk), BF16),
                        pltpu.VMEM((tq, LANES), F32)],
        compiler_params=_cparams("parallel", "parallel", "arbitrary"),
        name="attention",
    )(q, k, v)


def _ssd_kernel(*refs, bsz, nc):
    ins = refs[:8]
    (cw_ref, cb_ref, dtb_ref, na_ref, trif_ref, trib_ref, e2_ref, ew_ref, smask_ref, dsk_ref,
     yf_ref, yb_ref, s_ref) = refs[8:]
    step = pl.program_id(0)
    q_len = C_CHUNK

    @pl.when(step == 0)
    def _():
        s_ref[...] = jnp.zeros_like(s_ref)

    chains = [(d, b) for d in range(2) for b in range(bsz)]
    chunk = (step, nc - 1 - step)
    tri = (trif_ref[...], trib_ref[...])
    lane = lax.broadcasted_iota(jnp.int32, (q_len, LANES), 1)
    ri = lax.broadcasted_iota(jnp.int32, (q_len, q_len), 0)
    ci = lax.broadcasted_iota(jnp.int32, (q_len, q_len), 1)
    causal = (ri >= ci, ri <= ci)
    lane_w = lax.broadcasted_iota(jnp.int32, (q_len, GROUP_W), 1)
    smask = smask_ref[...] > 0

    xs, bm, cm, dt_slot, cum_slot = [], [], [], [], []
    for d, b in chains:
        xp_ref, x_ref, xn_ref, sm_ref = ins[4 * d:4 * d + 4]
        prev_row = xp_ref[b, SUBLANES - 1:SUBLANES, :] * jnp.where(chunk[d] > 0, 1.0, 0.0)
        next_row = xn_ref[b, 0:1, :] * jnp.where(chunk[d] < nc - 1, 1.0, 0.0)
        act = _silu(_conv3(x_ref[b], prev_row, next_row, cw_ref, cb_ref[...]))
        xs.append(act[:, :GROUP_W])
        bm.append(act[:, GROUP_W:GROUP_W + LANES])
        cm.append(act[:, GROUP_W + LANES:])
        dt = _softplus(sm_ref[b] + dtb_ref[...])
        dt_slot.append(dt)
        cum_slot.append(_dot_sel_lhs(tri[d], dt * na_ref[...]))
    cum_w = [_dot_sel_rhs(cum_slot[i], ew_ref[d]) for i, (d, b) in enumerate(chains)]
    cum_2 = [_dot_sel_rhs(cum_slot[i], e2_ref[d]) for i, (d, b) in enumerate(chains)]
    dt_w = [_dot_sel_rhs(dt_slot[i], ew_ref[d]) for i, (d, b) in enumerate(chains)]
    cb = [_dot_nt(jnp.concatenate([jnp.where(lane < C_STATE, c, 0.0), jnp.where(lane >= C_STATE, c, 0.0)], axis=0), bk)
          for c, bk in zip(cm, bm)]
    s_in = [s_ref[bsz * d + b] for d, b in chains]
    y_off = [_dot(cm[i], s_in[i]) * jnp.exp(cum_w[i]) for i in range(len(chains))]
    xdt = [xs[i] * dt_w[i] for i in range(len(chains))]
    for i, (d, b) in enumerate(chains):
        edge = 0 if d else q_len - 1
        cum_edge = cum_w[i][edge:edge + 1, :]
        s_new = jnp.where(smask, _dot_tn(bm[i], xdt[i] * jnp.exp(cum_edge - cum_w[i])), 0.0)
        s_ref[bsz * d + b] = s_in[i] * jnp.exp(cum_edge) + s_new
    y = list(y_off)
    for h in range(HEADS):
        g = h // 2
        head_lanes = (lane_w >= HDIM * h) & (lane_w < HDIM * (h + 1))
        for i, (d, b) in enumerate(chains):
            a_bc = cum_2[i][:, LANES * h:LANES * (h + 1)]
            decay = jnp.exp(jnp.where(causal[d], a_bc - a_bc.T, -1e30))
            scores = cb[i][q_len * g:q_len * (g + 1), :] * decay
            yh = jnp.dot(scores.astype(BF16), xdt[i].astype(BF16), preferred_element_type=F32)
            y[i] = y[i] + jnp.where(head_lanes, yh, 0.0)
    for i, (d, b) in enumerate(chains):
        if d == 0:
            yf_ref[b] = y[i] + xs[i] * dsk_ref[...]
        else:
            yb_ref[b] = y[i]


def _ssd_scan(p3, conv_w, conv_b, dtb_slot, na_slot, dsk_w):
    bsz, seq, _ = p3.shape
    q_len = C_CHUNK
    nc = seq // q_len
    hb = q_len // SUBLANES
    nblk8 = seq // SUBLANES
    ch = (lambda s: s, lambda s: nc - 1 - s)
    trif = _np_bf16(_tri_blocks(q_len, q_len, upper=False))
    trib = _np_bf16(_tri_blocks(q_len, q_len, upper=True))
    e2 = _np_bf16(np.stack([_head_expand(SM_DT + HEADS * d, LANES) for d in range(2)]))
    ew = _np_bf16(np.stack([_head_expand(SM_DT + HEADS * d, HDIM) for d in range(2)]))
    gi = np.arange(LANES)[:, None] // C_STATE
    hi = np.arange(GROUP_W)[None, :] // HDIM
    smask = jnp.asarray((gi == hi // 2).astype(np.float32))

    def dir_specs(d):
        return [pl.BlockSpec((bsz, SUBLANES, C_XBC), lambda s: (0, jnp.maximum(ch[d](s) * hb - 1, 0), 0)),
                pl.BlockSpec((bsz, q_len, C_XBC), lambda s: (0, ch[d](s), OFF_C_XBC // C_XBC)),
                pl.BlockSpec((bsz, SUBLANES, C_XBC), lambda s: (0, jnp.minimum((ch[d](s) + 1) * hb, nblk8 - 1), 0)),
                pl.BlockSpec((bsz, q_len, LANES), lambda s: (0, ch[d](s), OFF_SMALL // LANES))]

    full = lambda a: pl.BlockSpec(a.shape, lambda s: (0,) * a.ndim)
    consts = [conv_w, conv_b, dtb_slot, na_slot, trif, trib, e2, ew, smask, dsk_w]
    out_blk = lambda d: pl.BlockSpec((bsz, q_len, GROUP_W), lambda s: (0, ch[d](s), 0))
    out_shape = jax.ShapeDtypeStruct((bsz, seq, GROUP_W), F32)
    return pl.pallas_call(
        functools.partial(_ssd_kernel, bsz=bsz, nc=nc),
        grid=(nc,),
        in_specs=dir_specs(0) + dir_specs(1) + [full(a) for a in consts],
        out_specs=[out_blk(0), out_blk(1)],
        out_shape=[out_shape, out_shape],
        scratch_shapes=[pltpu.VMEM((2 * bsz, LANES, GROUP_W), F32)],
        compiler_params=_cparams("arbitrary"),
        name="ssd_scan",
    )(*([p3] * 8), *consts)


def _delta_local_kernel(xp_ref, x_ref, xn_ref, sm_ref, cw_ref, na_ref, dtb_ref, ones_ref, trif_ref, trib_ref,
                        eg_ref, eb_ref, bd_ref, u_ref, w_ref, qk_ref, qd_ref, kd_ref, ge_ref, *, nblk, cb):
    blk = pl.program_id(1)
    q_len = D_CHUNK
    x = x_ref[...]
    prev_row = xp_ref[SUBLANES - 1:SUBLANES, :] * jnp.where(blk > 0, 1.0, 0.0)
    next_row = xn_ref[0:1, :] * jnp.where(blk < nblk - 1, 1.0, 0.0)
    act = _silu(_conv3(x, prev_row, next_row, cw_ref, None))
    ones_bd = ones_ref[...]

    def l2n(t):
        ss = _dot_sel_rhs(t * t, ones_bd)
        return t * lax.rsqrt(ss + 1e-6)

    q_all = l2n(act[:, :GROUP_W]) * (HDIM ** -0.5)
    k_all = l2n(act[:, GROUP_W:2 * GROUP_W])
    v_all = act[:, 2 * GROUP_W:]
    sm = sm_ref[...]
    beta_slot = jax.nn.sigmoid(sm)
    g_slot = na_ref[...] * _softplus(sm + dtb_ref[...])
    bd_mask = bd_ref[...] > 0

    lane = lax.broadcasted_iota(jnp.int32, (q_len, GROUP_W), 1) & (q_len - 1)
    row = lax.broadcasted_iota(jnp.int32, (q_len, GROUP_W), 0)
    eye_w = jnp.where(row == lane, 1.0, 0.0)
    incl = (row >= lane, row <= lane)
    strict = (row > lane, row < lane)
    tri = (trif_ref[...], trib_ref[...])
    g_w = [_dot_sel_rhs(g_slot, eg_ref[d]) for d in range(2)]
    beta_w = [_dot_sel_rhs(beta_slot, eb_ref[d]) for d in range(2)]

    units = [(d, c) for d in range(2) for c in range(cb)]
    rows_of = lambda c: slice(q_len * c, q_len * (c + 1))
    seg, g_cum = [], []
    for d, c in units:
        gu = g_w[d][rows_of(c)]
        sg = _dot_sel_lhs(tri[d], jnp.concatenate([jnp.where(strict[d], gu, 0.0), gu], axis=1))
        seg.append(sg[:, :GROUP_W])
        g_cum.append(sg[:, GROUP_W:])
    kq = []
    for d, c in units:
        kc = k_all[rows_of(c)]
        lhs = jnp.concatenate([kc * beta_w[d][rows_of(c)], q_all[rows_of(c)]], axis=0)
        kq.append(_dot_nt(lhs, _block_diag(kc, bd_mask)))
    decay = [jnp.exp(jnp.where(incl[d], seg[i], -1e30)) for i, (d, c) in enumerate(units)]
    n_w = [jnp.where(strict[d], -(kq[i][:q_len] * decay[i]), 0.0) for i, (d, c) in enumerate(units)]
    p_w = [eye_w + n for n in n_w]
    m_w = [_dot(n, _block_diag(n, bd_mask)) for n in n_w]
    for _ in range(4):
        r = [_dot(jnp.concatenate([m, p], axis=0), _block_diag(m, bd_mask)) for m, p in zip(m_w, p_w)]
        m_w = [ri[:q_len] for ri in r]
        p_w = [p + ri[q_len:] for p, ri in zip(p_w, r)]
    t_w = [p + _dot(p, _block_diag(m, bd_mask)) for m, p in zip(m_w, p_w)]
    for i, (d, c) in enumerate(units):
        rs = rows_of(c)
        kc, bw = k_all[rs], beta_w[d][rs]
        exp_g = jnp.exp(g_cum[i])
        edge = 0 if d else q_len - 1
        g_edge = g_cum[i][edge:edge + 1, :]
        rhs = jnp.concatenate([_block_diag(v_all[rs] * bw, bd_mask), _block_diag(kc * bw * exp_g, bd_mask)], axis=1)
        uw = _dot(t_w[i], rhs)
        u_ref[d, rs, :] = uw[:, :GROUP_W]
        w_ref[d, rs, :] = uw[:, GROUP_W:].astype(BF16)
        qk_ref[d, rs, :] = (kq[i][q_len:] * decay[i]).astype(BF16)
        qd_ref[d, rs, :] = (q_all[rs] * exp_g).astype(BF16)
        kd_ref[d, rs, :] = (kc * jnp.exp(g_edge - g_cum[i])).astype(BF16)
        ge_ref[d, SUBLANES * c:SUBLANES * (c + 1), :] = jnp.broadcast_to(jnp.exp(g_edge), (SUBLANES, GROUP_W))


def _delta_local(p3, conv_w, na_slot, dtb_slot):
    bsz, seq, _ = p3.shape
    q_len = D_CHUNK
    rows = min(256, seq)
    cb = rows // q_len
    nblk = seq // rows
    hb = rows // SUBLANES
    nblk8 = seq // SUBLANES
    ones_bd = _np_bf16(np.kron(np.eye(HEADS), np.ones((HDIM, HDIM))))
    trif = _np_bf16(_tri_blocks(q_len, q_len, upper=False))
    trib = _np_bf16(_tri_blocks(q_len, q_len, upper=True))
    eg = _np_bf16(np.stack([_head_expand(SM_AB + 8 + HEADS * d, HDIM) for d in range(2)]))
    eb = _np_bf16(np.stack([_head_expand(SM_AB + HEADS * d, HDIM) for d in range(2)]))
    bd = jnp.asarray(np.kron(np.eye(HEADS), np.ones((q_len, HDIM))).astype(np.float32))
    x_spec = pl.BlockSpec((None, rows, D_QKV), lambda b, i: (b, i, OFF_D_QKV // D_QKV))
    xp_spec = pl.BlockSpec((None, SUBLANES, D_QKV), lambda b, i: (b, jnp.maximum(i * hb - 1, 0), OFF_D_QKV // D_QKV))
    xn_spec = pl.BlockSpec((None, SUBLANES, D_QKV), lambda b, i: (b, jnp.minimum((i + 1) * hb, nblk8 - 1), OFF_D_QKV // D_QKV))
    sm_spec = pl.BlockSpec((None, rows, LANES), lambda b, i: (b, i, OFF_SMALL // LANES))
    full = lambda a: pl.BlockSpec(a.shape, lambda b, i: (0,) * a.ndim)
    consts = [conv_w, na_slot, dtb_slot, ones_bd, trif, trib, eg, eb, bd]
    out_blk = pl.BlockSpec((None, 2, rows, GROUP_W), lambda b, i: (b, 0, i, 0))
    ge_blk = pl.BlockSpec((None, 2, SUBLANES * cb, GROUP_W), lambda b, i: (b, 0, i, 0))
    shp = lambda dt: jax.ShapeDtypeStruct((bsz, 2, seq, GROUP_W), dt)
    return pl.pallas_call(
        functools.partial(_delta_local_kernel, nblk=nblk, cb=cb),
        grid=(bsz, nblk),
        in_specs=[xp_spec, x_spec, xn_spec, sm_spec] + [full(a) for a in consts],
        out_specs=[out_blk] * 5 + [ge_blk],
        out_shape=[shp(F32), shp(BF16), shp(BF16), shp(BF16), shp(BF16),
                   jax.ShapeDtypeStruct((bsz, 2, seq // q_len * SUBLANES, GROUP_W), F32)],
        compiler_params=_cparams("parallel", "parallel"),
        name="delta_local",
    )(p3, p3, p3, p3, *consts)


def _delta_scan_kernel(*refs, bsz):
    ins, (bd_ref, of_ref, ob_ref, s_ref) = refs[:12], refs[12:]
    o_refs = (of_ref, ob_ref)

    @pl.when(pl.program_id(0) == 0)
    def _():
        s_ref[...] = jnp.zeros_like(s_ref)

    q_len = D_CHUNK
    bd_mask = bd_ref[...] > 0
    chains = [(d, b) for d in range(2) for b in range(bsz)]
    get = lambda k, d, b: ins[6 * d + k][b]
    s = [s_ref[bsz * d + b] for d, b in chains]
    r = [jnp.dot(jnp.concatenate([get(1, d, b), get(3, d, b)], axis=0), s[i].astype(BF16),
                 preferred_element_type=F32) for i, (d, b) in enumerate(chains)]
    v_new = [get(0, d, b) - r[i][:q_len] for i, (d, b) in enumerate(chains)]
    upd = [lax.dot_general(get(4, d, b), v_new[i].astype(BF16), (((0,), (0,)), ((), ())),
                           preferred_element_type=F32) for i, (d, b) in enumerate(chains)]
    for i, (d, b) in enumerate(chains):
        s_ref[bsz * d + b] = s[i] * get(5, d, b)[0:1, :] + jnp.where(bd_mask, upd[i], 0.0)
    for i, (d, b) in enumerate(chains):
        o_refs[d][b] = r[i][q_len:] + jnp.dot(get(2, d, b), _block_diag(v_new[i], bd_mask),
                                              preferred_element_type=F32)


def _delta_scan(u, w, qk, qd, kd, ge):
    bsz, _, seq, _ = u.shape
    q_len = D_CHUNK
    nc = seq // q_len
    bd = jnp.asarray(np.kron(np.eye(HEADS), np.ones((q_len, HDIM))).astype(np.float32))
    ch = (lambda c: c, lambda c: nc - 1 - c)
    blk = lambda d, rows: pl.BlockSpec((bsz, None, rows, GROUP_W), lambda c: (0, d, ch[d](c), 0))
    in_specs = [blk(d, rows) for d in range(2) for rows in (q_len,) * 5 + (SUBLANES,)]
    out_blk = lambda d: pl.BlockSpec((bsz, q_len, GROUP_W), lambda c: (0, ch[d](c), 0))
    out_shape = jax.ShapeDtypeStruct((bsz, seq, GROUP_W), F32)
    return pl.pallas_call(
        functools.partial(_delta_scan_kernel, bsz=bsz),
        grid=(nc,),
        in_specs=in_specs + [pl.BlockSpec(bd.shape, lambda c: (0, 0))],
        out_specs=[out_blk(0), out_blk(1)],
        out_shape=[out_shape, out_shape],
        scratch_shapes=[pltpu.VMEM((2 * bsz, GROUP_W, GROUP_W), F32)],
        compiler_params=_cparams("arbitrary"),
        name="delta_scan",
    )(*([u, w, qk, qd, kd, ge] * 2), bd)


def _out_proj_kernel(oa_ref, ob_ref, ycf_ref, ycb_ref, zc_ref, odf_ref, odb_ref, zd_ref, x_ref,
                     wa_ref, wb_ref, wc_ref, wd_ref, ga_ref, gb_ref, gc_ref, gd_ref, ones_ref, gp_ref, o_ref):
    oa = (_rms(oa_ref[...].astype(F32), GROUP_W) * ga_ref[...]).astype(BF16)
    ob = (_rms(ob_ref[...].astype(F32), GROUP_W) * gb_ref[...]).astype(BF16)
    oc = (_rms((ycf_ref[...] + ycb_ref[...]) * _silu(zc_ref[...]), GROUP_W) * gc_ref[...]).astype(BF16)
    od = odf_ref[...] + odb_ref[...]
    ms = _dot_sel_rhs(od * od, ones_ref[...]) * (1.0 / HDIM)
    od = od * lax.rsqrt(ms + EPS) * gd_ref[...] * _silu(zd_ref[...])
    acc = jnp.dot(oa, wa_ref[...], preferred_element_type=F32)
    acc += jnp.dot(ob, wb_ref[...], preferred_element_type=F32)
    acc += jnp.dot(oc, wc_ref[...], preferred_element_type=F32)
    acc += jnp.dot(od.astype(BF16), wd_ref[...], preferred_element_type=F32)
    o_ref[...] = x_ref[...] + _rms(acc, D_MODEL) * gp_ref[...]


def _out_proj(oa, ob, ycf, ycb, odf, odb, p3, x3, wa, wb, wc, wd, ga, gb, gc, gd, gpost):
    bsz, seq, _ = x3.shape
    tm = min(512, seq)
    ones_bd = _np_bf16(np.kron(np.eye(HEADS), np.ones((HDIM, HDIM))))
    rows = lambda w: pl.BlockSpec((None, tm, w), lambda b, i: (b, i, 0))
    full = lambda a: pl.BlockSpec(a.shape, lambda b, i: (0,) * a.ndim)
    z_spec = lambda off: pl.BlockSpec((None, tm, GROUP_W), lambda b, i: (b, i, off // GROUP_W))
    consts = [wa, wb, wc, wd, ga, gb, gc, gd, ones_bd, gpost]
    return pl.pallas_call(
        _out_proj_kernel,
        grid=(bsz, seq // tm),
        in_specs=[rows(HEADS * LANES), rows(HEADS * LANES), rows(GROUP_W), rows(GROUP_W), z_spec(OFF_C_Z),
                  rows(GROUP_W), rows(GROUP_W), z_spec(OFF_D_Z), rows(D_MODEL)] + [full(a) for a in consts],
        out_specs=rows(D_MODEL),
        out_shape=jax.ShapeDtypeStruct(x3.shape, F32),
        compiler_params=_cparams("parallel", "parallel"),
        name="out_proj",
    )(oa, ob, ycf, ycb, p3, odf, odb, p3, x3, *consts)


def _ffn_kernel(xp_ref, x_ref, xn_ref, gpre_ref, win_ref, cw_ref, cb_ref, wout_ref, gpost_ref, o_ref,
                *, tiles_per_seq, nj):
    tm = x_ref.shape[0]
    halo = BF16_ROWS
    pos = lax.rem(pl.program_id(0), tiles_per_seq)
    norm = lambda t: _rms(t, D_MODEL) * gpre_ref[...]
    h = jnp.concatenate([(norm(xp_ref[...]) * jnp.where(pos == 0, 0.0, 1.0)).astype(BF16),
                         norm(x_ref[...]).astype(BF16),
                         (norm(xn_ref[...]) * jnp.where(pos == tiles_per_seq - 1, 0.0, 1.0)).astype(BF16)], axis=0)
    n = tm + 2 * halo
    body = slice(halo, halo + tm)

    def project(j):
        return [jnp.dot(h, win_ref[:, off + FF_TILE * j:off + FF_TILE * (j + 1)], preferred_element_type=F32)
                for off in (0, D_FF)]

    def conv(t, off, j):
        cs = slice(off + FF_TILE * j, off + FF_TILE * (j + 1))
        return (pltpu.roll(t, 1, axis=0)[body] * cw_ref[0:1, cs] + t[body] * cw_ref[1:2, cs]
                + pltpu.roll(t, n - 1, axis=0)[body] * cw_ref[2:3, cs] + cb_ref[:, cs])

    def down(act, j):
        return jnp.dot(act, wout_ref[FF_TILE * j:FF_TILE * (j + 1), :], preferred_element_type=F32)

    acc = None
    act_prev = None
    nxt = project(0)
    for j in range(nj):
        gate, up = nxt
        if j + 1 < nj:
            nxt = project(j + 1)
        if j >= 1:
            part = down(act_prev, j - 1)
            acc = part if acc is None else acc + part
        act_prev = (_silu(conv(gate, 0, j)) * conv(up, D_FF, j)).astype(BF16)
    acc = acc + down(act_prev, nj - 1)
    o_ref[...] = x_ref[...] + _rms(acc, D_MODEL) * gpost_ref[...]


def _ffn(x2d, seq, layer, gpre, w_in, conv_w, conv_b, w_out, gpost):
    m = x2d.shape[0]
    tm = min(512, seq)
    nj = D_FF // FF_TILE
    hb = tm // BF16_ROWS
    nblk = m // BF16_ROWS
    kern = functools.partial(_ffn_kernel, tiles_per_seq=seq // tm, nj=nj)
    const = lambda a: pl.BlockSpec((None,) + a.shape[1:], lambda i: (layer, 0, 0), pipeline_mode=pl.Buffered(1))
    return pl.pallas_call(
        kern,
        grid=(m // tm,),
        in_specs=[pl.BlockSpec((BF16_ROWS, D_MODEL), lambda i: (jnp.maximum(i * hb - 1, 0), 0)),
                  pl.BlockSpec((tm, D_MODEL), lambda i: (i, 0)),
                  pl.BlockSpec((BF16_ROWS, D_MODEL), lambda i: (jnp.minimum((i + 1) * hb, nblk - 1), 0)),
                  const(gpre), const(w_in), const(conv_w), const(conv_b), const(w_out), const(gpost)],
        out_specs=pl.BlockSpec((tm, D_MODEL), lambda i: (i, 0)),
        out_shape=jax.ShapeDtypeStruct(x2d.shape, F32),
        compiler_params=_cparams("parallel"),
        name="conv_ffn",
    )(x2d, x2d, x2d, gpre, w_in, conv_w, conv_b, w_out, gpost)


def _slot_vec(pairs):
    v = jnp.zeros((1, LANES), F32)
    for off, vals in pairs:
        v = v.at[0, off:off + vals.shape[0]].set(vals.astype(F32))
    return v


def _layer(x3, tabs_a, tabs_b, prm, layer, ffn_prm):
    bsz, seq, _ = x3.shape
    m = bsz * seq
    x2d = x3.reshape(m, D_MODEL)
    p3 = _in_proj(x2d, prm["pre_mix_norm"], _arrange_w_in(prm["w_in"])).reshape(bsz, seq, IN_COLS_PADDED)

    wuq = prm["a_w_uq"].reshape(A_Q_LORA, HEADS, A_NOPE + A_ROPE)
    rope_w = wuq[..., A_NOPE:]
    zpad = jnp.zeros((A_Q_LORA, HEADS, LANES - A_NOPE - A_ROPE), F32)
    w1 = jnp.concatenate([wuq[..., :A_NOPE], rope_w, zpad], axis=-1).reshape(A_Q_LORA, HEADS * LANES)
    w2 = jnp.concatenate([jnp.zeros_like(wuq[..., :A_NOPE]), _rot_last(rope_w), zpad], axis=-1).reshape(A_Q_LORA, HEADS * LANES)
    rpad = ((0, 256 - A_Q_LORA), (0, 0))
    w1, w2 = jnp.pad(w1, rpad).astype(BF16), jnp.pad(w2, rpad).astype(BF16)
    wukv = prm["a_w_ukv"].reshape(A_KV_LORA, HEADS, A_NOPE + HDIM)
    hpad = jnp.zeros((A_KV_LORA, HEADS, LANES - HDIM), F32)
    wk = jnp.concatenate([wukv[..., :A_NOPE], hpad], axis=-1).reshape(A_KV_LORA, HEADS * LANES).astype(BF16)
    wv = jnp.concatenate([wukv[..., A_NOPE:], hpad], axis=-1).reshape(A_KV_LORA, HEADS * LANES).astype(BF16)
    gq = jnp.pad(prm["a_q_norm"], (0, 256 - A_Q_LORA)).reshape(1, 256)
    gkv = prm["a_kv_norm"].reshape(1, A_KV_LORA)
    qa, ka, va = _mla_prep(p3, tabs_a, gq, gkv, w1, w2, wk, wv)
    oa = _attention(qa, ka, va)

    cos_b, sin_b = tabs_b
    c_b = (HDIM ** -0.5) * LOG2E
    pad64 = lambda g: jnp.pad(g, (0, LANES - HDIM)).reshape(1, LANES)
    qb, kb, vb = _gqa_prep(p3, cos_b, sin_b,
                           pad64(prm["b_q_norm"] * c_b), pad64(_swap_last(prm["b_q_norm"]) * c_b),
                           pad64(prm["b_k_norm"]), pad64(_swap_last(prm["b_k_norm"])))
    ob = _attention(qb, kb, vb)

    na_c = -jnp.exp(prm["c_a_log"].astype(F32)).reshape(-1)
    dtb_slot = _slot_vec([(SM_DT, prm["c_dt_bias"].reshape(-1))])
    na_slot = _slot_vec([(SM_DT, na_c)])
    conv_b = prm["c_conv_b"].reshape(1, -1)
    dsk_w = jnp.repeat(prm["c_d_skip"].astype(F32), HDIM).reshape(1, GROUP_W)
    ycf, ycb = _ssd_scan(p3, prm["c_conv_w"], conv_b, dtb_slot, na_slot, dsk_w)

    na_d = -jnp.exp(prm["d_a_log"].astype(F32)).reshape(-1)
    d_na_slot = _slot_vec([(SM_AB + 8, na_d)])
    d_dtb_slot = _slot_vec([(SM_AB + 8, prm["d_dt_bias"].reshape(-1))])
    u, w, qk, qd, kd, ge = _delta_local(p3, prm["d_conv_w"], d_na_slot, d_dtb_slot)
    odf, odb = _delta_scan(u, w, qk, qd, kd, ge)

    wo = prm["w_out"]
    wa = _pad_heads(wo[0:256], 0, HEADS, HDIM, LANES).astype(BF16)
    wb = _pad_heads(wo[256:512], 0, HEADS, HDIM, LANES).astype(BF16)
    wc, wd = wo[512:768].astype(BF16), wo[768:1024].astype(BF16)
    ga = _pad_heads(prm["a_out_norm"], 0, HEADS, HDIM, LANES).reshape(1, -1)
    gb = _pad_heads(prm["b_out_norm"], 0, HEADS, HDIM, LANES).reshape(1, -1)
    gd = jnp.tile(prm["d_out_norm"], HEADS).reshape(1, -1)
    gc = prm["c_out_norm"].reshape(1, -1)
    x3 = _out_proj(oa, ob, ycf, ycb, odf, odb, p3, x3, wa, wb, wc, wd, ga, gb, gc, gd,
                   prm["post_mix_norm"].reshape(1, -1))

    x2d = _ffn(x3.reshape(m, D_MODEL), seq, layer, *ffn_prm)
    return x2d.reshape(bsz, seq, D_MODEL)


def _attn_tables(seq):
    cos_a, sin_a = _rope_tables(seq, A_ROPE)
    cos_b, sin_b = _rope_tables(seq, HDIM)
    c_a = ((A_NOPE + A_ROPE) ** -0.5) * LOG2E
    ones = jnp.ones((seq, A_NOPE), F32)
    z64 = jnp.zeros((seq, A_NOPE), F32)
    z32 = jnp.zeros((seq, LANES - A_NOPE - A_ROPE), F32)
    cq = c_a * jnp.concatenate([ones, cos_a, z32], axis=1)
    sq = c_a * jnp.concatenate([z64, sin_a, z32], axis=1)
    ck = jnp.concatenate([z64, cos_a, z32], axis=1)
    sk = jnp.concatenate([z64, sin_a, z32], axis=1)
    padb = lambda t: jnp.pad(t, ((0, 0), (0, LANES - HDIM)))
    return (cq, sq, ck, sk), (padb(cos_b), padb(sin_b))


def kernel(x, pre_mix_norm, w_in, a_q_norm, a_w_uq, a_kv_norm, a_w_ukv, a_out_norm, b_q_norm, b_k_norm, b_out_norm, c_conv_w, c_conv_b, c_a_log, c_dt_bias, c_d_skip, c_out_norm, d_conv_w, d_a_log, d_dt_bias, d_out_norm, w_out, post_mix_norm, pre_ffn_norm, f_w_in, f_conv_w, f_conv_b, f_w_out, post_ffn_norm):
    params = dict(pre_mix_norm=pre_mix_norm, w_in=w_in, a_q_norm=a_q_norm, a_w_uq=a_w_uq, a_kv_norm=a_kv_norm,
                  a_w_ukv=a_w_ukv, a_out_norm=a_out_norm, b_q_norm=b_q_norm, b_k_norm=b_k_norm,
                  b_out_norm=b_out_norm, c_conv_w=c_conv_w, c_conv_b=c_conv_b, c_a_log=c_a_log,
                  c_dt_bias=c_dt_bias, c_d_skip=c_d_skip, c_out_norm=c_out_norm, d_conv_w=d_conv_w,
                  d_a_log=d_a_log, d_dt_bias=d_dt_bias, d_out_norm=d_out_norm, w_out=w_out,
                  post_mix_norm=post_mix_norm)
    ffn_prm = (pre_ffn_norm[:, None, :], f_w_in.astype(BF16), f_conv_w, f_conv_b[:, None, :],
               f_w_out.astype(BF16), post_ffn_norm[:, None, :])
    tabs_a, tabs_b = _attn_tables(x.shape[1])
    for layer in range(w_in.shape[0]):
        x = _layer(x, tabs_a, tabs_b, {k: v[layer] for k, v in params.items()}, layer, ffn_prm)
    return x
```

```python
import functools
import math

import numpy as np
import jax
import jax.numpy as jnp
from jax import lax
from jax.experimental import pallas as pl
from jax.experimental.pallas import tpu as pltpu

F32 = jnp.float32
BF16 = jnp.bfloat16

LANES = 128
SUBLANES = 8
BF16_ROWS = 16
VMEM_LIMIT = 56 * 1024 * 1024

EPS = 1e-6
ROPE_BASE = 10000.0
GRID_W = 64
D_MODEL = 1024
GROUP_W = 256
HEADS = 4
HDIM = 64

A_NOPE, A_ROPE, A_Q_LORA, A_KV_LORA = 64, 32, 192, 128
B_KV_HEADS = 2
C_STATE, C_CHUNK, C_XBC = 64, 128, 512
D_CHUNK, D_QKV = 64, 768
D_FF = 2816
FF_TILE = 256

OFF_C_XBC, OFF_A_CQ, OFF_D_QKV = 0, 512, 768
OFF_B_Q, OFF_B_QROT, OFF_C_Z, OFF_D_Z = 1536, 1792, 2048, 2304
OFF_A_CKV, OFF_SMALL, OFF_B_K, OFF_B_KROT, OFF_B_V = 2560, 2688, 2816, 2944, 3072
IN_COLS_PADDED = 3200
SM_DT, SM_AB, SM_KR, SM_KRROT = 0, 16, 64, 96

LOG2E = math.log2(math.e)


def _cparams(*sem):
    return pltpu.CompilerParams(dimension_semantics=sem, vmem_limit_bytes=VMEM_LIMIT)


def _dot(a, b):
    return jnp.dot(a.astype(BF16), b.astype(BF16), preferred_element_type=F32)


def _dot_nt(a, b):
    return lax.dot_general(a.astype(BF16), b.astype(BF16), (((1,), (1,)), ((), ())),
                           preferred_element_type=F32)


def _dot_tn(a, b):
    return lax.dot_general(a.astype(BF16), b.astype(BF16), (((0,), (0,)), ((), ())),
                           preferred_element_type=F32)


def _split3(x):
    hi = x.astype(BF16)
    r1 = x - hi.astype(F32)
    mid = r1.astype(BF16)
    lo = (r1 - mid.astype(F32)).astype(BF16)
    return hi, mid, lo


def _dot_sel_rhs(x, sel):
    hi, mid, lo = _split3(x)
    d = lambda p: jnp.dot(p, sel, preferred_element_type=F32)
    return d(hi) + d(mid) + d(lo)


def _dot_sel_lhs(sel, x):
    hi, mid, lo = _split3(x)
    d = lambda p: jnp.dot(sel, p, preferred_element_type=F32)
    return d(hi) + d(mid) + d(lo)


def _softplus(x):
    return jnp.maximum(x, 0.0) + jnp.log1p(jnp.exp(-jnp.abs(x)))


def _silu(x):
    return x * jax.nn.sigmoid(x)


def _rms(x, n):
    return x * lax.rsqrt(jnp.sum(x * x, axis=-1, keepdims=True) * (1.0 / n) + EPS)


def _conv3(x, prev_row, next_row, w_ref, bias):
    n = x.shape[0]
    row = lax.broadcasted_iota(jnp.int32, x.shape, 0)
    xm1 = jnp.where(row == 0, prev_row, pltpu.roll(x, 1, axis=0))
    xp1 = jnp.where(row == n - 1, next_row, pltpu.roll(x, n - 1, axis=0))
    y = xm1 * w_ref[0:1, :] + x * w_ref[1:2, :] + xp1 * w_ref[2:3, :]
    return y if bias is None else y + bias


def _block_diag(x, mask):
    return jnp.where(mask, jnp.concatenate([x] * HEADS, axis=0), 0.0).astype(BF16)


def _np_bf16(a):
    return jnp.asarray(np.asarray(a, np.float32), BF16)


def _head_expand(first_lane, width):
    e = np.zeros((LANES, HEADS * width), np.float32)
    for h in range(HEADS):
        e[first_lane + h, h * width:(h + 1) * width] = 1.0
    return e


def _tri_blocks(n, blk, upper):
    i = np.arange(n)
    same = (i[:, None] // blk) == (i[None, :] // blk)
    tri = (i[:, None] <= i[None, :]) if upper else (i[:, None] >= i[None, :])
    return (same & tri).astype(np.float32)


def _rot_last(w):
    r = w.shape[-1]
    xs = w.reshape(w.shape[:-1] + (2, 2, r // 4))
    return jnp.stack([-xs[..., 1, :], xs[..., 0, :]], axis=-2).reshape(w.shape)


def _swap_last(w):
    r = w.shape[-1]
    xs = w.reshape(w.shape[:-1] + (2, 2, r // 4))
    return jnp.stack([xs[..., 1, :], xs[..., 0, :]], axis=-2).reshape(w.shape)


def _rope_tables(seq_len, rot_dim):
    rows = seq_len // GRID_W
    sec = rot_dim // 2
    inv_freq = ROPE_BASE ** (-jnp.arange(0, sec, 2, dtype=F32) / sec)
    ang_r = jnp.arange(rows).astype(F32)[:, None] * inv_freq
    ang_c = jnp.arange(GRID_W).astype(F32)[:, None] * inv_freq

    def table(fn):
        t_r = jnp.repeat(fn(ang_r), GRID_W, axis=0)
        t_c = jnp.tile(fn(ang_c), (rows, 1))
        return jnp.concatenate([t_r, t_r, t_c, t_c], axis=-1)

    return table(jnp.cos), table(jnp.sin)


def _pad_heads(a, axis, n_heads, real, padded):
    shp = list(a.shape)
    a = a.reshape(shp[:axis] + [n_heads, real] + shp[axis + 1:])
    pad = [(0, 0)] * a.ndim
    pad[axis + 1] = (0, padded - real)
    a = jnp.pad(a, pad)
    return a.reshape(shp[:axis] + [n_heads * padded] + shp[axis + 1:])


def _arrange_w_in(w):
    a0, b0, c0, d0 = 0, 352, 864, 1640
    zeros = lambda n: jnp.zeros((w.shape[0], n), w.dtype)
    cq, ckv, kr = w[:, a0:a0 + 192], w[:, a0 + 192:a0 + 320], w[:, a0 + 320:a0 + 352]
    bq, bk, bv = w[:, b0:b0 + 256], w[:, b0 + 256:b0 + 384], w[:, b0 + 384:b0 + 512]
    cz, cxbc, cdt = w[:, c0:c0 + 256], w[:, c0 + 256:c0 + 768], w[:, c0 + 768:c0 + 776]
    dqkv, dz, dab = w[:, d0:d0 + 768], w[:, d0 + 768:d0 + 1024], w[:, d0 + 1024:d0 + 1040]
    bq_rot = _rot_last(bq.reshape(-1, HEADS, HDIM)).reshape(-1, 256)
    bk_rot = _rot_last(bk.reshape(-1, B_KV_HEADS, HDIM)).reshape(-1, 128)
    small = jnp.concatenate([cdt, zeros(8), dab, zeros(32), kr, _rot_last(kr)], axis=1)
    cols = [cxbc, cq, zeros(64), dqkv, bq, bq_rot, cz, dz, ckv, small, bk, bk_rot, bv]
    out = jnp.concatenate(cols, axis=1)
    assert out.shape[1] == IN_COLS_PADDED
    return out.astype(BF16)


def _in_proj_kernel(x_ref, g_ref, w_ref, o_ref):
    h = (_rms(x_ref[...], D_MODEL) * g_ref[...]).astype(BF16)
    o_ref[...] = jnp.dot(h, w_ref[...], preferred_element_type=F32)


def _in_proj(x2d, gain, w):
    m = x2d.shape[0]
    tm = 512
    return pl.pallas_call(
        _in_proj_kernel,
        grid=(m // tm,),
        in_specs=[pl.BlockSpec((tm, D_MODEL), lambda i: (i, 0)),
                  pl.BlockSpec((1, D_MODEL), lambda i: (0, 0)),
                  pl.BlockSpec((D_MODEL, IN_COLS_PADDED), lambda i: (0, 0))],
        out_specs=pl.BlockSpec((tm, IN_COLS_PADDED), lambda i: (i, 0)),
        out_shape=jax.ShapeDtypeStruct((m, IN_COLS_PADDED), F32),
        compiler_params=_cparams("parallel"),
        name="in_proj",
    )(x2d, gain.reshape(1, -1), w)


def _mla_prep_kernel(cq_ref, ckv_ref, sm_ref, cqt_ref, sqt_ref, ckt_ref, skt_ref, gq_ref, gkv_ref,
                     w1_ref, w2_ref, wk_ref, wv_ref, q_ref, k_ref, v_ref):
    cqn = (_rms(cq_ref[...], A_Q_LORA) * gq_ref[...]).astype(BF16)
    q1 = jnp.dot(cqn, w1_ref[...], preferred_element_type=F32)
    q2 = jnp.dot(cqn, w2_ref[...], preferred_element_type=F32)
    kvn = (_rms(ckv_ref[...], A_KV_LORA) * gkv_ref[...]).astype(BF16)
    k1 = jnp.dot(kvn, wk_ref[...], preferred_element_type=F32)
    v1 = jnp.dot(kvn, wv_ref[...], preferred_element_type=F32)
    sm = sm_ref[...]
    k_rope = sm * ckt_ref[...] + pltpu.roll(sm, LANES - A_ROPE, axis=1) * skt_ref[...]
    cqt, sqt = cqt_ref[...], sqt_ref[...]
    ones_lane = lax.broadcasted_iota(jnp.int32, sm.shape, 1) == HDIM
    for h in range(HEADS):
        sl = slice(LANES * h, LANES * (h + 1))
        q_ref[h] = (q1[:, sl] * cqt + q2[:, sl] * sqt).astype(BF16)
        k_ref[h] = (k1[:, sl] + k_rope).astype(BF16)
        v_ref[h] = jnp.where(ones_lane, 1.0, v1[:, sl]).astype(BF16)


def _mla_prep(p3, tabs, gq, gkv, w1, w2, wk, wv):
    bsz, seq, _ = p3.shape
    tm = min(512, seq)
    col = lambda off, w: pl.BlockSpec((None, tm, w), lambda b, i, o=off // w: (b, i, o))
    tab = pl.BlockSpec((tm, LANES), lambda b, i: (i, 0))
    full = lambda a: pl.BlockSpec(a.shape, lambda b, i: (0,) * a.ndim)
    head_out = pl.BlockSpec((None, HEADS, tm, LANES), lambda b, i: (b, 0, i, 0))
    shp = jax.ShapeDtypeStruct((bsz, HEADS, seq, LANES), BF16)
    return pl.pallas_call(
        _mla_prep_kernel,
        grid=(bsz, seq // tm),
        in_specs=[col(OFF_A_CQ, 256), col(OFF_A_CKV, 128), col(OFF_SMALL, 128),
                  tab, tab, tab, tab, full(gq), full(gkv), full(w1), full(w2), full(wk), full(wv)],
        out_specs=[head_out, head_out, head_out],
        out_shape=[shp, shp, shp],
        compiler_params=_cparams("parallel", "parallel"),
        name="mla_prep",
    )(p3, p3, p3, *tabs, gq, gkv, w1, w2, wk, wv)


def _head_slot(x, h, lo):
    grp = x[:, LANES * (h // 2):LANES * (h // 2 + 1)]
    if h % 2:
        grp = pltpu.roll(grp, HDIM, axis=1)
    return jnp.where(lo, grp, 0.0)


def _gqa_prep_kernel(q_ref, qr_ref, k_ref, kr_ref, v_ref, cos_ref, sin_ref, gq_ref, gqs_ref, gk_ref, gks_ref,
                     qo_ref, ko_ref, vo_ref):
    cos, sin = cos_ref[...], sin_ref[...]
    lane = lax.broadcasted_iota(jnp.int32, cos.shape, 1)
    lo = lane < HDIM
    ones_lane = lane == HDIM

    def normed_rope(x, xr, h, g, gs):
        xh, xrh = _head_slot(x, h, lo), _head_slot(xr, h, lo)
        r = lax.rsqrt(jnp.sum(xh * xh, axis=-1, keepdims=True) * (1.0 / HDIM) + EPS)
        return (r * (xh * (cos * g) + xrh * (sin * gs))).astype(BF16)

    q, qr, k, kr, v = q_ref[...], qr_ref[...], k_ref[...], kr_ref[...], v_ref[...]
    for h in range(HEADS):
        qo_ref[h] = normed_rope(q, qr, h, gq_ref[...], gqs_ref[...])
    for h in range(B_KV_HEADS):
        ko_ref[h] = normed_rope(k, kr, h, gk_ref[...], gks_ref[...])
        vo_ref[h] = jnp.where(ones_lane, 1.0, _head_slot(v, h, lo)).astype(BF16)


def _gqa_prep(p3, cos, sin, gq, gqs, gk, gks):
    bsz, seq, _ = p3.shape
    tm = min(512, seq)
    col = lambda off, w: pl.BlockSpec((None, tm, w), lambda b, i, o=off // w: (b, i, o))
    tab = pl.BlockSpec((tm, LANES), lambda b, i: (i, 0))
    vec = pl.BlockSpec((1, LANES), lambda b, i: (0, 0))
    out = lambda n: pl.BlockSpec((None, n, tm, LANES), lambda b, i: (b, 0, i, 0))
    shp = lambda n: jax.ShapeDtypeStruct((bsz, n, seq, LANES), BF16)
    return pl.pallas_call(
        _gqa_prep_kernel,
        grid=(bsz, seq // tm),
        in_specs=[col(OFF_B_Q, 256), col(OFF_B_QROT, 256), col(OFF_B_K, 128), col(OFF_B_KROT, 128),
                  col(OFF_B_V, 128), tab, tab, vec, vec, vec, vec],
        out_specs=[out(HEADS), out(B_KV_HEADS), out(B_KV_HEADS)],
        out_shape=[shp(HEADS), shp(B_KV_HEADS), shp(B_KV_HEADS)],
        compiler_params=_cparams("parallel", "parallel"),
        name="gqa_prep",
    )(p3, p3, p3, p3, p3, cos, sin, gq, gqs, gk, gks)


def _attn_kernel(q_ref, k_ref, v_ref, o_ref, s0_ref, s1_ref, p0_ref, p1_ref, acc_ref, *, tk, nk):
    q = q_ref[...]
    tq = q.shape[0]

    s_refs, p_refs = (s0_ref, s1_ref), (p0_ref, p1_ref)

    def scores(c):
        s_refs[c % 2][...] = lax.dot_general(q, k_ref[tk * c:tk * (c + 1), :], (((1,), (1,)), ((), ())),
                                             preferred_element_type=F32)

    def probs(c, m):
        s = s_refs[c % 2][...]
        m_new = jnp.maximum(m, jnp.max(s, axis=-1, keepdims=True))
        p_refs[c % 2][...] = jnp.exp2(s - m_new).astype(BF16)
        return jnp.exp2(m - m_new), m_new

    def accumulate(c, alpha):
        pv = jnp.dot(p_refs[c % 2][...], v_ref[tk * c:tk * (c + 1), :], preferred_element_type=F32)
        acc_ref[...] = pv if c == 0 else alpha * acc_ref[...] + pv

    m = jnp.full((tq, 1), -1e30, F32)
    alphas = {}
    scores(0)
    for c in range(nk):
        if c + 1 < nk:
            scores(c + 1)
        alphas[c], m = probs(c, m)
        if c >= 1:
            accumulate(c - 1, alphas[c - 1])
    accumulate(nk - 1, alphas[nk - 1])
    acc = acc_ref[...]
    lane = lax.broadcasted_iota(jnp.int32, acc.shape, 1)
    o_ref[...] = jnp.where(lane < HDIM, acc / acc[:, HDIM:HDIM + 1], 0.0).astype(BF16)


def _attention(q, k, v):
    bsz, nh, seq, _ = q.shape
    rep = nh // k.shape[1]
    tq = min(1024, seq)
    tk = min(512, seq // 2)
    nk = seq // tk
    kern = functools.partial(_attn_kernel, tk=tk, nk=nk)
    kv_spec = pl.BlockSpec((None, None, seq, LANES), lambda b, h, i: (b, h // rep, 0, 0))
    return pl.pallas_call(
        kern,
        grid=(bsz, nh, seq // tq),
        in_specs=[pl.BlockSpec((None, None, tq, LANES), lambda b, h, i: (b, h, i, 0)), kv_spec, kv_spec],
        out_specs=pl.BlockSpec((None, tq, LANES), lambda b, h, i: (b, i, h)),
        out_shape=jax.ShapeDtypeStruct((bsz, seq, nh * LANES), BF16),
        scratch_shapes=[pltpu.VMEM((tq, tk), F32), pltpu.VMEM((tq, tk), F32),
                        pltpu.VMEM((tq, tk), BF16), pltpu.VMEM((tq, tk), BF16),
                        pltpu.VMEM((tq, LANES), F32)],
        compiler_params=_cparams("parallel", "parallel", "arbitrary"),
        name="attention",
    )(q, k, v)


def _ssd_kernel(*refs, bsz, nc):
    ins = refs[:8]
    (cw_ref, cb_ref, dtb_ref, na_ref, trif_ref, trib_ref, e2_ref, ew_ref, smask_ref, dsk_ref,
     yf_ref, yb_ref, s_ref) = refs[8:]
    step = pl.program_id(0)
    q_len = C_CHUNK

    @pl.when(step == 0)
    def _():
        s_ref[...] = jnp.zeros_like(s_ref)

    chains = [(d, b) for d in range(2) for b in range(bsz)]
    chunk = (step, nc - 1 - step)
    tri = (trif_ref[...], trib_ref[...])
    lane = lax.broadcasted_iota(jnp.int32, (q_len, LANES), 1)
    ri = lax.broadcasted_iota(jnp.int32, (q_len, q_len), 0)
    ci = lax.broadcasted_iota(jnp.int32, (q_len, q_len), 1)
    causal = (ri >= ci, ri <= ci)
    lane_w = lax.broadcasted_iota(jnp.int32, (q_len, GROUP_W), 1)
    smask = smask_ref[...] > 0

    xs, bm, cm, dt_slot, cum_slot = [], [], [], [], []
    for d, b in chains:
        xp_ref, x_ref, xn_ref, sm_ref = ins[4 * d:4 * d + 4]
        prev_row = xp_ref[b, SUBLANES - 1:SUBLANES, :] * jnp.where(chunk[d] > 0, 1.0, 0.0)
        next_row = xn_ref[b, 0:1, :] * jnp.where(chunk[d] < nc - 1, 1.0, 0.0)
        act = _silu(_conv3(x_ref[b], prev_row, next_row, cw_ref, cb_ref[...]))
        xs.append(act[:, :GROUP_W])
        bm.append(act[:, GROUP_W:GROUP_W + LANES])
        cm.append(act[:, GROUP_W + LANES:])
        dt = _softplus(sm_ref[b] + dtb_ref[...])
        dt_slot.append(dt)
        cum_slot.append(_dot_sel_lhs(tri[d], dt * na_ref[...]))
    cum_w = [_dot_sel_rhs(cum_slot[i], ew_ref[d]) for i, (d, b) in enumerate(chains)]
    cum_2 = [_dot_sel_rhs(cum_slot[i], e2_ref[d]) for i, (d, b) in enumerate(chains)]
    dt_w = [_dot_sel_rhs(dt_slot[i], ew_ref[d]) for i, (d, b) in enumerate(chains)]
    cb = [_dot_nt(jnp.concatenate([jnp.where(lane < C_STATE, c, 0.0), jnp.where(lane >= C_STATE, c, 0.0)], axis=0), bk)
          for c, bk in zip(cm, bm)]
    s_in = [s_ref[bsz * d + b] for d, b in chains]
    y_off = [_dot(cm[i], s_in[i]) * jnp.exp(cum_w[i]) for i in range(len(chains))]
    xdt = [xs[i] * dt_w[i] for i in range(len(chains))]
    for i, (d, b) in enumerate(chains):
        edge = 0 if d else q_len - 1
        cum_edge = cum_w[i][edge:edge + 1, :]
        s_new = jnp.where(smask, _dot_tn(bm[i], xdt[i] * jnp.exp(cum_edge - cum_w[i])), 0.0)
        s_ref[bsz * d + b] = s_in[i] * jnp.exp(cum_edge) + s_new
    y = list(y_off)
    for h in range(HEADS):
        g = h // 2
        head_lanes = (lane_w >= HDIM * h) & (lane_w < HDIM * (h + 1))
        for i, (d, b) in enumerate(chains):
            a_bc = cum_2[i][:, LANES * h:LANES * (h + 1)]
            decay = jnp.exp(jnp.where(causal[d], a_bc - a_bc.T, -1e30))
            scores = cb[i][q_len * g:q_len * (g + 1), :] * decay
            yh = jnp.dot(scores.astype(BF16), xdt[i].astype(BF16), preferred_element_type=F32)
            y[i] = y[i] + jnp.where(head_lanes, yh, 0.0)
    for i, (d, b) in enumerate(chains):
        if d == 0:
            yf_ref[b] = y[i] + xs[i] * dsk_ref[...]
        else:
            yb_ref[b] = y[i]


def _ssd_scan(p3, conv_w, conv_b, dtb_slot, na_slot, dsk_w):
    bsz, seq, _ = p3.shape
    q_len = C_CHUNK
    nc = seq // q_len
    hb = q_len // SUBLANES
    nblk8 = seq // SUBLANES
    ch = (lambda s: s, lambda s: nc - 1 - s)
    trif = _np_bf16(_tri_blocks(q_len, q_len, upper=False))
    trib = _np_bf16(_tri_blocks(q_len, q_len, upper=True))
    e2 = _np_bf16(np.stack([_head_expand(SM_DT + HEADS * d, LANES) for d in range(2)]))
    ew = _np_bf16(np.stack([_head_expand(SM_DT + HEADS * d, HDIM) for d in range(2)]))
    gi = np.arange(LANES)[:, None] // C_STATE
    hi = np.arange(GROUP_W)[None, :] // HDIM
    smask = jnp.asarray((gi == hi // 2).astype(np.float32))

    def dir_specs(d):
        return [pl.BlockSpec((bsz, SUBLANES, C_XBC), lambda s: (0, jnp.maximum(ch[d](s) * hb - 1, 0), 0)),
                pl.BlockSpec((bsz, q_len, C_XBC), lambda s: (0, ch[d](s), OFF_C_XBC // C_XBC)),
                pl.BlockSpec((bsz, SUBLANES, C_XBC), lambda s: (0, jnp.minimum((ch[d](s) + 1) * hb, nblk8 - 1), 0)),
                pl.BlockSpec((bsz, q_len, LANES), lambda s: (0, ch[d](s), OFF_SMALL // LANES))]

    full = lambda a: pl.BlockSpec(a.shape, lambda s: (0,) * a.ndim)
    consts = [conv_w, conv_b, dtb_slot, na_slot, trif, trib, e2, ew, smask, dsk_w]
    out_blk = lambda d: pl.BlockSpec((bsz, q_len, GROUP_W), lambda s: (0, ch[d](s), 0))
    out_shape = jax.ShapeDtypeStruct((bsz, seq, GROUP_W), F32)
    return pl.pallas_call(
        functools.partial(_ssd_kernel, bsz=bsz, nc=nc),
        grid=(nc,),
        in_specs=dir_specs(0) + dir_specs(1) + [full(a) for a in consts],
        out_specs=[out_blk(0), out_blk(1)],
        out_shape=[out_shape, out_shape],
        scratch_shapes=[pltpu.VMEM((2 * bsz, LANES, GROUP_W), F32)],
        compiler_params=_cparams("arbitrary"),
        name="ssd_scan",
    )(*([p3] * 8), *consts)


def _delta_local_kernel(xp_ref, x_ref, xn_ref, sm_ref, cw_ref, na_ref, dtb_ref, ones_ref, trif_ref, trib_ref,
                        eg_ref, eb_ref, bd_ref, u_ref, w_ref, qk_ref, qd_ref, kd_ref, ge_ref, *, nblk, cb):
    blk = pl.program_id(1)
    q_len = D_CHUNK
    x = x_ref[...]
    prev_row = xp_ref[SUBLANES - 1:SUBLANES, :] * jnp.where(blk > 0, 1.0, 0.0)
    next_row = xn_ref[0:1, :] * jnp.where(blk < nblk - 1, 1.0, 0.0)
    act = _silu(_conv3(x, prev_row, next_row, cw_ref, None))
    ones_bd = ones_ref[...]

    def l2n(t):
        ss = _dot_sel_rhs(t * t, ones_bd)
        return t * lax.rsqrt(ss + 1e-6)

    q_all = l2n(act[:, :GROUP_W]) * (HDIM ** -0.5)
    k_all = l2n(act[:, GROUP_W:2 * GROUP_W])
    v_all = act[:, 2 * GROUP_W:]
    sm = sm_ref[...]
    beta_slot = jax.nn.sigmoid(sm)
    g_slot = na_ref[...] * _softplus(sm + dtb_ref[...])
    bd_mask = bd_ref[...] > 0

    lane = lax.broadcasted_iota(jnp.int32, (q_len, GROUP_W), 1) & (q_len - 1)
    row = lax.broadcasted_iota(jnp.int32, (q_len, GROUP_W), 0)
    eye_w = jnp.where(row == lane, 1.0, 0.0)
    incl = (row >= lane, row <= lane)
    strict = (row > lane, row < lane)
    tri = (trif_ref[...], trib_ref[...])
    g_w = [_dot_sel_rhs(g_slot, eg_ref[d]) for d in range(2)]
    beta_w = [_dot_sel_rhs(beta_slot, eb_ref[d]) for d in range(2)]

    units = [(d, c) for d in range(2) for c in range(cb)]
    rows_of = lambda c: slice(q_len * c, q_len * (c + 1))
    seg, g_cum = [], []
    for d, c in units:
        gu = g_w[d][rows_of(c)]
        sg = _dot_sel_lhs(tri[d], jnp.concatenate([jnp.where(strict[d], gu, 0.0), gu], axis=1))
        seg.append(sg[:, :GROUP_W])
        g_cum.append(sg[:, GROUP_W:])
    kq = []
    for d, c in units:
        kc = k_all[rows_of(c)]
        lhs = jnp.concatenate([kc * beta_w[d][rows_of(c)], q_all[rows_of(c)]], axis=0)
        kq.append(_dot_nt(lhs, _block_diag(kc, bd_mask)))
    decay = [jnp.exp(jnp.where(incl[d], seg[i], -1e30)) for i, (d, c) in enumerate(units)]
    n_w = [jnp.where(strict[d], -(kq[i][:q_len] * decay[i]), 0.0) for i, (d, c) in enumerate(units)]
    p_w = [eye_w + n for n in n_w]
    m_w = [_dot(n, _block_diag(n, bd_mask)) for n in n_w]
    for _ in range(4):
        r = [_dot(jnp.concatenate([m, p], axis=0), _block_diag(m, bd_mask)) for m, p in zip(m_w, p_w)]
        m_w = [ri[:q_len] for ri in r]
        p_w = [p + ri[q_len:] for p, ri in zip(p_w, r)]
    t_w = [p + _dot(p, _block_diag(m, bd_mask)) for m, p in zip(m_w, p_w)]
    for i, (d, c) in enumerate(units):
        rs = rows_of(c)
        kc, bw = k_all[rs], beta_w[d][rs]
        exp_g = jnp.exp(g_cum[i])
        edge = 0 if d else q_len - 1
        g_edge = g_cum[i][edge:edge + 1, :]
        rhs = jnp.concatenate([_block_diag(v_all[rs] * bw, bd_mask), _block_diag(kc * bw * exp_g, bd_mask)], axis=1)
        uw = _dot(t_w[i], rhs)
        u_ref[d, rs, :] = uw[:, :GROUP_W]
        w_ref[d, rs, :] = uw[:, GROUP_W:].astype(BF16)
        qk_ref[d, rs, :] = (kq[i][q_len:] * decay[i]).astype(BF16)
        qd_ref[d, rs, :] = (q_all[rs] * exp_g).astype(BF16)
        kd_ref[d, rs, :] = (kc * jnp.exp(g_edge - g_cum[i])).astype(BF16)
        ge_ref[d, SUBLANES * c:SUBLANES * (c + 1), :] = jnp.broadcast_to(jnp.exp(g_edge), (SUBLANES, GROUP_W))


def _delta_local(p3, conv_w, na_slot, dtb_slot):
    bsz, seq, _ = p3.shape
    q_len = D_CHUNK
    rows = min(256, seq)
    cb = rows // q_len
    nblk = seq // rows
    hb = rows // SUBLANES
    nblk8 = seq // SUBLANES
    ones_bd = _np_bf16(np.kron(np.eye(HEADS), np.ones((HDIM, HDIM))))
    trif = _np_bf16(_tri_blocks(q_len, q_len, upper=False))
    trib = _np_bf16(_tri_blocks(q_len, q_len, upper=True))
    eg = _np_bf16(np.stack([_head_expand(SM_AB + 8 + HEADS * d, HDIM) for d in range(2)]))
    eb = _np_bf16(np.stack([_head_expand(SM_AB + HEADS * d, HDIM) for d in range(2)]))
    bd = jnp.asarray(np.kron(np.eye(HEADS), np.ones((q_len, HDIM))).astype(np.float32))
    x_spec = pl.BlockSpec((None, rows, D_QKV), lambda b, i: (b, i, OFF_D_QKV // D_QKV))
    xp_spec = pl.BlockSpec((None, SUBLANES, D_QKV), lambda b, i: (b, jnp.maximum(i * hb - 1, 0), OFF_D_QKV // D_QKV))
    xn_spec = pl.BlockSpec((None, SUBLANES, D_QKV), lambda b, i: (b, jnp.minimum((i + 1) * hb, nblk8 - 1), OFF_D_QKV // D_QKV))
    sm_spec = pl.BlockSpec((None, rows, LANES), lambda b, i: (b, i, OFF_SMALL // LANES))
    full = lambda a: pl.BlockSpec(a.shape, lambda b, i: (0,) * a.ndim)
    consts = [conv_w, na_slot, dtb_slot, ones_bd, trif, trib, eg, eb, bd]
    out_blk = pl.BlockSpec((None, 2, rows, GROUP_W), lambda b, i: (b, 0, i, 0))
    ge_blk = pl.BlockSpec((None, 2, SUBLANES * cb, GROUP_W), lambda b, i: (b, 0, i, 0))
    shp = lambda dt: jax.ShapeDtypeStruct((bsz, 2, seq, GROUP_W), dt)
    return pl.pallas_call(
        functools.partial(_delta_local_kernel, nblk=nblk, cb=cb),
        grid=(bsz, nblk),
        in_specs=[xp_spec, x_spec, xn_spec, sm_spec] + [full(a) for a in consts],
        out_specs=[out_blk] * 5 + [ge_blk],
        out_shape=[shp(F32), shp(BF16), shp(BF16), shp(BF16), shp(BF16),
                   jax.ShapeDtypeStruct((bsz, 2, seq // q_len * SUBLANES, GROUP_W), F32)],
        compiler_params=_cparams("parallel", "parallel"),
        name="delta_local",
    )(p3, p3, p3, p3, *consts)


def _delta_scan_kernel(*refs, bsz):
    ins, (bd_ref, of_ref, ob_ref, s_ref) = refs[:12], refs[12:]
    o_refs = (of_ref, ob_ref)

    @pl.when(pl.program_id(0) == 0)
    def _():
        s_ref[...] = jnp.zeros_like(s_ref)

    q_len = D_CHUNK
    bd_mask = bd_ref[...] > 0
    chains = [(d, b) for d in range(2) for b in range(bsz)]
    get = lambda k, d, b: ins[6 * d + k][b]
    s = [s_ref[bsz * d + b] for d, b in chains]
    r = [jnp.dot(jnp.concatenate([get(1, d, b), get(3, d, b)], axis=0), s[i].astype(BF16),
                 preferred_element_type=F32) for i, (d, b) in enumerate(chains)]
    v_new = [get(0, d, b) - r[i][:q_len] for i, (d, b) in enumerate(chains)]
    upd = [lax.dot_general(get(4, d, b), v_new[i].astype(BF16), (((0,), (0,)), ((), ())),
                           preferred_element_type=F32) for i, (d, b) in enumerate(chains)]
    for i, (d, b) in enumerate(chains):
        s_ref[bsz * d + b] = s[i] * get(5, d, b)[0:1, :] + jnp.where(bd_mask, upd[i], 0.0)
    for i, (d, b) in enumerate(chains):
        o_refs[d][b] = r[i][q_len:] + jnp.dot(get(2, d, b), _block_diag(v_new[i], bd_mask),
                                              preferred_element_type=F32)


def _delta_scan(u, w, qk, qd, kd, ge):
    bsz, _, seq, _ = u.shape
    q_len = D_CHUNK
    nc = seq // q_len
    bd = jnp.asarray(np.kron(np.eye(HEADS), np.ones((q_len, HDIM))).astype(np.float32))
    ch = (lambda c: c, lambda c: nc - 1 - c)
    blk = lambda d, rows: pl.BlockSpec((bsz, None, rows, GROUP_W), lambda c: (0, d, ch[d](c), 0))
    in_specs = [blk(d, rows) for d in range(2) for rows in (q_len,) * 5 + (SUBLANES,)]
    out_blk = lambda d: pl.BlockSpec((bsz, q_len, GROUP_W), lambda c: (0, ch[d](c), 0))
    out_shape = jax.ShapeDtypeStruct((bsz, seq, GROUP_W), F32)
    return pl.pallas_call(
        functools.partial(_delta_scan_kernel, bsz=bsz),
        grid=(nc,),
        in_specs=in_specs + [pl.BlockSpec(bd.shape, lambda c: (0, 0))],
        out_specs=[out_blk(0), out_blk(1)],
        out_shape=[out_shape, out_shape],
        scratch_shapes=[pltpu.VMEM((2 * bsz, GROUP_W, GROUP_W), F32)],
        compiler_params=_cparams("arbitrary"),
        name="delta_scan",
    )(*([u, w, qk, qd, kd, ge] * 2), bd)


def _out_proj_kernel(oa_ref, ob_ref, ycf_ref, ycb_ref, zc_ref, odf_ref, odb_ref, zd_ref, x_ref,
                     wa_ref, wb_ref, wc_ref, wd_ref, ga_ref, gb_ref, gc_ref, gd_ref, ones_ref, gp_ref, o_ref):
    oa = (_rms(oa_ref[...].astype(F32), GROUP_W) * ga_ref[...]).astype(BF16)
    ob = (_rms(ob_ref[...].astype(F32), GROUP_W) * gb_ref[...]).astype(BF16)
    oc = (_rms((ycf_ref[...] + ycb_ref[...]) * _silu(zc_ref[...]), GROUP_W) * gc_ref[...]).astype(BF16)
    od = odf_ref[...] + odb_ref[...]
    ms = _dot_sel_rhs(od * od, ones_ref[...]) * (1.0 / HDIM)
    od = od * lax.rsqrt(ms + EPS) * gd_ref[...] * _silu(zd_ref[...])
    acc = jnp.dot(oa, wa_ref[...], preferred_element_type=F32)
    acc += jnp.dot(ob, wb_ref[...], preferred_element_type=F32)
    acc += jnp.dot(oc, wc_ref[...], preferred_element_type=F32)
    acc += jnp.dot(od.astype(BF16), wd_ref[...], preferred_element_type=F32)
    o_ref[...] = x_ref[...] + _rms(acc, D_MODEL) * gp_ref[...]


def _out_proj(oa, ob, ycf, ycb, odf, odb, p3, x3, wa, wb, wc, wd, ga, gb, gc, gd, gpost):
    bsz, seq, _ = x3.shape
    tm = min(512, seq)
    ones_bd = _np_bf16(np.kron(np.eye(HEADS), np.ones((HDIM, HDIM))))
    rows = lambda w: pl.BlockSpec((None, tm, w), lambda b, i: (b, i, 0))
    full = lambda a: pl.BlockSpec(a.shape, lambda b, i: (0,) * a.ndim)
    z_spec = lambda off: pl.BlockSpec((None, tm, GROUP_W), lambda b, i: (b, i, off // GROUP_W))
    consts = [wa, wb, wc, wd, ga, gb, gc, gd, ones_bd, gpost]
    return pl.pallas_call(
        _out_proj_kernel,
        grid=(bsz, seq // tm),
        in_specs=[rows(HEADS * LANES), rows(HEADS * LANES), rows(GROUP_W), rows(GROUP_W), z_spec(OFF_C_Z),
                  rows(GROUP_W), rows(GROUP_W), z_spec(OFF_D_Z), rows(D_MODEL)] + [full(a) for a in consts],
        out_specs=rows(D_MODEL),
        out_shape=jax.ShapeDtypeStruct(x3.shape, F32),
        compiler_params=_cparams("parallel", "parallel"),
        name="out_proj",
    )(oa, ob, ycf, ycb, p3, odf, odb, p3, x3, *consts)


def _ffn_kernel(xp_ref, x_ref, xn_ref, gpre_ref, win_ref, cw_ref, cb_ref, wout_ref, gpost_ref, o_ref,
                gate_ref, up_ref, act_ref, *, tiles_per_seq, nj):
    tm = x_ref.shape[0]
    halo = BF16_ROWS
    pos = lax.rem(pl.program_id(0), tiles_per_seq)
    norm = lambda t: _rms(t, D_MODEL) * gpre_ref[...]
    h = jnp.concatenate([(norm(xp_ref[...]) * jnp.where(pos == 0, 0.0, 1.0)).astype(BF16),
                         norm(x_ref[...]).astype(BF16),
                         (norm(xn_ref[...]) * jnp.where(pos == tiles_per_seq - 1, 0.0, 1.0)).astype(BF16)], axis=0)
    n = tm + 2 * halo
    body = slice(halo, halo + tm)

    def project(j):
        for ref, off in ((gate_ref, 0), (up_ref, D_FF)):
            ref[j % 2] = jnp.dot(h, win_ref[:, off + FF_TILE * j:off + FF_TILE * (j + 1)],
                                 preferred_element_type=F32)

    def conv(ref, off, j):
        t = ref[j % 2]
        cs = slice(off + FF_TILE * j, off + FF_TILE * (j + 1))
        return (pltpu.roll(t, 1, axis=0)[body] * cw_ref[0:1, cs] + t[body] * cw_ref[1:2, cs]
                + pltpu.roll(t, n - 1, axis=0)[body] * cw_ref[2:3, cs] + cb_ref[:, cs])

    project(0)
    for j in range(nj):
        if j + 1 < nj:
            project(j + 1)
        act_ref[:, FF_TILE * j:FF_TILE * (j + 1)] = (
            _silu(conv(gate_ref, 0, j)) * conv(up_ref, D_FF, j)).astype(BF16)
    acc = jnp.dot(act_ref[...], wout_ref[...], preferred_element_type=F32)
    o_ref[...] = x_ref[...] + _rms(acc, D_MODEL) * gpost_ref[...]


def _ffn(x2d, seq, layer, gpre, w_in, conv_w, conv_b, w_out, gpost):
    m = x2d.shape[0]
    tm = min(512, seq)
    nj = D_FF // FF_TILE
    hb = tm // BF16_ROWS
    nblk = m // BF16_ROWS
    kern = functools.partial(_ffn_kernel, tiles_per_seq=seq // tm, nj=nj)
    const = lambda a: pl.BlockSpec((None,) + a.shape[1:], lambda i: (layer, 0, 0), pipeline_mode=pl.Buffered(1))
    return pl.pallas_call(
        kern,
        grid=(m // tm,),
        in_specs=[pl.BlockSpec((BF16_ROWS, D_MODEL), lambda i: (jnp.maximum(i * hb - 1, 0), 0)),
                  pl.BlockSpec((tm, D_MODEL), lambda i: (i, 0)),
                  pl.BlockSpec((BF16_ROWS, D_MODEL), lambda i: (jnp.minimum((i + 1) * hb, nblk - 1), 0)),
                  const(gpre), const(w_in), const(conv_w), const(conv_b), const(w_out), const(gpost)],
        out_specs=pl.BlockSpec((tm, D_MODEL), lambda i: (i, 0)),
        out_shape=jax.ShapeDtypeStruct(x2d.shape, F32),
        scratch_shapes=[pltpu.VMEM((2, tm + 2 * BF16_ROWS, FF_TILE), F32),
                        pltpu.VMEM((2, tm + 2 * BF16_ROWS, FF_TILE), F32),
                        pltpu.VMEM((tm, D_FF), BF16)],
        compiler_params=_cparams("parallel"),
        name="conv_ffn",
    )(x2d, x2d, x2d, gpre, w_in, conv_w, conv_b, w_out, gpost)


def _slot_vec(pairs):
    v = jnp.zeros((1, LANES), F32)
    for off, vals in pairs:
        v = v.at[0, off:off + vals.shape[0]].set(vals.astype(F32))
    return v


def _layer(x3, tabs_a, tabs_b, prm, layer, ffn_prm):
    bsz, seq, _ = x3.shape
    m = bsz * seq
    x2d = x3.reshape(m, D_MODEL)
    p3 = _in_proj(x2d, prm["pre_mix_norm"], _arrange_w_in(prm["w_in"])).reshape(bsz, seq, IN_COLS_PADDED)

    wuq = prm["a_w_uq"].reshape(A_Q_LORA, HEADS, A_NOPE + A_ROPE)
    rope_w = wuq[..., A_NOPE:]
    zpad = jnp.zeros((A_Q_LORA, HEADS, LANES - A_NOPE - A_ROPE), F32)
    w1 = jnp.concatenate([wuq[..., :A_NOPE], rope_w, zpad], axis=-1).reshape(A_Q_LORA, HEADS * LANES)
    w2 = jnp.concatenate([jnp.zeros_like(wuq[..., :A_NOPE]), _rot_last(rope_w), zpad], axis=-1).reshape(A_Q_LORA, HEADS * LANES)
    rpad = ((0, 256 - A_Q_LORA), (0, 0))
    w1, w2 = jnp.pad(w1, rpad).astype(BF16), jnp.pad(w2, rpad).astype(BF16)
    wukv = prm["a_w_ukv"].reshape(A_KV_LORA, HEADS, A_NOPE + HDIM)
    hpad = jnp.zeros((A_KV_LORA, HEADS, LANES - HDIM), F32)
    wk = jnp.concatenate([wukv[..., :A_NOPE], hpad], axis=-1).reshape(A_KV_LORA, HEADS * LANES).astype(BF16)
    wv = jnp.concatenate([wukv[..., A_NOPE:], hpad], axis=-1).reshape(A_KV_LORA, HEADS * LANES).astype(BF16)
    gq = jnp.pad(prm["a_q_norm"], (0, 256 - A_Q_LORA)).reshape(1, 256)
    gkv = prm["a_kv_norm"].reshape(1, A_KV_LORA)
    qa, ka, va = _mla_prep(p3, tabs_a, gq, gkv, w1, w2, wk, wv)
    oa = _attention(qa, ka, va)

    cos_b, sin_b = tabs_b
    c_b = (HDIM ** -0.5) * LOG2E
    pad64 = lambda g: jnp.pad(g, (0, LANES - HDIM)).reshape(1, LANES)
    qb, kb, vb = _gqa_prep(p3, cos_b, sin_b,
                           pad64(prm["b_q_norm"] * c_b), pad64(_swap_last(prm["b_q_norm"]) * c_b),
                           pad64(prm["b_k_norm"]), pad64(_swap_last(prm["b_k_norm"])))
    ob = _attention(qb, kb, vb)

    na_c = -jnp.exp(prm["c_a_log"].astype(F32)).reshape(-1)
    dtb_slot = _slot_vec([(SM_DT, prm["c_dt_bias"].reshape(-1))])
    na_slot = _slot_vec([(SM_DT, na_c)])
    conv_b = prm["c_conv_b"].reshape(1, -1)
    dsk_w = jnp.repeat(prm["c_d_skip"].astype(F32), HDIM).reshape(1, GROUP_W)
    ycf, ycb = _ssd_scan(p3, prm["c_conv_w"], conv_b, dtb_slot, na_slot, dsk_w)

    na_d = -jnp.exp(prm["d_a_log"].astype(F32)).reshape(-1)
    d_na_slot = _slot_vec([(SM_AB + 8, na_d)])
    d_dtb_slot = _slot_vec([(SM_AB + 8, prm["d_dt_bias"].reshape(-1))])
    u, w, qk, qd, kd, ge = _delta_local(p3, prm["d_conv_w"], d_na_slot, d_dtb_slot)
    odf, odb = _delta_scan(u, w, qk, qd, kd, ge)

    wo = prm["w_out"]
    wa = _pad_heads(wo[0:256], 0, HEADS, HDIM, LANES).astype(BF16)
    wb = _pad_heads(wo[256:512], 0, HEADS, HDIM, LANES).astype(BF16)
    wc, wd = wo[512:768].astype(BF16), wo[768:1024].astype(BF16)
    ga = _pad_heads(prm["a_out_norm"], 0, HEADS, HDIM, LANES).reshape(1, -1)
    gb = _pad_heads(prm["b_out_norm"], 0, HEADS, HDIM, LANES).reshape(1, -1)
    gd = jnp.tile(prm["d_out_norm"], HEADS).reshape(1, -1)
    gc = prm["c_out_norm"].reshape(1, -1)
    x3 = _out_proj(oa, ob, ycf, ycb, odf, odb, p3, x3, wa, wb, wc, wd, ga, gb, gc, gd,
                   prm["post_mix_norm"].reshape(1, -1))

    x2d = _ffn(x3.reshape(m, D_MODEL), seq, layer, *ffn_prm)
    return x2d.reshape(bsz, seq, D_MODEL)


def _attn_tables(seq):
    cos_a, sin_a = _rope_tables(seq, A_ROPE)
    cos_b, sin_b = _rope_tables(seq, HDIM)
    c_a = ((A_NOPE + A_ROPE) ** -0.5) * LOG2E
    ones = jnp.ones((seq, A_NOPE), F32)
    z64 = jnp.zeros((seq, A_NOPE), F32)
    z32 = jnp.zeros((seq, LANES - A_NOPE - A_ROPE), F32)
    cq = c_a * jnp.concatenate([ones, cos_a, z32], axis=1)
    sq = c_a * jnp.concatenate([z64, sin_a, z32], axis=1)
    ck = jnp.concatenate([z64, cos_a, z32], axis=1)
    sk = jnp.concatenate([z64, sin_a, z32], axis=1)
    padb = lambda t: jnp.pad(t, ((0, 0), (0, LANES - HDIM)))
    return (cq, sq, ck, sk), (padb(cos_b), padb(sin_b))


def kernel(x, pre_mix_norm, w_in, a_q_norm, a_w_uq, a_kv_norm, a_w_ukv, a_out_norm, b_q_norm, b_k_norm, b_out_norm, c_conv_w, c_conv_b, c_a_log, c_dt_bias, c_d_skip, c_out_norm, d_conv_w, d_a_log, d_dt_bias, d_out_norm, w_out, post_mix_norm, pre_ffn_norm, f_w_in, f_conv_w, f_conv_b, f_w_out, post_ffn_norm):
    params = dict(pre_mix_norm=pre_mix_norm, w_in=w_in, a_q_norm=a_q_norm, a_w_uq=a_w_uq, a_kv_norm=a_kv_norm,
                  a_w_ukv=a_w_ukv, a_out_norm=a_out_norm, b_q_norm=b_q_norm, b_k_norm=b_k_norm,
                  b_out_norm=b_out_norm, c_conv_w=c_conv_w, c_conv_b=c_conv_b, c_a_log=c_a_log,
                  c_dt_bias=c_dt_bias, c_d_skip=c_d_skip, c_out_norm=c_out_norm, d_conv_w=d_conv_w,
                  d_a_log=d_a_log, d_dt_bias=d_dt_bias, d_out_norm=d_out_norm, w_out=w_out,
                  post_mix_norm=post_mix_norm)
    ffn_prm = (pre_ffn_norm[:, None, :], f_w_in.astype(BF16), f_conv_w, f_conv_b[:, None, :],
               f_w_out.astype(BF16), post_ffn_norm[:, None, :])
    tabs_a, tabs_b = _attn_tables(x.shape[1])
    for layer in range(w_in.shape[0]):
        x = _layer(x, tabs_a, tabs_b, {k: v[layer] for k, v in params.items()}, layer, ffn_prm)
    return x
```

```python
import functools
import math

import numpy as np
import jax
import jax.numpy as jnp
from jax import lax
from jax.experimental import pallas as pl
from jax.experimental.pallas import tpu as pltpu

F32 = jnp.float32
BF16 = jnp.bfloat16

LANES = 128
SUBLANES = 8
BF16_ROWS = 16
VMEM_LIMIT = 56 * 1024 * 1024

EPS = 1e-6
ROPE_BASE = 10000.0
GRID_W = 64
D_MODEL = 1024
GROUP_W = 256
HEADS = 4
HDIM = 64

A_NOPE, A_ROPE, A_Q_LORA, A_KV_LORA = 64, 32, 192, 128
B_KV_HEADS = 2
C_STATE, C_CHUNK, C_XBC = 64, 128, 512
D_CHUNK, D_QKV = 64, 768
D_FF = 2816
FF_TILE = 256

OFF_C_XBC, OFF_D_Z, OFF_D_QKV, OFF_C_Z, OFF_SMALL = 0, 512, 768, 1536, 1792
P_HBM_COLS = 1920
OFF_A_CQ, OFF_B_Q, OFF_B_QROT = 1920, 2176, 2432
OFF_A_CKV, OFF_B_K, OFF_B_KROT, OFF_B_V = 2688, 2816, 2944, 3072
IN_COLS_PADDED = 3200
SM_DT, SM_AB, SM_KR, SM_KRROT = 0, 16, 64, 96

LOG2E = math.log2(math.e)


def _cparams(*sem):
    return pltpu.CompilerParams(dimension_semantics=sem, vmem_limit_bytes=VMEM_LIMIT)


def _dot(a, b):
    return jnp.dot(a.astype(BF16), b.astype(BF16), preferred_element_type=F32)


def _dot_nt(a, b):
    return lax.dot_general(a.astype(BF16), b.astype(BF16), (((1,), (1,)), ((), ())),
                           preferred_element_type=F32)


def _dot_tn(a, b):
    return lax.dot_general(a.astype(BF16), b.astype(BF16), (((0,), (0,)), ((), ())),
                           preferred_element_type=F32)


def _split3(x):
    hi = x.astype(BF16)
    r1 = x - hi.astype(F32)
    mid = r1.astype(BF16)
    lo = (r1 - mid.astype(F32)).astype(BF16)
    return hi, mid, lo


def _dot_sel_rhs(x, sel):
    hi, mid, lo = _split3(x)
    d = lambda p: jnp.dot(p, sel, preferred_element_type=F32)
    return d(hi) + d(mid) + d(lo)


def _dot_sel_lhs(sel, x):
    hi, mid, lo = _split3(x)
    d = lambda p: jnp.dot(sel, p, preferred_element_type=F32)
    return d(hi) + d(mid) + d(lo)


def _softplus(x):
    return jnp.maximum(x, 0.0) + jnp.log1p(jnp.exp(-jnp.abs(x)))


def _silu(x):
    return x * jax.nn.sigmoid(x)


def _rms(x, n):
    return x * lax.rsqrt(jnp.sum(x * x, axis=-1, keepdims=True) * (1.0 / n) + EPS)


def _conv3(x, prev_row, next_row, w_ref, bias):
    n = x.shape[0]
    row = lax.broadcasted_iota(jnp.int32, x.shape, 0)
    xm1 = jnp.where(row == 0, prev_row, pltpu.roll(x, 1, axis=0))
    xp1 = jnp.where(row == n - 1, next_row, pltpu.roll(x, n - 1, axis=0))
    y = xm1 * w_ref[0:1, :] + x * w_ref[1:2, :] + xp1 * w_ref[2:3, :]
    return y if bias is None else y + bias


def _block_diag(x, mask):
    return jnp.where(mask, jnp.concatenate([x] * HEADS, axis=0), 0.0).astype(BF16)


def _np_bf16(a):
    return jnp.asarray(np.asarray(a, np.float32), BF16)


def _head_expand(first_lane, width):
    e = np.zeros((LANES, HEADS * width), np.float32)
    for h in range(HEADS):
        e[first_lane + h, h * width:(h + 1) * width] = 1.0
    return e


def _tri_blocks(n, blk, upper):
    i = np.arange(n)
    same = (i[:, None] // blk) == (i[None, :] // blk)
    tri = (i[:, None] <= i[None, :]) if upper else (i[:, None] >= i[None, :])
    return (same & tri).astype(np.float32)


def _rot_last(w):
    r = w.shape[-1]
    xs = w.reshape(w.shape[:-1] + (2, 2, r // 4))
    return jnp.stack([-xs[..., 1, :], xs[..., 0, :]], axis=-2).reshape(w.shape)


def _swap_last(w):
    r = w.shape[-1]
    xs = w.reshape(w.shape[:-1] + (2, 2, r // 4))
    return jnp.stack([xs[..., 1, :], xs[..., 0, :]], axis=-2).reshape(w.shape)


def _rope_tables(seq_len, rot_dim):
    rows = seq_len // GRID_W
    sec = rot_dim // 2
    inv_freq = ROPE_BASE ** (-jnp.arange(0, sec, 2, dtype=F32) / sec)
    ang_r = jnp.arange(rows).astype(F32)[:, None] * inv_freq
    ang_c = jnp.arange(GRID_W).astype(F32)[:, None] * inv_freq

    def table(fn):
        t_r = jnp.repeat(fn(ang_r), GRID_W, axis=0)
        t_c = jnp.tile(fn(ang_c), (rows, 1))
        return jnp.concatenate([t_r, t_r, t_c, t_c], axis=-1)

    return table(jnp.cos), table(jnp.sin)


def _pad_heads(a, axis, n_heads, real, padded):
    shp = list(a.shape)
    a = a.reshape(shp[:axis] + [n_heads, real] + shp[axis + 1:])
    pad = [(0, 0)] * a.ndim
    pad[axis + 1] = (0, padded - real)
    a = jnp.pad(a, pad)
    return a.reshape(shp[:axis] + [n_heads * padded] + shp[axis + 1:])


def _arrange_w_in(w):
    a0, b0, c0, d0 = 0, 352, 864, 1640
    zeros = lambda n: jnp.zeros((w.shape[0], n), w.dtype)
    cq, ckv, kr = w[:, a0:a0 + 192], w[:, a0 + 192:a0 + 320], w[:, a0 + 320:a0 + 352]
    bq, bk, bv = w[:, b0:b0 + 256], w[:, b0 + 256:b0 + 384], w[:, b0 + 384:b0 + 512]
    cz, cxbc, cdt = w[:, c0:c0 + 256], w[:, c0 + 256:c0 + 768], w[:, c0 + 768:c0 + 776]
    dqkv, dz, dab = w[:, d0:d0 + 768], w[:, d0 + 768:d0 + 1024], w[:, d0 + 1024:d0 + 1040]
    bq_rot = _rot_last(bq.reshape(-1, HEADS, HDIM)).reshape(-1, 256)
    bk_rot = _rot_last(bk.reshape(-1, B_KV_HEADS, HDIM)).reshape(-1, 128)
    small = jnp.concatenate([cdt, zeros(8), dab, zeros(32), kr, _rot_last(kr)], axis=1)
    cols = [cxbc, dz, dqkv, cz, small, cq, zeros(64), bq, bq_rot, ckv, bk, bk_rot, bv]
    out = jnp.concatenate(cols, axis=1)
    assert out.shape[1] == IN_COLS_PADDED
    return out.astype(BF16)


def _mla_heads(cq, ckv, sm, ckt, skt, gq_ref, gkv_ref, w1_ref, w2_ref, wk_ref, wv_ref, q_ref, k_ref, v_ref, c_a):
    cqn = (_rms(cq, A_Q_LORA) * gq_ref[...]).astype(BF16)
    q1 = jnp.dot(cqn, w1_ref[...], preferred_element_type=F32)
    q2 = jnp.dot(cqn, w2_ref[...], preferred_element_type=F32)
    kvn = (_rms(ckv, A_KV_LORA) * gkv_ref[...]).astype(BF16)
    k1 = jnp.dot(kvn, wk_ref[...], preferred_element_type=F32)
    v1 = jnp.dot(kvn, wv_ref[...], preferred_element_type=F32)
    k_rope = sm * ckt + pltpu.roll(sm, LANES - A_ROPE, axis=1) * skt
    lane = lax.broadcasted_iota(jnp.int32, sm.shape, 1)
    cqt = jnp.where(lane < A_NOPE, c_a, c_a * ckt)
    sqt = c_a * skt
    ones_lane = lane == HDIM
    for h in range(HEADS):
        sl = slice(LANES * h, LANES * (h + 1))
        q_ref[h] = (q1[:, sl] * cqt + q2[:, sl] * sqt).astype(BF16)
        k_ref[h] = (k1[:, sl] + k_rope).astype(BF16)
        v_ref[h] = jnp.where(ones_lane, 1.0, v1[:, sl]).astype(BF16)


def _head_slot(x, h, lo):
    grp = x[:, LANES * (h // 2):LANES * (h // 2 + 1)]
    if h % 2:
        grp = pltpu.roll(grp, HDIM, axis=1)
    return jnp.where(lo, grp, 0.0)


def _gqa_heads(q, qr, k, kr, v, cos, sin, gq_ref, gqs_ref, gk_ref, gks_ref, qo_ref, ko_ref, vo_ref):
    lane = lax.broadcasted_iota(jnp.int32, cos.shape, 1)
    lo = lane < HDIM
    ones_lane = lane == HDIM

    def normed_rope(x, xr, h, g, gs):
        xh, xrh = _head_slot(x, h, lo), _head_slot(xr, h, lo)
        r = lax.rsqrt(jnp.sum(xh * xh, axis=-1, keepdims=True) * (1.0 / HDIM) + EPS)
        return (r * (xh * (cos * g) + xrh * (sin * gs))).astype(BF16)

    for h in range(HEADS):
        qo_ref[h] = normed_rope(q, qr, h, gq_ref[...], gqs_ref[...])
    for h in range(B_KV_HEADS):
        ko_ref[h] = normed_rope(k, kr, h, gk_ref[...], gks_ref[...])
        vo_ref[h] = jnp.where(ones_lane, 1.0, _head_slot(v, h, lo)).astype(BF16)


def _in_proj_kernel(x_ref, g_ref, w_ref, ckt_ref, skt_ref, cosb_ref, sinb_ref, gq_ref, gkv_ref, w1_ref, w2_ref,
                    wk_ref, wv_ref, gbq_ref, gbqs_ref, gbk_ref, gbks_ref,
                    p_ref, qa_ref, ka_ref, va_ref, qb_ref, kb_ref, vb_ref, *, c_a):
    h = (_rms(x_ref[...], D_MODEL) * g_ref[...]).astype(BF16)
    p_att = jnp.dot(h, w_ref[:, P_HBM_COLS:], preferred_element_type=F32)
    p_hbm = jnp.dot(h, w_ref[:, :P_HBM_COLS], preferred_element_type=F32)
    p_ref[...] = p_hbm

    def col(off, w):
        src, base = (p_hbm, 0) if off < P_HBM_COLS else (p_att, P_HBM_COLS)
        return src[:, off - base:off - base + w]

    _mla_heads(col(OFF_A_CQ, 256), col(OFF_A_CKV, LANES), col(OFF_SMALL, LANES), ckt_ref[...], skt_ref[...],
               gq_ref, gkv_ref, w1_ref, w2_ref, wk_ref, wv_ref, qa_ref, ka_ref, va_ref, c_a)
    _gqa_heads(col(OFF_B_Q, 256), col(OFF_B_QROT, 256), col(OFF_B_K, LANES), col(OFF_B_KROT, LANES),
               col(OFF_B_V, LANES), cosb_ref[...], sinb_ref[...], gbq_ref, gbqs_ref, gbk_ref, gbks_ref,
               qb_ref, kb_ref, vb_ref)


def _in_proj(x3, gain, w, tabs, mla_prm, gqa_prm):
    bsz, seq, _ = x3.shape
    m = bsz * seq
    tm = min(512, seq)
    tps = seq // tm
    c_a = ((A_NOPE + A_ROPE) ** -0.5) * LOG2E
    consts = list(mla_prm) + list(gqa_prm)
    tab = pl.BlockSpec((tm, LANES), lambda i: (i % tps, 0))
    full = lambda a: pl.BlockSpec(a.shape, lambda i: (0,) * a.ndim)
    heads = lambda n: pl.BlockSpec((None, n, tm, LANES), lambda i: (i // tps, 0, i % tps, 0))
    shp = lambda n: jax.ShapeDtypeStruct((bsz, n, seq, LANES), BF16)
    return pl.pallas_call(
        functools.partial(_in_proj_kernel, c_a=c_a),
        grid=(m // tm,),
        in_specs=[pl.BlockSpec((tm, D_MODEL), lambda i: (i, 0)),
                  pl.BlockSpec((1, D_MODEL), lambda i: (0, 0)),
                  pl.BlockSpec((D_MODEL, IN_COLS_PADDED), lambda i: (0, 0), pipeline_mode=pl.Buffered(1)),
                  tab, tab, tab, tab] + [full(a) for a in consts],
        out_specs=[pl.BlockSpec((tm, P_HBM_COLS), lambda i: (i, 0)),
                   heads(HEADS), heads(HEADS), heads(HEADS), heads(HEADS), heads(B_KV_HEADS), heads(B_KV_HEADS)],
        out_shape=[jax.ShapeDtypeStruct((m, P_HBM_COLS), F32),
                   shp(HEADS), shp(HEADS), shp(HEADS), shp(HEADS), shp(B_KV_HEADS), shp(B_KV_HEADS)],
        compiler_params=_cparams("parallel"),
        name="in_proj",
    )(x3.reshape(m, D_MODEL), gain.reshape(1, -1), w, *tabs, *consts)


def _attn_kernel(q_ref, k_ref, v_ref, o_ref, s0_ref, s1_ref, p0_ref, p1_ref, acc_ref, *, tk, nk):
    q = q_ref[...]
    tq = q.shape[0]

    s_refs, p_refs = (s0_ref, s1_ref), (p0_ref, p1_ref)

    def scores(c):
        s_refs[c % 2][...] = lax.dot_general(q, k_ref[tk * c:tk * (c + 1), :], (((1,), (1,)), ((), ())),
                                             preferred_element_type=F32)

    def probs(c, m):
        s = s_refs[c % 2][...]
        m_new = jnp.maximum(m, jnp.max(s, axis=-1, keepdims=True))
        p_refs[c % 2][...] = jnp.exp2(s - m_new).astype(BF16)
        return jnp.exp2(m - m_new), m_new

    def accumulate(c, alpha):
        pv = jnp.dot(p_refs[c % 2][...], v_ref[tk * c:tk * (c + 1), :], preferred_element_type=F32)
        acc_ref[...] = pv if c == 0 else alpha * acc_ref[...] + pv

    m = jnp.full((tq, 1), -1e30, F32)
    alphas = {}
    scores(0)
    for c in range(nk):
        if c + 1 < nk:
            scores(c + 1)
        alphas[c], m = probs(c, m)
        if c >= 1:
            accumulate(c - 1, alphas[c - 1])
    accumulate(nk - 1, alphas[nk - 1])
    acc = acc_ref[...]
    lane = lax.broadcasted_iota(jnp.int32, acc.shape, 1)
    o_ref[...] = jnp.where(lane < HDIM, acc / acc[:, HDIM:HDIM + 1], 0.0).astype(BF16)


def _attention(q, k, v):
    bsz, nh, seq, _ = q.shape
    rep = nh // k.shape[1]
    tq = min(1024, seq)
    tk = min(512, seq // 2)
    nk = seq // tk
    kern = functools.partial(_attn_kernel, tk=tk, nk=nk)
    kv_spec = pl.BlockSpec((None, None, seq, LANES), lambda b, h, i: (b, h // rep, 0, 0))
    return pl.pallas_call(
        kern,
        grid=(bsz, nh, seq // tq),
        in_specs=[pl.BlockSpec((None, None, tq, LANES), lambda b, h, i: (b, h, i, 0)), kv_spec, kv_spec],
        out_specs=pl.BlockSpec((None, tq, LANES), lambda b, h, i: (b, i, h)),
        out_shape=jax.ShapeDtypeStruct((bsz, seq, nh * LANES), BF16),
        scratch_shapes=[pltpu.VMEM((tq, tk), F32), pltpu.VMEM((tq, tk), F32),
                        pltpu.VMEM((tq, tk), BF16), pltpu.VMEM((tq, tk), BF16),
                        pltpu.VMEM((tq, LANES), F32)],
        compiler_params=_cparams("parallel", "parallel", "arbitrary"),
        name="attention",
    )(q, k, v)


def _ssd_kernel(*refs, bsz, nc):
    ins = refs[:8]
    (cw_ref, cb_ref, dtb_ref, na_ref, trif_ref, trib_ref, e2_ref, ew_ref, smask_ref, dsk_ref,
     yf_ref, yb_ref, s_ref) = refs[8:]
    step = pl.program_id(0)
    q_len = C_CHUNK

    @pl.when(step == 0)
    def _():
        s_ref[...] = jnp.zeros_like(s_ref)

    chains = [(d, b) for d in range(2) for b in range(bsz)]
    chunk = (step, nc - 1 - step)
    tri = (trif_ref[...], trib_ref[...])
    lane = lax.broadcasted_iota(jnp.int32, (q_len, LANES), 1)
    ri = lax.broadcasted_iota(jnp.int32, (q_len, q_len), 0)
    ci = lax.broadcasted_iota(jnp.int32, (q_len, q_len), 1)
    causal = (ri >= ci, ri <= ci)
    lane_w = lax.broadcasted_iota(jnp.int32, (q_len, GROUP_W), 1)
    smask = smask_ref[...] > 0

    xs, bm, cm, dt_slot, cum_slot = [], [], [], [], []
    for d, b in chains:
        xp_ref, x_ref, xn_ref, sm_ref = ins[4 * d:4 * d + 4]
        prev_row = xp_ref[b, SUBLANES - 1:SUBLANES, :] * jnp.where(chunk[d] > 0, 1.0, 0.0)
        next_row = xn_ref[b, 0:1, :] * jnp.where(chunk[d] < nc - 1, 1.0, 0.0)
        act = _silu(_conv3(x_ref[b], prev_row, next_row, cw_ref, cb_ref[...]))
        xs.append(act[:, :GROUP_W])
        bm.append(act[:, GROUP_W:GROUP_W + LANES])
        cm.append(act[:, GROUP_W + LANES:])
        dt = _softplus(sm_ref[b] + dtb_ref[...])
        dt_slot.append(dt)
        cum_slot.append(_dot_sel_lhs(tri[d], dt * na_ref[...]))
    cum_w = [_dot_sel_rhs(cum_slot[i], ew_ref[d]) for i, (d, b) in enumerate(chains)]
    cum_2 = [_dot_sel_rhs(cum_slot[i], e2_ref[d]) for i, (d, b) in enumerate(chains)]
    dt_w = [_dot_sel_rhs(dt_slot[i], ew_ref[d]) for i, (d, b) in enumerate(chains)]
    cb = [_dot_nt(jnp.concatenate([jnp.where(lane < C_STATE, c, 0.0), jnp.where(lane >= C_STATE, c, 0.0)], axis=0), bk)
          for c, bk in zip(cm, bm)]
    s_in = [s_ref[bsz * d + b] for d, b in chains]
    y_off = [_dot(cm[i], s_in[i]) * jnp.exp(cum_w[i]) for i in range(len(chains))]
    xdt = [xs[i] * dt_w[i] for i in range(len(chains))]
    for i, (d, b) in enumerate(chains):
        edge = 0 if d else q_len - 1
        cum_edge = cum_w[i][edge:edge + 1, :]
        s_new = jnp.where(smask, _dot_tn(bm[i], xdt[i] * jnp.exp(cum_edge - cum_w[i])), 0.0)
        s_ref[bsz * d + b] = s_in[i] * jnp.exp(cum_edge) + s_new
    y = list(y_off)
    for h in range(HEADS):
        g = h // 2
        head_lanes = (lane_w >= HDIM * h) & (lane_w < HDIM * (h + 1))
        for i, (d, b) in enumerate(chains):
            a_bc = cum_2[i][:, LANES * h:LANES * (h + 1)]
            decay = jnp.exp(jnp.where(causal[d], a_bc - a_bc.T, -1e30))
            scores = cb[i][q_len * g:q_len * (g + 1), :] * decay
            yh = jnp.dot(scores.astype(BF16), xdt[i].astype(BF16), preferred_element_type=F32)
            y[i] = y[i] + jnp.where(head_lanes, yh, 0.0)
    for i, (d, b) in enumerate(chains):
        if d == 0:
            yf_ref[b] = y[i] + xs[i] * dsk_ref[...]
        else:
            yb_ref[b] = y[i]


def _ssd_scan(p3, conv_w, conv_b, dtb_slot, na_slot, dsk_w):
    bsz, seq, _ = p3.shape
    q_len = C_CHUNK
    nc = seq // q_len
    hb = q_len // SUBLANES
    nblk8 = seq // SUBLANES
    ch = (lambda s: s, lambda s: nc - 1 - s)
    trif = _np_bf16(_tri_blocks(q_len, q_len, upper=False))
    trib = _np_bf16(_tri_blocks(q_len, q_len, upper=True))
    e2 = _np_bf16(np.stack([_head_expand(SM_DT + HEADS * d, LANES) for d in range(2)]))
    ew = _np_bf16(np.stack([_head_expand(SM_DT + HEADS * d, HDIM) for d in range(2)]))
    gi = np.arange(LANES)[:, None] // C_STATE
    hi = np.arange(GROUP_W)[None, :] // HDIM
    smask = jnp.asarray((gi == hi // 2).astype(np.float32))

    def dir_specs(d):
        return [pl.BlockSpec((bsz, SUBLANES, C_XBC), lambda s: (0, jnp.maximum(ch[d](s) * hb - 1, 0), 0)),
                pl.BlockSpec((bsz, q_len, C_XBC), lambda s: (0, ch[d](s), OFF_C_XBC // C_XBC)),
                pl.BlockSpec((bsz, SUBLANES, C_XBC), lambda s: (0, jnp.minimum((ch[d](s) + 1) * hb, nblk8 - 1), 0)),
                pl.BlockSpec((bsz, q_len, LANES), lambda s: (0, ch[d](s), OFF_SMALL // LANES))]

    full = lambda a: pl.BlockSpec(a.shape, lambda s: (0,) * a.ndim)
    consts = [conv_w, conv_b, dtb_slot, na_slot, trif, trib, e2, ew, smask, dsk_w]
    out_blk = lambda d: pl.BlockSpec((bsz, q_len, GROUP_W), lambda s: (0, ch[d](s), 0))
    out_shape = jax.ShapeDtypeStruct((bsz, seq, GROUP_W), F32)
    return pl.pallas_call(
        functools.partial(_ssd_kernel, bsz=bsz, nc=nc),
        grid=(nc,),
        in_specs=dir_specs(0) + dir_specs(1) + [full(a) for a in consts],
        out_specs=[out_blk(0), out_blk(1)],
        out_shape=[out_shape, out_shape],
        scratch_shapes=[pltpu.VMEM((2 * bsz, LANES, GROUP_W), F32)],
        compiler_params=_cparams("arbitrary"),
        name="ssd_scan",
    )(*([p3] * 8), *consts)


def _delta_local_kernel(xp_ref, x_ref, xn_ref, sm_ref, cw_ref, na_ref, dtb_ref, ones_ref, trif_ref, trib_ref,
                        eg_ref, eb_ref, bd_ref, u_ref, w_ref, qk_ref, qd_ref, kd_ref, ge_ref, *, nblk, cb):
    blk = pl.program_id(1)
    q_len = D_CHUNK
    x = x_ref[...]
    prev_row = xp_ref[SUBLANES - 1:SUBLANES, :] * jnp.where(blk > 0, 1.0, 0.0)
    next_row = xn_ref[0:1, :] * jnp.where(blk < nblk - 1, 1.0, 0.0)
    act = _silu(_conv3(x, prev_row, next_row, cw_ref, None))
    ones_bd = ones_ref[...]

    def l2n(t):
        ss = _dot_sel_rhs(t * t, ones_bd)
        return t * lax.rsqrt(ss + 1e-6)

    q_all = l2n(act[:, :GROUP_W]) * (HDIM ** -0.5)
    k_all = l2n(act[:, GROUP_W:2 * GROUP_W])
    v_all = act[:, 2 * GROUP_W:]
    sm = sm_ref[...]
    beta_slot = jax.nn.sigmoid(sm)
    g_slot = na_ref[...] * _softplus(sm + dtb_ref[...])
    bd_mask = bd_ref[...] > 0

    lane = lax.broadcasted_iota(jnp.int32, (q_len, GROUP_W), 1) & (q_len - 1)
    row = lax.broadcasted_iota(jnp.int32, (q_len, GROUP_W), 0)
    eye_w = jnp.where(row == lane, 1.0, 0.0)
    incl = (row >= lane, row <= lane)
    strict = (row > lane, row < lane)
    tri = (trif_ref[...], trib_ref[...])
    g_w = [_dot_sel_rhs(g_slot, eg_ref[d]) for d in range(2)]
    beta_w = [_dot_sel_rhs(beta_slot, eb_ref[d]) for d in range(2)]

    units = [(d, c) for d in range(2) for c in range(cb)]
    rows_of = lambda c: slice(q_len * c, q_len * (c + 1))
    seg, g_cum = [], []
    for d, c in units:
        gu = g_w[d][rows_of(c)]
        sg = _dot_sel_lhs(tri[d], jnp.concatenate([jnp.where(strict[d], gu, 0.0), gu], axis=1))
        seg.append(sg[:, :GROUP_W])
        g_cum.append(sg[:, GROUP_W:])
    kq = []
    for d, c in units:
        kc = k_all[rows_of(c)]
        lhs = jnp.concatenate([kc * beta_w[d][rows_of(c)], q_all[rows_of(c)]], axis=0)
        kq.append(_dot_nt(lhs, _block_diag(kc, bd_mask)))
    decay = [jnp.exp(jnp.where(incl[d], seg[i], -1e30)) for i, (d, c) in enumerate(units)]
    n_w = [jnp.where(strict[d], -(kq[i][:q_len] * decay[i]), 0.0) for i, (d, c) in enumerate(units)]
    p_w = [eye_w + n for n in n_w]
    m_w = [_dot(n, _block_diag(n, bd_mask)) for n in n_w]
    for _ in range(4):
        r = [_dot(jnp.concatenate([m, p], axis=0), _block_diag(m, bd_mask)) for m, p in zip(m_w, p_w)]
        m_w = [ri[:q_len] for ri in r]
        p_w = [p + ri[q_len:] for p, ri in zip(p_w, r)]
    t_w = [p + _dot(p, _block_diag(m, bd_mask)) for m, p in zip(m_w, p_w)]
    for i, (d, c) in enumerate(units):
        rs = rows_of(c)
        kc, bw = k_all[rs], beta_w[d][rs]
        exp_g = jnp.exp(g_cum[i])
        edge = 0 if d else q_len - 1
        g_edge = g_cum[i][edge:edge + 1, :]
        rhs = jnp.concatenate([_block_diag(v_all[rs] * bw, bd_mask), _block_diag(kc * bw * exp_g, bd_mask)], axis=1)
        uw = _dot(t_w[i], rhs)
        u_ref[d, rs, :] = uw[:, :GROUP_W]
        w_ref[d, rs, :] = uw[:, GROUP_W:].astype(BF16)
        qk_ref[d, rs, :] = (kq[i][q_len:] * decay[i]).astype(BF16)
        qd_ref[d, rs, :] = (q_all[rs] * exp_g).astype(BF16)
        kd_ref[d, rs, :] = (kc * jnp.exp(g_edge - g_cum[i])).astype(BF16)
        ge_ref[d, SUBLANES * c:SUBLANES * (c + 1), :] = jnp.broadcast_to(jnp.exp(g_edge), (SUBLANES, GROUP_W))


def _delta_local(p3, conv_w, na_slot, dtb_slot):
    bsz, seq, _ = p3.shape
    q_len = D_CHUNK
    rows = min(256, seq)
    cb = rows // q_len
    nblk = seq // rows
    hb = rows // SUBLANES
    nblk8 = seq // SUBLANES
    ones_bd = _np_bf16(np.kron(np.eye(HEADS), np.ones((HDIM, HDIM))))
    trif = _np_bf16(_tri_blocks(q_len, q_len, upper=False))
    trib = _np_bf16(_tri_blocks(q_len, q_len, upper=True))
    eg = _np_bf16(np.stack([_head_expand(SM_AB + 8 + HEADS * d, HDIM) for d in range(2)]))
    eb = _np_bf16(np.stack([_head_expand(SM_AB + HEADS * d, HDIM) for d in range(2)]))
    bd = jnp.asarray(np.kron(np.eye(HEADS), np.ones((q_len, HDIM))).astype(np.float32))
    x_spec = pl.BlockSpec((None, rows, D_QKV), lambda b, i: (b, i, OFF_D_QKV // D_QKV))
    xp_spec = pl.BlockSpec((None, SUBLANES, D_QKV), lambda b, i: (b, jnp.maximum(i * hb - 1, 0), OFF_D_QKV // D_QKV))
    xn_spec = pl.BlockSpec((None, SUBLANES, D_QKV), lambda b, i: (b, jnp.minimum((i + 1) * hb, nblk8 - 1), OFF_D_QKV // D_QKV))
    sm_spec = pl.BlockSpec((None, rows, LANES), lambda b, i: (b, i, OFF_SMALL // LANES))
    full = lambda a: pl.BlockSpec(a.shape, lambda b, i: (0,) * a.ndim)
    consts = [conv_w, na_slot, dtb_slot, ones_bd, trif, trib, eg, eb, bd]
    out_blk = pl.BlockSpec((None, 2, rows, GROUP_W), lambda b, i: (b, 0, i, 0))
    ge_blk = pl.BlockSpec((None, 2, SUBLANES * cb, GROUP_W), lambda b, i: (b, 0, i, 0))
    shp = lambda dt: jax.ShapeDtypeStruct((bsz, 2, seq, GROUP_W), dt)
    return pl.pallas_call(
        functools.partial(_delta_local_kernel, nblk=nblk, cb=cb),
        grid=(bsz, nblk),
        in_specs=[xp_spec, x_spec, xn_spec, sm_spec] + [full(a) for a in consts],
        out_specs=[out_blk] * 5 + [ge_blk],
        out_shape=[shp(F32), shp(BF16), shp(BF16), shp(BF16), shp(BF16),
                   jax.ShapeDtypeStruct((bsz, 2, seq // q_len * SUBLANES, GROUP_W), F32)],
        compiler_params=_cparams("parallel", "parallel"),
        name="delta_local",
    )(p3, p3, p3, p3, *consts)


def _delta_scan_kernel(*refs, bsz):
    ins, (bd_ref, of_ref, ob_ref, s_ref) = refs[:12], refs[12:]
    o_refs = (of_ref, ob_ref)

    @pl.when(pl.program_id(0) == 0)
    def _():
        s_ref[...] = jnp.zeros_like(s_ref)

    q_len = D_CHUNK
    bd_mask = bd_ref[...] > 0
    chains = [(d, b) for d in range(2) for b in range(bsz)]
    get = lambda k, d, b: ins[6 * d + k][b]
    s = [s_ref[bsz * d + b] for d, b in chains]
    r = [jnp.dot(jnp.concatenate([get(1, d, b), get(3, d, b)], axis=0), s[i].astype(BF16),
                 preferred_element_type=F32) for i, (d, b) in enumerate(chains)]
    v_new = [get(0, d, b) - r[i][:q_len] for i, (d, b) in enumerate(chains)]
    upd = [lax.dot_general(get(4, d, b), v_new[i].astype(BF16), (((0,), (0,)), ((), ())),
                           preferred_element_type=F32) for i, (d, b) in enumerate(chains)]
    for i, (d, b) in enumerate(chains):
        s_ref[bsz * d + b] = s[i] * get(5, d, b)[0:1, :] + jnp.where(bd_mask, upd[i], 0.0)
    for i, (d, b) in enumerate(chains):
        o_refs[d][b] = r[i][q_len:] + jnp.dot(get(2, d, b), _block_diag(v_new[i], bd_mask),
                                              preferred_element_type=F32)


def _delta_scan(u, w, qk, qd, kd, ge):
    bsz, _, seq, _ = u.shape
    q_len = D_CHUNK
    nc = seq // q_len
    bd = jnp.asarray(np.kron(np.eye(HEADS), np.ones((q_len, HDIM))).astype(np.float32))
    ch = (lambda c: c, lambda c: nc - 1 - c)
    blk = lambda d, rows: pl.BlockSpec((bsz, None, rows, GROUP_W), lambda c: (0, d, ch[d](c), 0))
    in_specs = [blk(d, rows) for d in range(2) for rows in (q_len,) * 5 + (SUBLANES,)]
    out_blk = lambda d: pl.BlockSpec((bsz, q_len, GROUP_W), lambda c: (0, ch[d](c), 0))
    out_shape = jax.ShapeDtypeStruct((bsz, seq, GROUP_W), F32)
    return pl.pallas_call(
        functools.partial(_delta_scan_kernel, bsz=bsz),
        grid=(nc,),
        in_specs=in_specs + [pl.BlockSpec(bd.shape, lambda c: (0, 0))],
        out_specs=[out_blk(0), out_blk(1)],
        out_shape=[out_shape, out_shape],
        scratch_shapes=[pltpu.VMEM((2 * bsz, GROUP_W, GROUP_W), F32)],
        compiler_params=_cparams("arbitrary"),
        name="delta_scan",
    )(*([u, w, qk, qd, kd, ge] * 2), bd)


def _out_proj_kernel(oa_ref, ob_ref, ycf_ref, ycb_ref, zc_ref, odf_ref, odb_ref, zd_ref, x_ref,
                     wa_ref, wb_ref, wc_ref, wd_ref, ga_ref, gb_ref, gc_ref, gd_ref, ones_ref, gp_ref, o_ref):
    oa = (_rms(oa_ref[...].astype(F32), GROUP_W) * ga_ref[...]).astype(BF16)
    ob = (_rms(ob_ref[...].astype(F32), GROUP_W) * gb_ref[...]).astype(BF16)
    oc = (_rms((ycf_ref[...] + ycb_ref[...]) * _silu(zc_ref[...]), GROUP_W) * gc_ref[...]).astype(BF16)
    od = odf_ref[...] + odb_ref[...]
    ms = _dot_sel_rhs(od * od, ones_ref[...]) * (1.0 / HDIM)
    od = od * lax.rsqrt(ms + EPS) * gd_ref[...] * _silu(zd_ref[...])
    acc = jnp.dot(oa, wa_ref[...], preferred_element_type=F32)
    acc += jnp.dot(ob, wb_ref[...], preferred_element_type=F32)
    acc += jnp.dot(oc, wc_ref[...], preferred_element_type=F32)
    acc += jnp.dot(od.astype(BF16), wd_ref[...], preferred_element_type=F32)
    o_ref[...] = x_ref[...] + _rms(acc, D_MODEL) * gp_ref[...]


def _out_proj(oa, ob, ycf, ycb, odf, odb, p3, x3, wa, wb, wc, wd, ga, gb, gc, gd, gpost):
    bsz, seq, _ = x3.shape
    tm = min(512, seq)
    ones_bd = _np_bf16(np.kron(np.eye(HEADS), np.ones((HDIM, HDIM))))
    rows = lambda w: pl.BlockSpec((None, tm, w), lambda b, i: (b, i, 0))
    full = lambda a: pl.BlockSpec(a.shape, lambda b, i: (0,) * a.ndim)
    z_spec = lambda off: pl.BlockSpec((None, tm, GROUP_W), lambda b, i: (b, i, off // GROUP_W))
    consts = [wa, wb, wc, wd, ga, gb, gc, gd, ones_bd, gpost]
    return pl.pallas_call(
        _out_proj_kernel,
        grid=(bsz, seq // tm),
        in_specs=[rows(HEADS * LANES), rows(HEADS * LANES), rows(GROUP_W), rows(GROUP_W), z_spec(OFF_C_Z),
                  rows(GROUP_W), rows(GROUP_W), z_spec(OFF_D_Z), rows(D_MODEL)] + [full(a) for a in consts],
        out_specs=rows(D_MODEL),
        out_shape=jax.ShapeDtypeStruct(x3.shape, F32),
        compiler_params=_cparams("parallel", "parallel"),
        name="out_proj",
    )(oa, ob, ycf, ycb, p3, odf, odb, p3, x3, *consts)


def _ffn_kernel(xp_ref, x_ref, xn_ref, gpre_ref, win_ref, cw_ref, cb_ref, wout_ref, gpost_ref, o_ref,
                gate_ref, up_ref, act_ref, *, tiles_per_seq, nj):
    tm = x_ref.shape[0]
    halo = BF16_ROWS
    pos = lax.rem(pl.program_id(0), tiles_per_seq)
    norm = lambda t: _rms(t, D_MODEL) * gpre_ref[...]
    h = jnp.concatenate([(norm(xp_ref[...]) * jnp.where(pos == 0, 0.0, 1.0)).astype(BF16),
                         norm(x_ref[...]).astype(BF16),
                         (norm(xn_ref[...]) * jnp.where(pos == tiles_per_seq - 1, 0.0, 1.0)).astype(BF16)], axis=0)
    n = tm + 2 * halo
    body = slice(halo, halo + tm)

    def project(j):
        for ref, off in ((gate_ref, 0), (up_ref, D_FF)):
            ref[j % 2] = jnp.dot(h, win_ref[:, off + FF_TILE * j:off + FF_TILE * (j + 1)],
                                 preferred_element_type=F32)

    def conv(ref, off, j):
        t = ref[j % 2]
        cs = slice(off + FF_TILE * j, off + FF_TILE * (j + 1))
        return (pltpu.roll(t, 1, axis=0)[body] * cw_ref[0:1, cs] + t[body] * cw_ref[1:2, cs]
                + pltpu.roll(t, n - 1, axis=0)[body] * cw_ref[2:3, cs] + cb_ref[:, cs])

    project(0)
    for j in range(nj):
        if j + 1 < nj:
            project(j + 1)
        act_ref[:, FF_TILE * j:FF_TILE * (j + 1)] = (
            _silu(conv(gate_ref, 0, j)) * conv(up_ref, D_FF, j)).astype(BF16)
    acc = jnp.dot(act_ref[...], wout_ref[...], preferred_element_type=F32)
    o_ref[...] = x_ref[...] + _rms(acc, D_MODEL) * gpost_ref[...]


def _ffn(x2d, seq, layer, gpre, w_in, conv_w, conv_b, w_out, gpost):
    m = x2d.shape[0]
    tm = min(512, seq)
    nj = D_FF // FF_TILE
    hb = tm // BF16_ROWS
    nblk = m // BF16_ROWS
    kern = functools.partial(_ffn_kernel, tiles_per_seq=seq // tm, nj=nj)
    const = lambda a: pl.BlockSpec((None,) + a.shape[1:], lambda i: (layer, 0, 0), pipeline_mode=pl.Buffered(1))
    return pl.pallas_call(
        kern,
        grid=(m // tm,),
        in_specs=[pl.BlockSpec((BF16_ROWS, D_MODEL), lambda i: (jnp.maximum(i * hb - 1, 0), 0)),
                  pl.BlockSpec((tm, D_MODEL), lambda i: (i, 0)),
                  pl.BlockSpec((BF16_ROWS, D_MODEL), lambda i: (jnp.minimum((i + 1) * hb, nblk - 1), 0)),
                  const(gpre), const(w_in), const(conv_w), const(conv_b), const(w_out), const(gpost)],
        out_specs=pl.BlockSpec((tm, D_MODEL), lambda i: (i, 0)),
        out_shape=jax.ShapeDtypeStruct(x2d.shape, F32),
        scratch_shapes=[pltpu.VMEM((2, tm + 2 * BF16_ROWS, FF_TILE), F32),
                        pltpu.VMEM((2, tm + 2 * BF16_ROWS, FF_TILE), F32),
                        pltpu.VMEM((tm, D_FF), BF16)],
        compiler_params=_cparams("parallel"),
        name="conv_ffn",
    )(x2d, x2d, x2d, gpre, w_in, conv_w, conv_b, w_out, gpost)


def _slot_vec(pairs):
    v = jnp.zeros((1, LANES), F32)
    for off, vals in pairs:
        v = v.at[0, off:off + vals.shape[0]].set(vals.astype(F32))
    return v


def _layer(x3, tabs_a, tabs_b, prm, layer, ffn_prm):
    bsz, seq, _ = x3.shape
    m = bsz * seq

    wuq = prm["a_w_uq"].reshape(A_Q_LORA, HEADS, A_NOPE + A_ROPE)
    rope_w = wuq[..., A_NOPE:]
    zpad = jnp.zeros((A_Q_LORA, HEADS, LANES - A_NOPE - A_ROPE), F32)
    w1 = jnp.concatenate([wuq[..., :A_NOPE], rope_w, zpad], axis=-1).reshape(A_Q_LORA, HEADS * LANES)
    w2 = jnp.concatenate([jnp.zeros_like(wuq[..., :A_NOPE]), _rot_last(rope_w), zpad], axis=-1).reshape(A_Q_LORA, HEADS * LANES)
    rpad = ((0, 256 - A_Q_LORA), (0, 0))
    w1, w2 = jnp.pad(w1, rpad).astype(BF16), jnp.pad(w2, rpad).astype(BF16)
    wukv = prm["a_w_ukv"].reshape(A_KV_LORA, HEADS, A_NOPE + HDIM)
    hpad = jnp.zeros((A_KV_LORA, HEADS, LANES - HDIM), F32)
    wk = jnp.concatenate([wukv[..., :A_NOPE], hpad], axis=-1).reshape(A_KV_LORA, HEADS * LANES).astype(BF16)
    wv = jnp.concatenate([wukv[..., A_NOPE:], hpad], axis=-1).reshape(A_KV_LORA, HEADS * LANES).astype(BF16)
    gq = jnp.pad(prm["a_q_norm"], (0, 256 - A_Q_LORA)).reshape(1, 256)
    gkv = prm["a_kv_norm"].reshape(1, A_KV_LORA)
    c_b = (HDIM ** -0.5) * LOG2E
    pad64 = lambda g: jnp.pad(g, (0, LANES - HDIM)).reshape(1, LANES)
    gqa_prm = (pad64(prm["b_q_norm"] * c_b), pad64(_swap_last(prm["b_q_norm"]) * c_b),
               pad64(prm["b_k_norm"]), pad64(_swap_last(prm["b_k_norm"])))
    p2d, qa, ka, va, qb, kb, vb = _in_proj(x3, prm["pre_mix_norm"], _arrange_w_in(prm["w_in"]), tabs_a + tabs_b,
                                           (gq, gkv, w1, w2, wk, wv), gqa_prm)
    p3 = p2d.reshape(bsz, seq, P_HBM_COLS)
    oa = _attention(qa, ka, va)
    ob = _attention(qb, kb, vb)

    na_c = -jnp.exp(prm["c_a_log"].astype(F32)).reshape(-1)
    dtb_slot = _slot_vec([(SM_DT, prm["c_dt_bias"].reshape(-1))])
    na_slot = _slot_vec([(SM_DT, na_c)])
    conv_b = prm["c_conv_b"].reshape(1, -1)
    dsk_w = jnp.repeat(prm["c_d_skip"].astype(F32), HDIM).reshape(1, GROUP_W)
    ycf, ycb = _ssd_scan(p3, prm["c_conv_w"], conv_b, dtb_slot, na_slot, dsk_w)

    na_d = -jnp.exp(prm["d_a_log"].astype(F32)).reshape(-1)
    d_na_slot = _slot_vec([(SM_AB + 8, na_d)])
    d_dtb_slot = _slot_vec([(SM_AB + 8, prm["d_dt_bias"].reshape(-1))])
    u, w, qk, qd, kd, ge = _delta_local(p3, prm["d_conv_w"], d_na_slot, d_dtb_slot)
    odf, odb = _delta_scan(u, w, qk, qd, kd, ge)

    wo = prm["w_out"]
    wa = _pad_heads(wo[0:256], 0, HEADS, HDIM, LANES).astype(BF16)
    wb = _pad_heads(wo[256:512], 0, HEADS, HDIM, LANES).astype(BF16)
    wc, wd = wo[512:768].astype(BF16), wo[768:1024].astype(BF16)
    ga = _pad_heads(prm["a_out_norm"], 0, HEADS, HDIM, LANES).reshape(1, -1)
    gb = _pad_heads(prm["b_out_norm"], 0, HEADS, HDIM, LANES).reshape(1, -1)
    gd = jnp.tile(prm["d_out_norm"], HEADS).reshape(1, -1)
    gc = prm["c_out_norm"].reshape(1, -1)
    x3 = _out_proj(oa, ob, ycf, ycb, odf, odb, p3, x3, wa, wb, wc, wd, ga, gb, gc, gd,
                   prm["post_mix_norm"].reshape(1, -1))

    x2d = _ffn(x3.reshape(m, D_MODEL), seq, layer, *ffn_prm)
    return x2d.reshape(bsz, seq, D_MODEL)


def _attn_tables(seq):
    cos_a, sin_a = _rope_tables(seq, A_ROPE)
    cos_b, sin_b = _rope_tables(seq, HDIM)
    pada = lambda t: jnp.pad(t, ((0, 0), (A_NOPE, LANES - A_NOPE - A_ROPE)))
    padb = lambda t: jnp.pad(t, ((0, 0), (0, LANES - HDIM)))
    return (pada(cos_a), pada(sin_a)), (padb(cos_b), padb(sin_b))


def kernel(x, pre_mix_norm, w_in, a_q_norm, a_w_uq, a_kv_norm, a_w_ukv, a_out_norm, b_q_norm, b_k_norm, b_out_norm, c_conv_w, c_conv_b, c_a_log, c_dt_bias, c_d_skip, c_out_norm, d_conv_w, d_a_log, d_dt_bias, d_out_norm, w_out, post_mix_norm, pre_ffn_norm, f_w_in, f_conv_w, f_conv_b, f_w_out, post_ffn_norm):
    params = dict(pre_mix_norm=pre_mix_norm, w_in=w_in, a_q_norm=a_q_norm, a_w_uq=a_w_uq, a_kv_norm=a_kv_norm,
                  a_w_ukv=a_w_ukv, a_out_norm=a_out_norm, b_q_norm=b_q_norm, b_k_norm=b_k_norm,
                  b_out_norm=b_out_norm, c_conv_w=c_conv_w, c_conv_b=c_conv_b, c_a_log=c_a_log,
                  c_dt_bias=c_dt_bias, c_d_skip=c_d_skip, c_out_norm=c_out_norm, d_conv_w=d_conv_w,
                  d_a_log=d_a_log, d_dt_bias=d_dt_bias, d_out_norm=d_out_norm, w_out=w_out,
                  post_mix_norm=post_mix_norm)
    ffn_prm = (pre_ffn_norm[:, None, :], f_w_in.astype(BF16), f_conv_w, f_conv_b[:, None, :],
               f_w_out.astype(BF16), post_ffn_norm[:, None, :])
    tabs_a, tabs_b = _attn_tables(x.shape[1])
    for layer in range(w_in.shape[0]):
        x = _layer(x, tabs_a, tabs_b, {k: v[layer] for k, v in params.items()}, layer, ffn_prm)
    return x
```

```python
import functools
import math

import numpy as np
import jax
import jax.numpy as jnp
from jax import lax
from jax.experimental import pallas as pl
from jax.experimental.pallas import tpu as pltpu

F32 = jnp.float32
BF16 = jnp.bfloat16

LANES = 128
SUBLANES = 8
BF16_ROWS = 16
VMEM_LIMIT = 56 * 1024 * 1024

EPS = 1e-6
ROPE_BASE = 10000.0
GRID_W = 64
D_MODEL = 1024
GROUP_W = 256
HEADS = 4
HDIM = 64

A_NOPE, A_ROPE, A_Q_LORA, A_KV_LORA = 64, 32, 192, 128
B_KV_HEADS = 2
C_STATE, C_CHUNK, C_XBC = 64, 128, 512
D_CHUNK, D_QKV = 64, 768
D_FF = 2816
FF_TILE = 256

OFF_C_XBC, OFF_D_Z, OFF_D_QKV, OFF_C_Z, OFF_SMALL = 0, 512, 768, 1536, 1792
P_HBM_COLS = 1920
OFF_A_CQ, OFF_B_Q, OFF_B_QROT = 1920, 2176, 2432
OFF_A_CKV, OFF_B_K, OFF_B_KROT, OFF_B_V = 2688, 2816, 2944, 3072
IN_COLS_PADDED = 3200
SM_DT, SM_AB, SM_KR, SM_KRROT = 0, 16, 64, 96

LOG2E = math.log2(math.e)


def _cparams(*sem):
    return pltpu.CompilerParams(dimension_semantics=sem, vmem_limit_bytes=VMEM_LIMIT)


def _dot(a, b):
    return jnp.dot(a.astype(BF16), b.astype(BF16), preferred_element_type=F32)


def _dot_nt(a, b):
    return lax.dot_general(a.astype(BF16), b.astype(BF16), (((1,), (1,)), ((), ())),
                           preferred_element_type=F32)


def _dot_tn(a, b):
    return lax.dot_general(a.astype(BF16), b.astype(BF16), (((0,), (0,)), ((), ())),
                           preferred_element_type=F32)


def _split3(x):
    hi = x.astype(BF16)
    r1 = x - hi.astype(F32)
    mid = r1.astype(BF16)
    lo = (r1 - mid.astype(F32)).astype(BF16)
    return hi, mid, lo


def _dot_sel_rhs(x, sel):
    hi, mid, lo = _split3(x)
    d = lambda p: jnp.dot(p, sel, preferred_element_type=F32)
    return d(hi) + d(mid) + d(lo)


def _dot_sel_lhs(sel, x):
    hi, mid, lo = _split3(x)
    d = lambda p: jnp.dot(sel, p, preferred_element_type=F32)
    return d(hi) + d(mid) + d(lo)


def _softplus(x):
    return jnp.maximum(x, 0.0) + jnp.log1p(jnp.exp(-jnp.abs(x)))


def _silu(x):
    return x * jax.nn.sigmoid(x)


def _rms(x, n):
    return x * lax.rsqrt(jnp.sum(x * x, axis=-1, keepdims=True) * (1.0 / n) + EPS)


def _conv3(x, prev_row, next_row, w_ref, bias):
    n = x.shape[0]
    row = lax.broadcasted_iota(jnp.int32, x.shape, 0)
    xm1 = jnp.where(row == 0, prev_row, pltpu.roll(x, 1, axis=0))
    xp1 = jnp.where(row == n - 1, next_row, pltpu.roll(x, n - 1, axis=0))
    y = xm1 * w_ref[0:1, :] + x * w_ref[1:2, :] + xp1 * w_ref[2:3, :]
    return y if bias is None else y + bias


def _block_diag(x, mask):
    return jnp.where(mask, jnp.concatenate([x] * HEADS, axis=0), 0.0).astype(BF16)


def _np_bf16(a):
    return jnp.asarray(np.asarray(a, np.float32), BF16)


def _head_expand(first_lane, width):
    e = np.zeros((LANES, HEADS * width), np.float32)
    for h in range(HEADS):
        e[first_lane + h, h * width:(h + 1) * width] = 1.0
    return e


def _tri_blocks(n, blk, upper):
    i = np.arange(n)
    same = (i[:, None] // blk) == (i[None, :] // blk)
    tri = (i[:, None] <= i[None, :]) if upper else (i[:, None] >= i[None, :])
    return (same & tri).astype(np.float32)


def _rot_last(w):
    r = w.shape[-1]
    xs = w.reshape(w.shape[:-1] + (2, 2, r // 4))
    return jnp.stack([-xs[..., 1, :], xs[..., 0, :]], axis=-2).reshape(w.shape)


def _swap_last(w):
    r = w.shape[-1]
    xs = w.reshape(w.shape[:-1] + (2, 2, r // 4))
    return jnp.stack([xs[..., 1, :], xs[..., 0, :]], axis=-2).reshape(w.shape)


def _rope_tables(seq_len, rot_dim):
    rows = seq_len // GRID_W
    sec = rot_dim // 2
    inv_freq = ROPE_BASE ** (-jnp.arange(0, sec, 2, dtype=F32) / sec)
    ang_r = jnp.arange(rows).astype(F32)[:, None] * inv_freq
    ang_c = jnp.arange(GRID_W).astype(F32)[:, None] * inv_freq

    def table(fn):
        t_r = jnp.repeat(fn(ang_r), GRID_W, axis=0)
        t_c = jnp.tile(fn(ang_c), (rows, 1))
        return jnp.concatenate([t_r, t_r, t_c, t_c], axis=-1)

    return table(jnp.cos), table(jnp.sin)


def _pad_heads(a, axis, n_heads, real, padded):
    shp = list(a.shape)
    a = a.reshape(shp[:axis] + [n_heads, real] + shp[axis + 1:])
    pad = [(0, 0)] * a.ndim
    pad[axis + 1] = (0, padded - real)
    a = jnp.pad(a, pad)
    return a.reshape(shp[:axis] + [n_heads * padded] + shp[axis + 1:])


def _arrange_w_in(w):
    a0, b0, c0, d0 = 0, 352, 864, 1640
    zeros = lambda n: jnp.zeros((w.shape[0], n), w.dtype)
    cq, ckv, kr = w[:, a0:a0 + 192], w[:, a0 + 192:a0 + 320], w[:, a0 + 320:a0 + 352]
    bq, bk, bv = w[:, b0:b0 + 256], w[:, b0 + 256:b0 + 384], w[:, b0 + 384:b0 + 512]
    cz, cxbc, cdt = w[:, c0:c0 + 256], w[:, c0 + 256:c0 + 768], w[:, c0 + 768:c0 + 776]
    dqkv, dz, dab = w[:, d0:d0 + 768], w[:, d0 + 768:d0 + 1024], w[:, d0 + 1024:d0 + 1040]
    bq_rot = _rot_last(bq.reshape(-1, HEADS, HDIM)).reshape(-1, 256)
    bk_rot = _rot_last(bk.reshape(-1, B_KV_HEADS, HDIM)).reshape(-1, 128)
    small = jnp.concatenate([cdt, zeros(8), dab, zeros(32), kr, _rot_last(kr)], axis=1)
    cols = [cxbc, dz, dqkv, cz, small, cq, zeros(64), bq, bq_rot, ckv, bk, bk_rot, bv]
    out = jnp.concatenate(cols, axis=1)
    assert out.shape[1] == IN_COLS_PADDED
    return out.astype(BF16)


def _mla_heads(cq, ckv, sm, ckt, skt, gq_ref, gkv_ref, w1_ref, w2_ref, wk_ref, wv_ref, q_ref, k_ref, v_ref, c_a):
    cqn = (_rms(cq, A_Q_LORA) * gq_ref[...]).astype(BF16)
    q1 = jnp.dot(cqn, w1_ref[...], preferred_element_type=F32)
    q2 = jnp.dot(cqn, w2_ref[...], preferred_element_type=F32)
    kvn = (_rms(ckv, A_KV_LORA) * gkv_ref[...]).astype(BF16)
    k1 = jnp.dot(kvn, wk_ref[...], preferred_element_type=F32)
    v1 = jnp.dot(kvn, wv_ref[...], preferred_element_type=F32)
    k_rope = sm * ckt + pltpu.roll(sm, LANES - A_ROPE, axis=1) * skt
    lane = lax.broadcasted_iota(jnp.int32, sm.shape, 1)
    cqt = jnp.where(lane < A_NOPE, c_a, c_a * ckt)
    sqt = c_a * skt
    ones_lane = lane == HDIM
    for h in range(HEADS):
        sl = slice(LANES * h, LANES * (h + 1))
        q_ref[h] = (q1[:, sl] * cqt + q2[:, sl] * sqt).astype(BF16)
        k_ref[h] = (k1[:, sl] + k_rope).astype(BF16)
        v_ref[h] = jnp.where(ones_lane, 1.0, v1[:, sl]).astype(BF16)


def _head_slot(x, h, lo):
    grp = x[:, LANES * (h // 2):LANES * (h // 2 + 1)]
    if h % 2:
        grp = pltpu.roll(grp, HDIM, axis=1)
    return jnp.where(lo, grp, 0.0)


def _gqa_heads(q, qr, k, kr, v, cos, sin, gq_ref, gqs_ref, gk_ref, gks_ref, qo_ref, ko_ref, vo_ref):
    lane = lax.broadcasted_iota(jnp.int32, cos.shape, 1)
    lo = lane < HDIM
    ones_lane = lane == HDIM

    def normed_rope(x, xr, h, g, gs):
        xh, xrh = _head_slot(x, h, lo), _head_slot(xr, h, lo)
        r = lax.rsqrt(jnp.sum(xh * xh, axis=-1, keepdims=True) * (1.0 / HDIM) + EPS)
        return (r * (xh * (cos * g) + xrh * (sin * gs))).astype(BF16)

    for h in range(HEADS):
        qo_ref[h] = normed_rope(q, qr, h, gq_ref[...], gqs_ref[...])
    for h in range(B_KV_HEADS):
        ko_ref[h] = normed_rope(k, kr, h, gk_ref[...], gks_ref[...])
        vo_ref[h] = jnp.where(ones_lane, 1.0, _head_slot(v, h, lo)).astype(BF16)


def _in_proj_kernel(x_ref, g_ref, w_ref, ckt_ref, skt_ref, cosb_ref, sinb_ref, gq_ref, gkv_ref, w1_ref, w2_ref,
                    wk_ref, wv_ref, gbq_ref, gbqs_ref, gbk_ref, gbks_ref,
                    p_ref, qa_ref, ka_ref, va_ref, qb_ref, kb_ref, vb_ref, *, c_a):
    h = (_rms(x_ref[...], D_MODEL) * g_ref[...]).astype(BF16)
    p_att = jnp.dot(h, w_ref[:, P_HBM_COLS:], preferred_element_type=F32)
    p_hbm = jnp.dot(h, w_ref[:, :P_HBM_COLS], preferred_element_type=F32)
    p_ref[...] = p_hbm

    def col(off, w):
        src, base = (p_hbm, 0) if off < P_HBM_COLS else (p_att, P_HBM_COLS)
        return src[:, off - base:off - base + w]

    _mla_heads(col(OFF_A_CQ, 256), col(OFF_A_CKV, LANES), col(OFF_SMALL, LANES), ckt_ref[...], skt_ref[...],
               gq_ref, gkv_ref, w1_ref, w2_ref, wk_ref, wv_ref, qa_ref, ka_ref, va_ref, c_a)
    _gqa_heads(col(OFF_B_Q, 256), col(OFF_B_QROT, 256), col(OFF_B_K, LANES), col(OFF_B_KROT, LANES),
               col(OFF_B_V, LANES), cosb_ref[...], sinb_ref[...], gbq_ref, gbqs_ref, gbk_ref, gbks_ref,
               qb_ref, kb_ref, vb_ref)


def _in_proj(x3, gain, w, tabs, mla_prm, gqa_prm):
    bsz, seq, _ = x3.shape
    m = bsz * seq
    tm = min(512, seq)
    tps = seq // tm
    c_a = ((A_NOPE + A_ROPE) ** -0.5) * LOG2E
    consts = list(mla_prm) + list(gqa_prm)
    tab = pl.BlockSpec((tm, LANES), lambda i: (i % tps, 0))
    full = lambda a: pl.BlockSpec(a.shape, lambda i: (0,) * a.ndim)
    heads = lambda n: pl.BlockSpec((None, n, tm, LANES), lambda i: (i // tps, 0, i % tps, 0))
    shp = lambda n: jax.ShapeDtypeStruct((bsz, n, seq, LANES), BF16)
    return pl.pallas_call(
        functools.partial(_in_proj_kernel, c_a=c_a),
        grid=(m // tm,),
        in_specs=[pl.BlockSpec((tm, D_MODEL), lambda i: (i, 0)),
                  pl.BlockSpec((1, D_MODEL), lambda i: (0, 0)),
                  pl.BlockSpec((D_MODEL, IN_COLS_PADDED), lambda i: (0, 0), pipeline_mode=pl.Buffered(1)),
                  tab, tab, tab, tab] + [full(a) for a in consts],
        out_specs=[pl.BlockSpec((tm, P_HBM_COLS), lambda i: (i, 0)),
                   heads(HEADS), heads(HEADS), heads(HEADS), heads(HEADS), heads(B_KV_HEADS), heads(B_KV_HEADS)],
        out_shape=[jax.ShapeDtypeStruct((m, P_HBM_COLS), F32),
                   shp(HEADS), shp(HEADS), shp(HEADS), shp(HEADS), shp(B_KV_HEADS), shp(B_KV_HEADS)],
        compiler_params=_cparams("parallel"),
        name="in_proj",
    )(x3.reshape(m, D_MODEL), gain.reshape(1, -1), w, *tabs, *consts)


def _attn_kernel(q_ref, k_ref, v_ref, o_ref, s0_ref, s1_ref, p0_ref, p1_ref, acc_ref, *, tk, nk):
    q = q_ref[...]
    tq = q.shape[0]

    s_refs, p_refs = (s0_ref, s1_ref), (p0_ref, p1_ref)

    def scores(c):
        s_refs[c % 2][...] = lax.dot_general(q, k_ref[tk * c:tk * (c + 1), :], (((1,), (1,)), ((), ())),
                                             preferred_element_type=F32)

    def probs(c, m):
        s = s_refs[c % 2][...]
        m_new = jnp.maximum(m, jnp.max(s, axis=-1, keepdims=True))
        p_refs[c % 2][...] = jnp.exp2(s - m_new).astype(BF16)
        return jnp.exp2(m - m_new), m_new

    def accumulate(c, alpha):
        pv = jnp.dot(p_refs[c % 2][...], v_ref[tk * c:tk * (c + 1), :], preferred_element_type=F32)
        acc_ref[...] = pv if c == 0 else alpha * acc_ref[...] + pv

    m = jnp.full((tq, 1), -1e30, F32)
    alphas = {}
    scores(0)
    for c in range(nk):
        if c + 1 < nk:
            scores(c + 1)
        alphas[c], m = probs(c, m)
        if c >= 1:
            accumulate(c - 1, alphas[c - 1])
    accumulate(nk - 1, alphas[nk - 1])
    acc = acc_ref[...]
    lane = lax.broadcasted_iota(jnp.int32, acc.shape, 1)
    o_ref[...] = jnp.where(lane < HDIM, acc / acc[:, HDIM:HDIM + 1], 0.0).astype(BF16)


def _attention(q, k, v):
    bsz, nh, seq, _ = q.shape
    rep = nh // k.shape[1]
    tq = min(1024, seq)
    tk = min(512, seq // 2)
    nk = seq // tk
    kern = functools.partial(_attn_kernel, tk=tk, nk=nk)
    kv_spec = pl.BlockSpec((None, None, seq, LANES), lambda b, h, i: (b, h // rep, 0, 0))
    return pl.pallas_call(
        kern,
        grid=(bsz, nh, seq // tq),
        in_specs=[pl.BlockSpec((None, None, tq, LANES), lambda b, h, i: (b, h, i, 0)), kv_spec, kv_spec],
        out_specs=pl.BlockSpec((None, tq, LANES), lambda b, h, i: (b, i, h)),
        out_shape=jax.ShapeDtypeStruct((bsz, seq, nh * LANES), BF16),
        scratch_shapes=[pltpu.VMEM((tq, tk), F32), pltpu.VMEM((tq, tk), F32),
                        pltpu.VMEM((tq, tk), BF16), pltpu.VMEM((tq, tk), BF16),
                        pltpu.VMEM((tq, LANES), F32)],
        compiler_params=_cparams("parallel", "parallel", "arbitrary"),
        name="attention",
    )(q, k, v)


def _ssd_kernel(*refs, bsz, nc):
    ins = refs[:8]
    (cw_ref, cb_ref, dtb_ref, na_ref, trif_ref, trib_ref, e2_ref, ew_ref, smask_ref, dsk_ref,
     yf_ref, yb_ref, s_ref) = refs[8:]
    step = pl.program_id(0)
    q_len = C_CHUNK

    @pl.when(step == 0)
    def _():
        s_ref[...] = jnp.zeros_like(s_ref)

    chains = [(d, b) for d in range(2) for b in range(bsz)]
    chunk = (step, nc - 1 - step)
    tri = (trif_ref[...], trib_ref[...])
    lane = lax.broadcasted_iota(jnp.int32, (q_len, LANES), 1)
    ri = lax.broadcasted_iota(jnp.int32, (q_len, q_len), 0)
    ci = lax.broadcasted_iota(jnp.int32, (q_len, q_len), 1)
    causal = (ri >= ci, ri <= ci)
    lane_w = lax.broadcasted_iota(jnp.int32, (q_len, GROUP_W), 1)
    smask = smask_ref[...] > 0

    xs, bm, cm, dt_slot, cum_slot = [], [], [], [], []
    for d, b in chains:
        xp_ref, x_ref, xn_ref, sm_ref = ins[4 * d:4 * d + 4]
        prev_row = xp_ref[b, SUBLANES - 1:SUBLANES, :] * jnp.where(chunk[d] > 0, 1.0, 0.0)
        next_row = xn_ref[b, 0:1, :] * jnp.where(chunk[d] < nc - 1, 1.0, 0.0)
        act = _silu(_conv3(x_ref[b], prev_row, next_row, cw_ref, cb_ref[...]))
        xs.append(act[:, :GROUP_W])
        bm.append(act[:, GROUP_W:GROUP_W + LANES])
        cm.append(act[:, GROUP_W + LANES:])
        dt = _softplus(sm_ref[b] + dtb_ref[...])
        dt_slot.append(dt)
        cum_slot.append(_dot_sel_lhs(tri[d], dt * na_ref[...]))
    cum_w = [_dot_sel_rhs(cum_slot[i], ew_ref[d]) for i, (d, b) in enumerate(chains)]
    cum_2 = [_dot_sel_rhs(cum_slot[i], e2_ref[d]) for i, (d, b) in enumerate(chains)]
    dt_w = [_dot_sel_rhs(dt_slot[i], ew_ref[d]) for i, (d, b) in enumerate(chains)]
    cb = [_dot_nt(jnp.concatenate([jnp.where(lane < C_STATE, c, 0.0), jnp.where(lane >= C_STATE, c, 0.0)], axis=0), bk)
          for c, bk in zip(cm, bm)]
    s_in = [s_ref[bsz * d + b] for d, b in chains]
    y_off = [_dot(cm[i], s_in[i]) * jnp.exp(cum_w[i]) for i in range(len(chains))]
    xdt = [xs[i] * dt_w[i] for i in range(len(chains))]
    for i, (d, b) in enumerate(chains):
        edge = 0 if d else q_len - 1
        cum_edge = cum_w[i][edge:edge + 1, :]
        s_new = jnp.where(smask, _dot_tn(bm[i], xdt[i] * jnp.exp(cum_edge - cum_w[i])), 0.0)
        s_ref[bsz * d + b] = s_in[i] * jnp.exp(cum_edge) + s_new
    y = list(y_off)
    for h in range(HEADS):
        g = h // 2
        head_lanes = (lane_w >= HDIM * h) & (lane_w < HDIM * (h + 1))
        for i, (d, b) in enumerate(chains):
            a_bc = cum_2[i][:, LANES * h:LANES * (h + 1)]
            decay = jnp.exp(jnp.where(causal[d], a_bc - a_bc.T, -1e30))
            scores = cb[i][q_len * g:q_len * (g + 1), :] * decay
            yh = jnp.dot(scores.astype(BF16), xdt[i].astype(BF16), preferred_element_type=F32)
            y[i] = y[i] + jnp.where(head_lanes, yh, 0.0)
    for i, (d, b) in enumerate(chains):
        if d == 0:
            yf_ref[b] = y[i] + xs[i] * dsk_ref[...]
        else:
            yb_ref[b] = y[i]


def _ssd_scan(p3, conv_w, conv_b, dtb_slot, na_slot, dsk_w):
    bsz, seq, _ = p3.shape
    q_len = C_CHUNK
    nc = seq // q_len
    hb = q_len // SUBLANES
    nblk8 = seq // SUBLANES
    ch = (lambda s: s, lambda s: nc - 1 - s)
    trif = _np_bf16(_tri_blocks(q_len, q_len, upper=False))
    trib = _np_bf16(_tri_blocks(q_len, q_len, upper=True))
    e2 = _np_bf16(np.stack([_head_expand(SM_DT + HEADS * d, LANES) for d in range(2)]))
    ew = _np_bf16(np.stack([_head_expand(SM_DT + HEADS * d, HDIM) for d in range(2)]))
    gi = np.arange(LANES)[:, None] // C_STATE
    hi = np.arange(GROUP_W)[None, :] // HDIM
    smask = jnp.asarray((gi == hi // 2).astype(np.float32))

    def dir_specs(d):
        return [pl.BlockSpec((bsz, SUBLANES, C_XBC), lambda s: (0, jnp.maximum(ch[d](s) * hb - 1, 0), 0)),
                pl.BlockSpec((bsz, q_len, C_XBC), lambda s: (0, ch[d](s), OFF_C_XBC // C_XBC)),
                pl.BlockSpec((bsz, SUBLANES, C_XBC), lambda s: (0, jnp.minimum((ch[d](s) + 1) * hb, nblk8 - 1), 0)),
                pl.BlockSpec((bsz, q_len, LANES), lambda s: (0, ch[d](s), OFF_SMALL // LANES))]

    full = lambda a: pl.BlockSpec(a.shape, lambda s: (0,) * a.ndim)
    consts = [conv_w, conv_b, dtb_slot, na_slot, trif, trib, e2, ew, smask, dsk_w]
    out_blk = lambda d: pl.BlockSpec((bsz, q_len, GROUP_W), lambda s: (0, ch[d](s), 0))
    out_shape = jax.ShapeDtypeStruct((bsz, seq, GROUP_W), F32)
    return pl.pallas_call(
        functools.partial(_ssd_kernel, bsz=bsz, nc=nc),
        grid=(nc,),
        in_specs=dir_specs(0) + dir_specs(1) + [full(a) for a in consts],
        out_specs=[out_blk(0), out_blk(1)],
        out_shape=[out_shape, out_shape],
        scratch_shapes=[pltpu.VMEM((2 * bsz, LANES, GROUP_W), F32)],
        compiler_params=_cparams("arbitrary"),
        name="ssd_scan",
    )(*([p3] * 8), *consts)


def _delta_local_kernel(xp_ref, x_ref, xn_ref, sm_ref, cw_ref, na_ref, dtb_ref, ones_ref, trif_ref, trib_ref,
                        eg_ref, eb_ref, bd_ref, u_ref, w_ref, qk_ref, qd_ref, kd_ref, ge_ref, *, nblk, cb, parts):
    blk = pl.program_id(1)
    q_len = D_CHUNK
    rows = x_ref.shape[0]
    part_rows = rows // parts
    part_cb = cb // parts
    ones_bd = ones_ref[...]
    bd_mask = bd_ref[...] > 0
    lane = lax.broadcasted_iota(jnp.int32, (q_len, GROUP_W), 1) & (q_len - 1)
    row = lax.broadcasted_iota(jnp.int32, (q_len, GROUP_W), 0)
    eye_w = jnp.where(row == lane, 1.0, 0.0)
    incl = (row >= lane, row <= lane)
    strict = (row > lane, row < lane)
    tri = (trif_ref[...], trib_ref[...])
    rows_of = lambda c: slice(q_len * c, q_len * (c + 1))

    def l2n(t):
        ss = _dot_sel_rhs(t * t, ones_bd)
        return t * lax.rsqrt(ss + 1e-6)

    def front(g, out):
        r0, r1 = part_rows * g, part_rows * (g + 1)
        prev_row = (xp_ref[SUBLANES - 1:SUBLANES, :] * jnp.where(blk > 0, 1.0, 0.0) if g == 0
                    else x_ref[r0 - 1:r0, :])
        next_row = (xn_ref[0:1, :] * jnp.where(blk < nblk - 1, 1.0, 0.0) if g == parts - 1
                    else x_ref[r1:r1 + 1, :])
        x = x_ref[r0:r1, :]
        n = part_rows
        ri = lax.broadcasted_iota(jnp.int32, (n, GROUP_W), 0)

        def conv_act(cs):
            xc = x[:, cs]
            xm1 = jnp.where(ri == 0, prev_row[:, cs], pltpu.roll(xc, 1, axis=0))
            xp1 = jnp.where(ri == n - 1, next_row[:, cs], pltpu.roll(xc, n - 1, axis=0))
            return _silu(xm1 * cw_ref[0:1, cs] + xc * cw_ref[1:2, cs] + xp1 * cw_ref[2:3, cs])

        out["q"] = l2n(conv_act(slice(0, GROUP_W))) * (HDIM ** -0.5)
        yield
        out["k"] = l2n(conv_act(slice(GROUP_W, 2 * GROUP_W)))
        yield
        out["v"] = conv_act(slice(2 * GROUP_W, 3 * GROUP_W))
        sm = sm_ref[r0:r1, :]
        beta_slot = jax.nn.sigmoid(sm)
        g_slot = na_ref[...] * _softplus(sm + dtb_ref[...])
        yield
        out["g_w"] = [_dot_sel_rhs(g_slot, eg_ref[d]) for d in range(2)]
        out["beta_w"] = [_dot_sel_rhs(beta_slot, eb_ref[d]) for d in range(2)]
        yield

    def back(g, f):
        q_all, k_all, v_all, g_w, beta_w = f["q"], f["k"], f["v"], f["g_w"], f["beta_w"]
        units = [(d, c) for d in range(2) for c in range(part_cb)]
        seg, g_cum = [], []
        for d, c in units:
            gu = g_w[d][rows_of(c)]
            sg = _dot_sel_lhs(tri[d], jnp.concatenate([jnp.where(strict[d], gu, 0.0), gu], axis=1))
            seg.append(sg[:, :GROUP_W])
            g_cum.append(sg[:, GROUP_W:])
        yield
        kq = []
        for d, c in units:
            kc = k_all[rows_of(c)]
            lhs = jnp.concatenate([kc * beta_w[d][rows_of(c)], q_all[rows_of(c)]], axis=0)
            kq.append(_dot_nt(lhs, _block_diag(kc, bd_mask)))
        yield
        decay = [jnp.exp(jnp.where(incl[d], seg[i], -1e30)) for i, (d, c) in enumerate(units)]
        n_w = [jnp.where(strict[d], -(kq[i][:q_len] * decay[i]), 0.0) for i, (d, c) in enumerate(units)]
        p_w = [eye_w + n for n in n_w]
        m_w = [_dot(n, _block_diag(n, bd_mask)) for n in n_w]
        yield
        for _ in range(4):
            r = [_dot(jnp.concatenate([m, p], axis=0), _block_diag(m, bd_mask)) for m, p in zip(m_w, p_w)]
            m_w = [ri[:q_len] for ri in r]
            p_w = [p + ri[q_len:] for p, ri in zip(p_w, r)]
            yield
        t_w = [p + _dot(p, _block_diag(m, bd_mask)) for m, p in zip(m_w, p_w)]
        yield
        for i, (d, c) in enumerate(units):
            rs = rows_of(c)
            orow = slice(part_rows * g + q_len * c, part_rows * g + q_len * (c + 1))
            oc = part_cb * g + c
            kc, bw = k_all[rs], beta_w[d][rs]
            exp_g = jnp.exp(g_cum[i])
            edge = 0 if d else q_len - 1
            g_edge = g_cum[i][edge:edge + 1, :]
            rhs = jnp.concatenate([_block_diag(v_all[rs] * bw, bd_mask), _block_diag(kc * bw * exp_g, bd_mask)],
                                  axis=1)
            uw = _dot(t_w[i], rhs)
            u_ref[d, orow, :] = uw[:, :GROUP_W]
            w_ref[d, orow, :] = uw[:, GROUP_W:].astype(BF16)
            qk_ref[d, orow, :] = (kq[i][q_len:] * decay[i]).astype(BF16)
            qd_ref[d, orow, :] = (q_all[rs] * exp_g).astype(BF16)
            kd_ref[d, orow, :] = (kc * jnp.exp(g_edge - g_cum[i])).astype(BF16)
            ge_ref[d, SUBLANES * oc:SUBLANES * (oc + 1), :] = jnp.broadcast_to(jnp.exp(g_edge), (SUBLANES, GROUP_W))
        yield

    fronts = [dict() for _ in range(parts)]
    for _ in front(0, fronts[0]):
        pass
    for g in range(parts):
        nxt = front(g + 1, fronts[g + 1]) if g + 1 < parts else iter(())
        for _ in back(g, fronts[g]):
            next(nxt, None)
        for _ in nxt:
            pass


def _delta_local(p3, conv_w, na_slot, dtb_slot):
    bsz, seq, _ = p3.shape
    q_len = D_CHUNK
    rows = min(512, seq)
    cb = rows // q_len
    parts = 2
    nblk = seq // rows
    hb = rows // SUBLANES
    nblk8 = seq // SUBLANES
    ones_bd = _np_bf16(np.kron(np.eye(HEADS), np.ones((HDIM, HDIM))))
    trif = _np_bf16(_tri_blocks(q_len, q_len, upper=False))
    trib = _np_bf16(_tri_blocks(q_len, q_len, upper=True))
    eg = _np_bf16(np.stack([_head_expand(SM_AB + 8 + HEADS * d, HDIM) for d in range(2)]))
    eb = _np_bf16(np.stack([_head_expand(SM_AB + HEADS * d, HDIM) for d in range(2)]))
    bd = jnp.asarray(np.kron(np.eye(HEADS), np.ones((q_len, HDIM))).astype(np.float32))
    x_spec = pl.BlockSpec((None, rows, D_QKV), lambda b, i: (b, i, OFF_D_QKV // D_QKV))
    xp_spec = pl.BlockSpec((None, SUBLANES, D_QKV), lambda b, i: (b, jnp.maximum(i * hb - 1, 0), OFF_D_QKV // D_QKV))
    xn_spec = pl.BlockSpec((None, SUBLANES, D_QKV), lambda b, i: (b, jnp.minimum((i + 1) * hb, nblk8 - 1), OFF_D_QKV // D_QKV))
    sm_spec = pl.BlockSpec((None, rows, LANES), lambda b, i: (b, i, OFF_SMALL // LANES))
    full = lambda a: pl.BlockSpec(a.shape, lambda b, i: (0,) * a.ndim)
    consts = [conv_w, na_slot, dtb_slot, ones_bd, trif, trib, eg, eb, bd]
    out_blk = pl.BlockSpec((None, 2, rows, GROUP_W), lambda b, i: (b, 0, i, 0))
    ge_blk = pl.BlockSpec((None, 2, SUBLANES * cb, GROUP_W), lambda b, i: (b, 0, i, 0))
    shp = lambda dt: jax.ShapeDtypeStruct((bsz, 2, seq, GROUP_W), dt)
    return pl.pallas_call(
        functools.partial(_delta_local_kernel, nblk=nblk, cb=cb, parts=parts),
        grid=(bsz, nblk),
        in_specs=[xp_spec, x_spec, xn_spec, sm_spec] + [full(a) for a in consts],
        out_specs=[out_blk] * 5 + [ge_blk],
        out_shape=[shp(F32), shp(BF16), shp(BF16), shp(BF16), shp(BF16),
                   jax.ShapeDtypeStruct((bsz, 2, seq // q_len * SUBLANES, GROUP_W), F32)],
        compiler_params=_cparams("parallel", "parallel"),
        name="delta_local",
    )(p3, p3, p3, p3, *consts)


def _delta_scan_kernel(*refs, bsz):
    ins, (bd_ref, of_ref, ob_ref, s_ref) = refs[:12], refs[12:]
    o_refs = (of_ref, ob_ref)

    @pl.when(pl.program_id(0) == 0)
    def _():
        s_ref[...] = jnp.zeros_like(s_ref)

    q_len = D_CHUNK
    bd_mask = bd_ref[...] > 0
    chains = [(d, b) for d in range(2) for b in range(bsz)]
    get = lambda k, d, b: ins[6 * d + k][b]
    s = [s_ref[bsz * d + b] for d, b in chains]
    r = [jnp.dot(jnp.concatenate([get(1, d, b), get(3, d, b)], axis=0), s[i].astype(BF16),
                 preferred_element_type=F32) for i, (d, b) in enumerate(chains)]
    v_new = [get(0, d, b) - r[i][:q_len] for i, (d, b) in enumerate(chains)]
    upd = [lax.dot_general(get(4, d, b), v_new[i].astype(BF16), (((0,), (0,)), ((), ())),
                           preferred_element_type=F32) for i, (d, b) in enumerate(chains)]
    for i, (d, b) in enumerate(chains):
        s_ref[bsz * d + b] = s[i] * get(5, d, b)[0:1, :] + jnp.where(bd_mask, upd[i], 0.0)
    for i, (d, b) in enumerate(chains):
        o_refs[d][b] = r[i][q_len:] + jnp.dot(get(2, d, b), _block_diag(v_new[i], bd_mask),
                                              preferred_element_type=F32)


def _delta_scan(u, w, qk, qd, kd, ge):
    bsz, _, seq, _ = u.shape
    q_len = D_CHUNK
    nc = seq // q_len
    bd = jnp.asarray(np.kron(np.eye(HEADS), np.ones((q_len, HDIM))).astype(np.float32))
    ch = (lambda c: c, lambda c: nc - 1 - c)
    blk = lambda d, rows: pl.BlockSpec((bsz, None, rows, GROUP_W), lambda c: (0, d, ch[d](c), 0))
    in_specs = [blk(d, rows) for d in range(2) for rows in (q_len,) * 5 + (SUBLANES,)]
    out_blk = lambda d: pl.BlockSpec((bsz, q_len, GROUP_W), lambda c: (0, ch[d](c), 0))
    out_shape = jax.ShapeDtypeStruct((bsz, seq, GROUP_W), F32)
    return pl.pallas_call(
        functools.partial(_delta_scan_kernel, bsz=bsz),
        grid=(nc,),
        in_specs=in_specs + [pl.BlockSpec(bd.shape, lambda c: (0, 0))],
        out_specs=[out_blk(0), out_blk(1)],
        out_shape=[out_shape, out_shape],
        scratch_shapes=[pltpu.VMEM((2 * bsz, GROUP_W, GROUP_W), F32)],
        compiler_params=_cparams("arbitrary"),
        name="delta_scan",
    )(*([u, w, qk, qd, kd, ge] * 2), bd)


def _out_proj_kernel(oa_ref, ob_ref, ycf_ref, ycb_ref, zc_ref, odf_ref, odb_ref, zd_ref, x_ref,
                     wa_ref, wb_ref, wc_ref, wd_ref, ga_ref, gb_ref, gc_ref, gd_ref, ones_ref, gp_ref, o_ref):
    oa = (_rms(oa_ref[...].astype(F32), GROUP_W) * ga_ref[...]).astype(BF16)
    ob = (_rms(ob_ref[...].astype(F32), GROUP_W) * gb_ref[...]).astype(BF16)
    oc = (_rms((ycf_ref[...] + ycb_ref[...]) * _silu(zc_ref[...]), GROUP_W) * gc_ref[...]).astype(BF16)
    od = odf_ref[...] + odb_ref[...]
    ms = _dot_sel_rhs(od * od, ones_ref[...]) * (1.0 / HDIM)
    od = od * lax.rsqrt(ms + EPS) * gd_ref[...] * _silu(zd_ref[...])
    acc = jnp.dot(oa, wa_ref[...], preferred_element_type=F32)
    acc += jnp.dot(ob, wb_ref[...], preferred_element_type=F32)
    acc += jnp.dot(oc, wc_ref[...], preferred_element_type=F32)
    acc += jnp.dot(od.astype(BF16), wd_ref[...], preferred_element_type=F32)
    o_ref[...] = x_ref[...] + _rms(acc, D_MODEL) * gp_ref[...]


def _out_proj(oa, ob, ycf, ycb, odf, odb, p3, x3, wa, wb, wc, wd, ga, gb, gc, gd, gpost):
    bsz, seq, _ = x3.shape
    tm = min(512, seq)
    ones_bd = _np_bf16(np.kron(np.eye(HEADS), np.ones((HDIM, HDIM))))
    rows = lambda w: pl.BlockSpec((None, tm, w), lambda b, i: (b, i, 0))
    full = lambda a: pl.BlockSpec(a.shape, lambda b, i: (0,) * a.ndim)
    z_spec = lambda off: pl.BlockSpec((None, tm, GROUP_W), lambda b, i: (b, i, off // GROUP_W))
    consts = [wa, wb, wc, wd, ga, gb, gc, gd, ones_bd, gpost]
    return pl.pallas_call(
        _out_proj_kernel,
        grid=(bsz, seq // tm),
        in_specs=[rows(HEADS * LANES), rows(HEADS * LANES), rows(GROUP_W), rows(GROUP_W), z_spec(OFF_C_Z),
                  rows(GROUP_W), rows(GROUP_W), z_spec(OFF_D_Z), rows(D_MODEL)] + [full(a) for a in consts],
        out_specs=rows(D_MODEL),
        out_shape=jax.ShapeDtypeStruct(x3.shape, F32),
        compiler_params=_cparams("parallel", "parallel"),
        name="out_proj",
    )(oa, ob, ycf, ycb, p3, odf, odb, p3, x3, *consts)


def _ffn_kernel(xp_ref, x_ref, xn_ref, gpre_ref, win_ref, cw_ref, cb_ref, wout_ref, gpost_ref, o_ref,
                gate_ref, up_ref, act_ref, *, tiles_per_seq, nj):
    tm = x_ref.shape[0]
    halo = BF16_ROWS
    pos = lax.rem(pl.program_id(0), tiles_per_seq)
    norm = lambda t: _rms(t, D_MODEL) * gpre_ref[...]
    h = jnp.concatenate([(norm(xp_ref[...]) * jnp.where(pos == 0, 0.0, 1.0)).astype(BF16),
                         norm(x_ref[...]).astype(BF16),
                         (norm(xn_ref[...]) * jnp.where(pos == tiles_per_seq - 1, 0.0, 1.0)).astype(BF16)], axis=0)
    n = tm + 2 * halo
    body = slice(halo, halo + tm)

    def project(j):
        for ref, off in ((gate_ref, 0), (up_ref, D_FF)):
            ref[j % 2] = jnp.dot(h, win_ref[:, off + FF_TILE * j:off + FF_TILE * (j + 1)],
                                 preferred_element_type=F32)

    def conv(ref, off, j):
        t = ref[j % 2]
        cs = slice(off + FF_TILE * j, off + FF_TILE * (j + 1))
        return (pltpu.roll(t, 1, axis=0)[body] * cw_ref[0:1, cs] + t[body] * cw_ref[1:2, cs]
                + pltpu.roll(t, n - 1, axis=0)[body] * cw_ref[2:3, cs] + cb_ref[:, cs])

    project(0)
    for j in range(nj):
        if j + 1 < nj:
            project(j + 1)
        act_ref[:, FF_TILE * j:FF_TILE * (j + 1)] = (
            _silu(conv(gate_ref, 0, j)) * conv(up_ref, D_FF, j)).astype(BF16)
    acc = jnp.dot(act_ref[...], wout_ref[...], preferred_element_type=F32)
    o_ref[...] = x_ref[...] + _rms(acc, D_MODEL) * gpost_ref[...]


def _ffn(x2d, seq, layer, gpre, w_in, conv_w, conv_b, w_out, gpost):
    m = x2d.shape[0]
    tm = min(512, seq)
    nj = D_FF // FF_TILE
    hb = tm // BF16_ROWS
    nblk = m // BF16_ROWS
    kern = functools.partial(_ffn_kernel, tiles_per_seq=seq // tm, nj=nj)
    const = lambda a: pl.BlockSpec((None,) + a.shape[1:], lambda i: (layer, 0, 0), pipeline_mode=pl.Buffered(1))
    return pl.pallas_call(
        kern,
        grid=(m // tm,),
        in_specs=[pl.BlockSpec((BF16_ROWS, D_MODEL), lambda i: (jnp.maximum(i * hb - 1, 0), 0)),
                  pl.BlockSpec((tm, D_MODEL), lambda i: (i, 0)),
                  pl.BlockSpec((BF16_ROWS, D_MODEL), lambda i: (jnp.minimum((i + 1) * hb, nblk - 1), 0)),
                  const(gpre), const(w_in), const(conv_w), const(conv_b), const(w_out), const(gpost)],
        out_specs=pl.BlockSpec((tm, D_MODEL), lambda i: (i, 0)),
        out_shape=jax.ShapeDtypeStruct(x2d.shape, F32),
        scratch_shapes=[pltpu.VMEM((2, tm + 2 * BF16_ROWS, FF_TILE), F32),
                        pltpu.VMEM((2, tm + 2 * BF16_ROWS, FF_TILE), F32),
                        pltpu.VMEM((tm, D_FF), BF16)],
        compiler_params=_cparams("parallel"),
        name="conv_ffn",
    )(x2d, x2d, x2d, gpre, w_in, conv_w, conv_b, w_out, gpost)


def _slot_vec(pairs):
    v = jnp.zeros((1, LANES), F32)
    for off, vals in pairs:
        v = v.at[0, off:off + vals.shape[0]].set(vals.astype(F32))
    return v


def _prep_layer(prm):
    wuq = prm["a_w_uq"].reshape(A_Q_LORA, HEADS, A_NOPE + A_ROPE)
    rope_w = wuq[..., A_NOPE:]
    zpad = jnp.zeros((A_Q_LORA, HEADS, LANES - A_NOPE - A_ROPE), F32)
    w1 = jnp.concatenate([wuq[..., :A_NOPE], rope_w, zpad], axis=-1).reshape(A_Q_LORA, HEADS * LANES)
    w2 = jnp.concatenate([jnp.zeros_like(wuq[..., :A_NOPE]), _rot_last(rope_w), zpad], axis=-1).reshape(A_Q_LORA, HEADS * LANES)
    rpad = ((0, 256 - A_Q_LORA), (0, 0))
    wukv = prm["a_w_ukv"].reshape(A_KV_LORA, HEADS, A_NOPE + HDIM)
    hpad = jnp.zeros((A_KV_LORA, HEADS, LANES - HDIM), F32)
    c_b = (HDIM ** -0.5) * LOG2E
    pad64 = lambda g: jnp.pad(g, (0, LANES - HDIM)).reshape(1, LANES)
    wo = prm["w_out"]
    return dict(
        w_in=_arrange_w_in(prm["w_in"]),
        pre_mix_norm=prm["pre_mix_norm"],
        mla=(jnp.pad(prm["a_q_norm"], (0, 256 - A_Q_LORA)).reshape(1, 256), prm["a_kv_norm"].reshape(1, A_KV_LORA),
             jnp.pad(w1, rpad).astype(BF16), jnp.pad(w2, rpad).astype(BF16),
             jnp.concatenate([wukv[..., :A_NOPE], hpad], axis=-1).reshape(A_KV_LORA, HEADS * LANES).astype(BF16),
             jnp.concatenate([wukv[..., A_NOPE:], hpad], axis=-1).reshape(A_KV_LORA, HEADS * LANES).astype(BF16)),
        gqa=(pad64(prm["b_q_norm"] * c_b), pad64(_swap_last(prm["b_q_norm"]) * c_b),
             pad64(prm["b_k_norm"]), pad64(_swap_last(prm["b_k_norm"]))),
        ssd=(prm["c_conv_w"], prm["c_conv_b"].reshape(1, -1), _slot_vec([(SM_DT, prm["c_dt_bias"].reshape(-1))]),
             _slot_vec([(SM_DT, -jnp.exp(prm["c_a_log"].astype(F32)).reshape(-1))]),
             jnp.repeat(prm["c_d_skip"].astype(F32), HDIM).reshape(1, GROUP_W)),
        delta=(prm["d_conv_w"], _slot_vec([(SM_AB + 8, -jnp.exp(prm["d_a_log"].astype(F32)).reshape(-1))]),
               _slot_vec([(SM_AB + 8, prm["d_dt_bias"].reshape(-1))])),
        out=(_pad_heads(wo[0:256], 0, HEADS, HDIM, LANES).astype(BF16),
             _pad_heads(wo[256:512], 0, HEADS, HDIM, LANES).astype(BF16),
             wo[512:768].astype(BF16), wo[768:1024].astype(BF16),
             _pad_heads(prm["a_out_norm"], 0, HEADS, HDIM, LANES).reshape(1, -1),
             _pad_heads(prm["b_out_norm"], 0, HEADS, HDIM, LANES).reshape(1, -1),
             prm["c_out_norm"].reshape(1, -1), jnp.tile(prm["d_out_norm"], HEADS).reshape(1, -1),
             prm["post_mix_norm"].reshape(1, -1)),
    )


def _layer(x3, tabs, w, layer, ffn_prm):
    bsz, seq, _ = x3.shape
    p2d, qa, ka, va, qb, kb, vb = _in_proj(x3, w["pre_mix_norm"], w["w_in"], tabs, w["mla"], w["gqa"])
    p3 = p2d.reshape(bsz, seq, P_HBM_COLS)
    oa = _attention(qa, ka, va)
    ob = _attention(qb, kb, vb)
    ycf, ycb = _ssd_scan(p3, *w["ssd"])
    odf, odb = _delta_scan(*_delta_local(p3, *w["delta"]))
    x3 = _out_proj(oa, ob, ycf, ycb, odf, odb, p3, x3, *w["out"])
    x2d = _ffn(x3.reshape(bsz * seq, D_MODEL), seq, layer, *ffn_prm)
    return x2d.reshape(bsz, seq, D_MODEL)


def _attn_tables(seq):
    cos_a, sin_a = _rope_tables(seq, A_ROPE)
    cos_b, sin_b = _rope_tables(seq, HDIM)
    pada = lambda t: jnp.pad(t, ((0, 0), (A_NOPE, LANES - A_NOPE - A_ROPE)))
    padb = lambda t: jnp.pad(t, ((0, 0), (0, LANES - HDIM)))
    return (pada(cos_a), pada(sin_a)), (padb(cos_b), padb(sin_b))


def kernel(x, pre_mix_norm, w_in, a_q_norm, a_w_uq, a_kv_norm, a_w_ukv, a_out_norm, b_q_norm, b_k_norm, b_out_norm, c_conv_w, c_conv_b, c_a_log, c_dt_bias, c_d_skip, c_out_norm, d_conv_w, d_a_log, d_dt_bias, d_out_norm, w_out, post_mix_norm, pre_ffn_norm, f_w_in, f_conv_w, f_conv_b, f_w_out, post_ffn_norm):
    params = dict(pre_mix_norm=pre_mix_norm, w_in=w_in, a_q_norm=a_q_norm, a_w_uq=a_w_uq, a_kv_norm=a_kv_norm,
                  a_w_ukv=a_w_ukv, a_out_norm=a_out_norm, b_q_norm=b_q_norm, b_k_norm=b_k_norm,
                  b_out_norm=b_out_norm, c_conv_w=c_conv_w, c_conv_b=c_conv_b, c_a_log=c_a_log,
                  c_dt_bias=c_dt_bias, c_d_skip=c_d_skip, c_out_norm=c_out_norm, d_conv_w=d_conv_w,
                  d_a_log=d_a_log, d_dt_bias=d_dt_bias, d_out_norm=d_out_norm, w_out=w_out,
                  post_mix_norm=post_mix_norm)
    ffn_prm = (pre_ffn_norm[:, None, :], f_w_in.astype(BF16), f_conv_w, f_conv_b[:, None, :],
               f_w_out.astype(BF16), post_ffn_norm[:, None, :])
    tabs_a, tabs_b = _attn_tables(x.shape[1])
    prepared = jax.vmap(_prep_layer)(params)
    for layer in range(w_in.shape[0]):
        w = jax.tree_util.tree_map(lambda a: a[layer], prepared)
        x = _layer(x, tabs_a + tabs_b, w, layer, ffn_prm)
    return x
```

```python
import functools
import math

import numpy as np
import jax
import jax.numpy as jnp
from jax import lax
from jax.experimental import pallas as pl
from jax.experimental.pallas import tpu as pltpu

F32 = jnp.float32
BF16 = jnp.bfloat16

LANES = 128
SUBLANES = 8
BF16_ROWS = 16
VMEM_LIMIT = 56 * 1024 * 1024

EPS = 1e-6
ROPE_BASE = 10000.0
GRID_W = 64
D_MODEL = 1024
GROUP_W = 256
HEADS = 4
HDIM = 64

A_NOPE, A_ROPE, A_Q_LORA, A_KV_LORA = 64, 32, 192, 128
B_KV_HEADS = 2
C_STATE, C_CHUNK, C_XBC = 64, 128, 512
D_CHUNK, D_QKV = 64, 768
D_FF = 2816
FF_TILE = 256

OFF_C_XBC, OFF_D_Z, OFF_D_QKV, OFF_C_Z, OFF_SMALL = 0, 512, 768, 1536, 1792
P_HBM_COLS = 1920
OFF_A_CQ, OFF_B_Q, OFF_B_QROT = 1920, 2176, 2432
OFF_A_CKV, OFF_B_K, OFF_B_KROT, OFF_B_V = 2688, 2816, 2944, 3072
IN_COLS_PADDED = 3200
SM_DT, SM_AB, SM_KR, SM_KRROT = 0, 16, 64, 96

LOG2E = math.log2(math.e)


def _cparams(*sem):
    return pltpu.CompilerParams(dimension_semantics=sem, vmem_limit_bytes=VMEM_LIMIT)


def _dot(a, b):
    return jnp.dot(a.astype(BF16), b.astype(BF16), preferred_element_type=F32)


def _dot_nt(a, b):
    return lax.dot_general(a.astype(BF16), b.astype(BF16), (((1,), (1,)), ((), ())),
                           preferred_element_type=F32)


def _dot_tn(a, b):
    return lax.dot_general(a.astype(BF16), b.astype(BF16), (((0,), (0,)), ((), ())),
                           preferred_element_type=F32)


def _split3(x):
    hi = x.astype(BF16)
    r1 = x - hi.astype(F32)
    mid = r1.astype(BF16)
    lo = (r1 - mid.astype(F32)).astype(BF16)
    return hi, mid, lo


def _dot_sel_rhs(x, sel):
    hi, mid, lo = _split3(x)
    d = lambda p: jnp.dot(p, sel, preferred_element_type=F32)
    return d(hi) + d(mid) + d(lo)


def _dot_sel_lhs(sel, x):
    hi, mid, lo = _split3(x)
    d = lambda p: jnp.dot(sel, p, preferred_element_type=F32)
    return d(hi) + d(mid) + d(lo)


def _softplus(x):
    return jnp.maximum(x, 0.0) + jnp.log1p(jnp.exp(-jnp.abs(x)))


def _silu(x):
    return x * jax.nn.sigmoid(x)


def _rms(x, n):
    return x * lax.rsqrt(jnp.sum(x * x, axis=-1, keepdims=True) * (1.0 / n) + EPS)


def _conv3(x, prev_row, next_row, w_ref, bias):
    n = x.shape[0]
    row = lax.broadcasted_iota(jnp.int32, x.shape, 0)
    xm1 = jnp.where(row == 0, prev_row, pltpu.roll(x, 1, axis=0))
    xp1 = jnp.where(row == n - 1, next_row, pltpu.roll(x, n - 1, axis=0))
    y = xm1 * w_ref[0:1, :] + x * w_ref[1:2, :] + xp1 * w_ref[2:3, :]
    return y if bias is None else y + bias


def _block_diag(x, mask):
    return jnp.where(mask, jnp.concatenate([x] * HEADS, axis=0), 0.0).astype(BF16)


def _np_bf16(a):
    return jnp.asarray(np.asarray(a, np.float32), BF16)


def _head_expand(first_lane, width):
    e = np.zeros((LANES, HEADS * width), np.float32)
    for h in range(HEADS):
        e[first_lane + h, h * width:(h + 1) * width] = 1.0
    return e


def _tri_blocks(n, blk, upper):
    i = np.arange(n)
    same = (i[:, None] // blk) == (i[None, :] // blk)
    tri = (i[:, None] <= i[None, :]) if upper else (i[:, None] >= i[None, :])
    return (same & tri).astype(np.float32)


def _rot_last(w):
    r = w.shape[-1]
    xs = w.reshape(w.shape[:-1] + (2, 2, r // 4))
    return jnp.stack([-xs[..., 1, :], xs[..., 0, :]], axis=-2).reshape(w.shape)


def _swap_last(w):
    r = w.shape[-1]
    xs = w.reshape(w.shape[:-1] + (2, 2, r // 4))
    return jnp.stack([xs[..., 1, :], xs[..., 0, :]], axis=-2).reshape(w.shape)


def _rope_tables(seq_len, rot_dim):
    rows = seq_len // GRID_W
    sec = rot_dim // 2
    inv_freq = ROPE_BASE ** (-jnp.arange(0, sec, 2, dtype=F32) / sec)
    ang_r = jnp.arange(rows).astype(F32)[:, None] * inv_freq
    ang_c = jnp.arange(GRID_W).astype(F32)[:, None] * inv_freq

    def table(fn):
        t_r = jnp.repeat(fn(ang_r), GRID_W, axis=0)
        t_c = jnp.tile(fn(ang_c), (rows, 1))
        return jnp.concatenate([t_r, t_r, t_c, t_c], axis=-1)

    return table(jnp.cos), table(jnp.sin)


def _pad_heads(a, axis, n_heads, real, padded):
    shp = list(a.shape)
    a = a.reshape(shp[:axis] + [n_heads, real] + shp[axis + 1:])
    pad = [(0, 0)] * a.ndim
    pad[axis + 1] = (0, padded - real)
    a = jnp.pad(a, pad)
    return a.reshape(shp[:axis] + [n_heads * padded] + shp[axis + 1:])


def _arrange_w_in(w):
    a0, b0, c0, d0 = 0, 352, 864, 1640
    lead = w.shape[:-1]
    zeros = lambda n: jnp.zeros(lead + (n,), w.dtype)
    cols_of = lambda lo, hi: w[..., lo:hi]
    cq, ckv, kr = cols_of(a0, a0 + 192), cols_of(a0 + 192, a0 + 320), cols_of(a0 + 320, a0 + 352)
    bq, bk, bv = cols_of(b0, b0 + 256), cols_of(b0 + 256, b0 + 384), cols_of(b0 + 384, b0 + 512)
    cz, cxbc, cdt = cols_of(c0, c0 + 256), cols_of(c0 + 256, c0 + 768), cols_of(c0 + 768, c0 + 776)
    dqkv, dz, dab = cols_of(d0, d0 + 768), cols_of(d0 + 768, d0 + 1024), cols_of(d0 + 1024, d0 + 1040)
    bq_rot = _rot_last(bq.reshape(lead + (HEADS, HDIM))).reshape(lead + (256,))
    bk_rot = _rot_last(bk.reshape(lead + (B_KV_HEADS, HDIM))).reshape(lead + (128,))
    small = jnp.concatenate([cdt, zeros(8), dab, zeros(32), kr, _rot_last(kr)], axis=-1)
    cols = [cxbc, dz, dqkv, cz, small, cq, zeros(64), bq, bq_rot, ckv, bk, bk_rot, bv]
    out = jnp.concatenate(cols, axis=-1)
    assert out.shape[-1] == IN_COLS_PADDED
    return out.astype(BF16)


def _mla_heads(cq, ckv, sm, ckt, skt, gq_ref, gkv_ref, w1_ref, w2_ref, wk_ref, wv_ref, q_ref, k_ref, v_ref, c_a):
    cqn = (_rms(cq, A_Q_LORA) * gq_ref[...]).astype(BF16)
    q1 = jnp.dot(cqn, w1_ref[...], preferred_element_type=F32)
    q2 = jnp.dot(cqn, w2_ref[...], preferred_element_type=F32)
    kvn = (_rms(ckv, A_KV_LORA) * gkv_ref[...]).astype(BF16)
    k1 = jnp.dot(kvn, wk_ref[...], preferred_element_type=F32)
    v1 = jnp.dot(kvn, wv_ref[...], preferred_element_type=F32)
    k_rope = sm * ckt + pltpu.roll(sm, LANES - A_ROPE, axis=1) * skt
    lane = lax.broadcasted_iota(jnp.int32, sm.shape, 1)
    cqt = jnp.where(lane < A_NOPE, c_a, c_a * ckt)
    sqt = c_a * skt
    ones_lane = lane == HDIM
    for h in range(HEADS):
        sl = slice(LANES * h, LANES * (h + 1))
        q_ref[h] = (q1[:, sl] * cqt + q2[:, sl] * sqt).astype(BF16)
        k_ref[h] = (k1[:, sl] + k_rope).astype(BF16)
        v_ref[h] = jnp.where(ones_lane, 1.0, v1[:, sl]).astype(BF16)


def _head_slot(x, h, lo):
    grp = x[:, LANES * (h // 2):LANES * (h // 2 + 1)]
    if h % 2:
        grp = pltpu.roll(grp, HDIM, axis=1)
    return jnp.where(lo, grp, 0.0)


def _gqa_heads(q, qr, k, kr, v, cos, sin, gq_ref, gqs_ref, gk_ref, gks_ref, qo_ref, ko_ref, vo_ref):
    lane = lax.broadcasted_iota(jnp.int32, cos.shape, 1)
    lo = lane < HDIM
    ones_lane = lane == HDIM

    def normed_rope(x, xr, h, g, gs):
        xh, xrh = _head_slot(x, h, lo), _head_slot(xr, h, lo)
        r = lax.rsqrt(jnp.sum(xh * xh, axis=-1, keepdims=True) * (1.0 / HDIM) + EPS)
        return (r * (xh * (cos * g) + xrh * (sin * gs))).astype(BF16)

    for h in range(HEADS):
        qo_ref[h] = normed_rope(q, qr, h, gq_ref[...], gqs_ref[...])
    for h in range(B_KV_HEADS):
        ko_ref[h] = normed_rope(k, kr, h, gk_ref[...], gks_ref[...])
        vo_ref[h] = jnp.where(ones_lane, 1.0, _head_slot(v, h, lo)).astype(BF16)


def _in_proj_kernel(x_ref, g_ref, w_ref, ckt_ref, skt_ref, cosb_ref, sinb_ref, gq_ref, gkv_ref, w1_ref, w2_ref,
                    wk_ref, wv_ref, gbq_ref, gbqs_ref, gbk_ref, gbks_ref,
                    p_ref, qa_ref, ka_ref, va_ref, qb_ref, kb_ref, vb_ref, *, c_a):
    h = (_rms(x_ref[...], D_MODEL) * g_ref[...]).astype(BF16)
    p_att = jnp.dot(h, w_ref[:, P_HBM_COLS:], preferred_element_type=F32)
    p_hbm = jnp.dot(h, w_ref[:, :P_HBM_COLS], preferred_element_type=F32)
    p_ref[...] = p_hbm

    def col(off, w):
        src, base = (p_hbm, 0) if off < P_HBM_COLS else (p_att, P_HBM_COLS)
        return src[:, off - base:off - base + w]

    _mla_heads(col(OFF_A_CQ, 256), col(OFF_A_CKV, LANES), col(OFF_SMALL, LANES), ckt_ref[...], skt_ref[...],
               gq_ref, gkv_ref, w1_ref, w2_ref, wk_ref, wv_ref, qa_ref, ka_ref, va_ref, c_a)
    _gqa_heads(col(OFF_B_Q, 256), col(OFF_B_QROT, 256), col(OFF_B_K, LANES), col(OFF_B_KROT, LANES),
               col(OFF_B_V, LANES), cosb_ref[...], sinb_ref[...], gbq_ref, gbqs_ref, gbk_ref, gbks_ref,
               qb_ref, kb_ref, vb_ref)


def _in_proj(x3, gain, w, tabs, mla_prm, gqa_prm):
    bsz, seq, _ = x3.shape
    m = bsz * seq
    tm = min(512, seq)
    tps = seq // tm
    c_a = ((A_NOPE + A_ROPE) ** -0.5) * LOG2E
    consts = list(mla_prm) + list(gqa_prm)
    tab = pl.BlockSpec((tm, LANES), lambda i: (i % tps, 0))
    full = lambda a: pl.BlockSpec(a.shape, lambda i: (0,) * a.ndim)
    heads = lambda n: pl.BlockSpec((None, n, tm, LANES), lambda i: (i // tps, 0, i % tps, 0))
    shp = lambda n: jax.ShapeDtypeStruct((bsz, n, seq, LANES), BF16)
    return pl.pallas_call(
        functools.partial(_in_proj_kernel, c_a=c_a),
        grid=(m // tm,),
        in_specs=[pl.BlockSpec((tm, D_MODEL), lambda i: (i, 0)),
                  pl.BlockSpec((1, D_MODEL), lambda i: (0, 0)),
                  pl.BlockSpec((D_MODEL, IN_COLS_PADDED), lambda i: (0, 0), pipeline_mode=pl.Buffered(1)),
                  tab, tab, tab, tab] + [full(a) for a in consts],
        out_specs=[pl.BlockSpec((tm, P_HBM_COLS), lambda i: (i, 0)),
                   heads(HEADS), heads(HEADS), heads(HEADS), heads(HEADS), heads(B_KV_HEADS), heads(B_KV_HEADS)],
        out_shape=[jax.ShapeDtypeStruct((m, P_HBM_COLS), F32),
                   shp(HEADS), shp(HEADS), shp(HEADS), shp(HEADS), shp(B_KV_HEADS), shp(B_KV_HEADS)],
        compiler_params=_cparams("parallel"),
        name="in_proj",
    )(x3.reshape(m, D_MODEL), gain.reshape(1, -1), w, *tabs, *consts)


def _attn_kernel(q_ref, k_ref, v_ref, o_ref, s0_ref, s1_ref, p0_ref, p1_ref, acc_ref, *, tk, nk):
    q = q_ref[...]
    tq = q.shape[0]

    s_refs, p_refs = (s0_ref, s1_ref), (p0_ref, p1_ref)

    def scores(c):
        s_refs[c % 2][...] = lax.dot_general(q, k_ref[tk * c:tk * (c + 1), :], (((1,), (1,)), ((), ())),
                                             preferred_element_type=F32)

    def probs(c, m):
        s = s_refs[c % 2][...]
        m_new = jnp.maximum(m, jnp.max(s, axis=-1, keepdims=True))
        p_refs[c % 2][...] = jnp.exp2(s - m_new).astype(BF16)
        return jnp.exp2(m - m_new), m_new

    def accumulate(c, alpha):
        pv = jnp.dot(p_refs[c % 2][...], v_ref[tk * c:tk * (c + 1), :], preferred_element_type=F32)
        acc_ref[...] = pv if c == 0 else alpha * acc_ref[...] + pv

    m = jnp.full((tq, 1), -1e30, F32)
    alphas = {}
    scores(0)
    for c in range(nk):
        if c + 1 < nk:
            scores(c + 1)
        alphas[c], m = probs(c, m)
        if c >= 1:
            accumulate(c - 1, alphas[c - 1])
    accumulate(nk - 1, alphas[nk - 1])
    acc = acc_ref[...]
    lane = lax.broadcasted_iota(jnp.int32, acc.shape, 1)
    o_ref[...] = jnp.where(lane < HDIM, acc / acc[:, HDIM:HDIM + 1], 0.0).astype(BF16)


def _attention(q, k, v):
    bsz, nh, seq, _ = q.shape
    rep = nh // k.shape[1]
    tq = min(1024, seq)
    tk = min(512, seq // 2)
    nk = seq // tk
    kern = functools.partial(_attn_kernel, tk=tk, nk=nk)
    kv_spec = pl.BlockSpec((None, None, seq, LANES), lambda b, h, i: (b, h // rep, 0, 0))
    return pl.pallas_call(
        kern,
        grid=(bsz, nh, seq // tq),
        in_specs=[pl.BlockSpec((None, None, tq, LANES), lambda b, h, i: (b, h, i, 0)), kv_spec, kv_spec],
        out_specs=pl.BlockSpec((None, tq, LANES), lambda b, h, i: (b, i, h)),
        out_shape=jax.ShapeDtypeStruct((bsz, seq, nh * LANES), BF16),
        scratch_shapes=[pltpu.VMEM((tq, tk), F32), pltpu.VMEM((tq, tk), F32),
                        pltpu.VMEM((tq, tk), BF16), pltpu.VMEM((tq, tk), BF16),
                        pltpu.VMEM((tq, LANES), F32)],
        compiler_params=_cparams("parallel", "parallel", "arbitrary"),
        name="attention",
    )(q, k, v)


def _ssd_kernel(*refs, bsz, nc):
    ins = refs[:8]
    (cw_ref, cb_ref, dtb_ref, na_ref, trif_ref, trib_ref, e2_ref, ew_ref, smask_ref, dsk_ref,
     yf_ref, yb_ref, s_ref) = refs[8:]
    step = pl.program_id(0)
    q_len = C_CHUNK

    @pl.when(step == 0)
    def _():
        s_ref[...] = jnp.zeros_like(s_ref)

    chains = [(d, b) for d in range(2) for b in range(bsz)]
    chunk = (step, nc - 1 - step)
    tri = (trif_ref[...], trib_ref[...])
    lane = lax.broadcasted_iota(jnp.int32, (q_len, LANES), 1)
    ri = lax.broadcasted_iota(jnp.int32, (q_len, q_len), 0)
    ci = lax.broadcasted_iota(jnp.int32, (q_len, q_len), 1)
    causal = (ri >= ci, ri <= ci)
    lane_w = lax.broadcasted_iota(jnp.int32, (q_len, GROUP_W), 1)
    smask = smask_ref[...] > 0

    xs, bm, cm, dt_slot, cum_slot = [], [], [], [], []
    for d, b in chains:
        xp_ref, x_ref, xn_ref, sm_ref = ins[4 * d:4 * d + 4]
        prev_row = xp_ref[b, SUBLANES - 1:SUBLANES, :] * jnp.where(chunk[d] > 0, 1.0, 0.0)
        next_row = xn_ref[b, 0:1, :] * jnp.where(chunk[d] < nc - 1, 1.0, 0.0)
        act = _silu(_conv3(x_ref[b], prev_row, next_row, cw_ref, cb_ref[...]))
        xs.append(act[:, :GROUP_W])
        bm.append(act[:, GROUP_W:GROUP_W + LANES])
        cm.append(act[:, GROUP_W + LANES:])
        dt = _softplus(sm_ref[b] + dtb_ref[...])
        dt_slot.append(dt)
        cum_slot.append(_dot_sel_lhs(tri[d], dt * na_ref[...]))
    cum_w = [_dot_sel_rhs(cum_slot[i], ew_ref[d]) for i, (d, b) in enumerate(chains)]
    cum_2 = [_dot_sel_rhs(cum_slot[i], e2_ref[d]) for i, (d, b) in enumerate(chains)]
    dt_w = [_dot_sel_rhs(dt_slot[i], ew_ref[d]) for i, (d, b) in enumerate(chains)]
    cb = [_dot_nt(jnp.concatenate([jnp.where(lane < C_STATE, c, 0.0), jnp.where(lane >= C_STATE, c, 0.0)], axis=0), bk)
          for c, bk in zip(cm, bm)]
    s_in = [s_ref[bsz * d + b] for d, b in chains]
    y_off = [_dot(cm[i], s_in[i]) * jnp.exp(cum_w[i]) for i in range(len(chains))]
    xdt = [xs[i] * dt_w[i] for i in range(len(chains))]
    for i, (d, b) in enumerate(chains):
        edge = 0 if d else q_len - 1
        cum_edge = cum_w[i][edge:edge + 1, :]
        s_new = jnp.where(smask, _dot_tn(bm[i], xdt[i] * jnp.exp(cum_edge - cum_w[i])), 0.0)
        s_ref[bsz * d + b] = s_in[i] * jnp.exp(cum_edge) + s_new
    y = list(y_off)
    for h in range(HEADS):
        g = h // 2
        head_lanes = (lane_w >= HDIM * h) & (lane_w < HDIM * (h + 1))
        for i, (d, b) in enumerate(chains):
            a_bc = cum_2[i][:, LANES * h:LANES * (h + 1)]
            decay = jnp.exp(jnp.where(causal[d], a_bc - a_bc.T, -1e30))
            scores = cb[i][q_len * g:q_len * (g + 1), :] * decay
            yh = jnp.dot(scores.astype(BF16), xdt[i].astype(BF16), preferred_element_type=F32)
            y[i] = y[i] + jnp.where(head_lanes, yh, 0.0)
    for i, (d, b) in enumerate(chains):
        if d == 0:
            yf_ref[b] = y[i] + xs[i] * dsk_ref[...]
        else:
            yb_ref[b] = y[i]


def _ssd_scan(p3, conv_w, conv_b, dtb_slot, na_slot, dsk_w):
    bsz, seq, _ = p3.shape
    q_len = C_CHUNK
    nc = seq // q_len
    hb = q_len // SUBLANES
    nblk8 = seq // SUBLANES
    ch = (lambda s: s, lambda s: nc - 1 - s)
    trif = _np_bf16(_tri_blocks(q_len, q_len, upper=False))
    trib = _np_bf16(_tri_blocks(q_len, q_len, upper=True))
    e2 = _np_bf16(np.stack([_head_expand(SM_DT + HEADS * d, LANES) for d in range(2)]))
    ew = _np_bf16(np.stack([_head_expand(SM_DT + HEADS * d, HDIM) for d in range(2)]))
    gi = np.arange(LANES)[:, None] // C_STATE
    hi = np.arange(GROUP_W)[None, :] // HDIM
    smask = jnp.asarray((gi == hi // 2).astype(np.float32))

    def dir_specs(d):
        return [pl.BlockSpec((bsz, SUBLANES, C_XBC), lambda s: (0, jnp.maximum(ch[d](s) * hb - 1, 0), 0)),
                pl.BlockSpec((bsz, q_len, C_XBC), lambda s: (0, ch[d](s), OFF_C_XBC // C_XBC)),
                pl.BlockSpec((bsz, SUBLANES, C_XBC), lambda s: (0, jnp.minimum((ch[d](s) + 1) * hb, nblk8 - 1), 0)),
                pl.BlockSpec((bsz, q_len, LANES), lambda s: (0, ch[d](s), OFF_SMALL // LANES))]

    full = lambda a: pl.BlockSpec(a.shape, lambda s: (0,) * a.ndim)
    consts = [conv_w, conv_b, dtb_slot, na_slot, trif, trib, e2, ew, smask, dsk_w]
    out_blk = lambda d: pl.BlockSpec((bsz, q_len, GROUP_W), lambda s: (0, ch[d](s), 0))
    out_shape = jax.ShapeDtypeStruct((bsz, seq, GROUP_W), F32)
    return pl.pallas_call(
        functools.partial(_ssd_kernel, bsz=bsz, nc=nc),
        grid=(nc,),
        in_specs=dir_specs(0) + dir_specs(1) + [full(a) for a in consts],
        out_specs=[out_blk(0), out_blk(1)],
        out_shape=[out_shape, out_shape],
        scratch_shapes=[pltpu.VMEM((2 * bsz, LANES, GROUP_W), F32)],
        compiler_params=_cparams("arbitrary"),
        name="ssd_scan",
    )(*([p3] * 8), *consts)


def _delta_local_kernel(xp_ref, x_ref, xn_ref, sm_ref, cw_ref, na_ref, dtb_ref, ones_ref, trif_ref, trib_ref,
                        eg_ref, eb_ref, bd_ref, u_ref, w_ref, qk_ref, qd_ref, kd_ref, ge_ref, *, nblk, cb, parts):
    blk = pl.program_id(1)
    q_len = D_CHUNK
    rows = x_ref.shape[0]
    part_rows = rows // parts
    part_cb = cb // parts
    ones_bd = ones_ref[...]
    bd_mask = bd_ref[...] > 0
    lane = lax.broadcasted_iota(jnp.int32, (q_len, GROUP_W), 1) & (q_len - 1)
    row = lax.broadcasted_iota(jnp.int32, (q_len, GROUP_W), 0)
    eye_w = jnp.where(row == lane, 1.0, 0.0)
    incl = (row >= lane, row <= lane)
    strict = (row > lane, row < lane)
    tri = (trif_ref[...], trib_ref[...])
    rows_of = lambda c: slice(q_len * c, q_len * (c + 1))

    def l2n(t):
        ss = _dot_sel_rhs(t * t, ones_bd)
        return t * lax.rsqrt(ss + 1e-6)

    def front(g, out):
        r0, r1 = part_rows * g, part_rows * (g + 1)
        prev_row = (xp_ref[SUBLANES - 1:SUBLANES, :] * jnp.where(blk > 0, 1.0, 0.0) if g == 0
                    else x_ref[r0 - 1:r0, :])
        next_row = (xn_ref[0:1, :] * jnp.where(blk < nblk - 1, 1.0, 0.0) if g == parts - 1
                    else x_ref[r1:r1 + 1, :])
        x = x_ref[r0:r1, :]
        n = part_rows
        ri = lax.broadcasted_iota(jnp.int32, (n, GROUP_W), 0)

        def conv_act(cs):
            xc = x[:, cs]
            xm1 = jnp.where(ri == 0, prev_row[:, cs], pltpu.roll(xc, 1, axis=0))
            xp1 = jnp.where(ri == n - 1, next_row[:, cs], pltpu.roll(xc, n - 1, axis=0))
            return _silu(xm1 * cw_ref[0:1, cs] + xc * cw_ref[1:2, cs] + xp1 * cw_ref[2:3, cs])

        out["q"] = l2n(conv_act(slice(0, GROUP_W))) * (HDIM ** -0.5)
        yield
        out["k"] = l2n(conv_act(slice(GROUP_W, 2 * GROUP_W)))
        yield
        out["v"] = conv_act(slice(2 * GROUP_W, 3 * GROUP_W))
        sm = sm_ref[r0:r1, :]
        beta_slot = jax.nn.sigmoid(sm)
        g_slot = na_ref[...] * _softplus(sm + dtb_ref[...])
        yield
        out["g_w"] = [_dot_sel_rhs(g_slot, eg_ref[d]) for d in range(2)]
        out["beta_w"] = [_dot_sel_rhs(beta_slot, eb_ref[d]) for d in range(2)]
        yield

    def back(g, f):
        q_all, k_all, v_all, g_w, beta_w = f["q"], f["k"], f["v"], f["g_w"], f["beta_w"]
        units = [(d, c) for d in range(2) for c in range(part_cb)]
        seg, g_cum = [], []
        for d, c in units:
            gu = g_w[d][rows_of(c)]
            sg = _dot_sel_lhs(tri[d], jnp.concatenate([jnp.where(strict[d], gu, 0.0), gu], axis=1))
            seg.append(sg[:, :GROUP_W])
            g_cum.append(sg[:, GROUP_W:])
        yield
        kq = []
        for d, c in units:
            kc = k_all[rows_of(c)]
            lhs = jnp.concatenate([kc * beta_w[d][rows_of(c)], q_all[rows_of(c)]], axis=0)
            kq.append(_dot_nt(lhs, _block_diag(kc, bd_mask)))
        yield
        decay = [jnp.exp(jnp.where(incl[d], seg[i], -1e30)) for i, (d, c) in enumerate(units)]
        n_w = [jnp.where(strict[d], -(kq[i][:q_len] * decay[i]), 0.0) for i, (d, c) in enumerate(units)]
        p_w = [eye_w + n for n in n_w]
        m_w = [_dot(n, _block_diag(n, bd_mask)) for n in n_w]
        yield
        for _ in range(4):
            r = [_dot(jnp.concatenate([m, p], axis=0), _block_diag(m, bd_mask)) for m, p in zip(m_w, p_w)]
            m_w = [ri[:q_len] for ri in r]
            p_w = [p + ri[q_len:] for p, ri in zip(p_w, r)]
            yield
        t_w = [p + _dot(p, _block_diag(m, bd_mask)) for m, p in zip(m_w, p_w)]
        yield
        for i, (d, c) in enumerate(units):
            rs = rows_of(c)
            orow = slice(part_rows * g + q_len * c, part_rows * g + q_len * (c + 1))
            oc = part_cb * g + c
            kc, bw = k_all[rs], beta_w[d][rs]
            exp_g = jnp.exp(g_cum[i])
            edge = 0 if d else q_len - 1
            g_edge = g_cum[i][edge:edge + 1, :]
            rhs = jnp.concatenate([_block_diag(v_all[rs] * bw, bd_mask), _block_diag(kc * bw * exp_g, bd_mask)],
                                  axis=1)
            uw = _dot(t_w[i], rhs)
            u_ref[d, orow, :] = uw[:, :GROUP_W]
            w_ref[d, orow, :] = uw[:, GROUP_W:].astype(BF16)
            qk_ref[d, orow, :] = (kq[i][q_len:] * decay[i]).astype(BF16)
            qd_ref[d, orow, :] = (q_all[rs] * exp_g).astype(BF16)
            kd_ref[d, orow, :] = (kc * jnp.exp(g_edge - g_cum[i])).astype(BF16)
            ge_ref[d, SUBLANES * oc:SUBLANES * (oc + 1), :] = jnp.broadcast_to(jnp.exp(g_edge), (SUBLANES, GROUP_W))
        yield

    fronts = [dict() for _ in range(parts)]
    for _ in front(0, fronts[0]):
        pass
    for g in range(parts):
        nxt = front(g + 1, fronts[g + 1]) if g + 1 < parts else iter(())
        for _ in back(g, fronts[g]):
            next(nxt, None)
        for _ in nxt:
            pass


def _delta_local(p3, conv_w, na_slot, dtb_slot):
    bsz, seq, _ = p3.shape
    q_len = D_CHUNK
    rows = min(512, seq)
    cb = rows // q_len
    parts = 2
    nblk = seq // rows
    hb = rows // SUBLANES
    nblk8 = seq // SUBLANES
    ones_bd = _np_bf16(np.kron(np.eye(HEADS), np.ones((HDIM, HDIM))))
    trif = _np_bf16(_tri_blocks(q_len, q_len, upper=False))
    trib = _np_bf16(_tri_blocks(q_len, q_len, upper=True))
    eg = _np_bf16(np.stack([_head_expand(SM_AB + 8 + HEADS * d, HDIM) for d in range(2)]))
    eb = _np_bf16(np.stack([_head_expand(SM_AB + HEADS * d, HDIM) for d in range(2)]))
    bd = jnp.asarray(np.kron(np.eye(HEADS), np.ones((q_len, HDIM))).astype(np.float32))
    x_spec = pl.BlockSpec((None, rows, D_QKV), lambda b, i: (b, i, OFF_D_QKV // D_QKV))
    xp_spec = pl.BlockSpec((None, SUBLANES, D_QKV), lambda b, i: (b, jnp.maximum(i * hb - 1, 0), OFF_D_QKV // D_QKV))
    xn_spec = pl.BlockSpec((None, SUBLANES, D_QKV), lambda b, i: (b, jnp.minimum((i + 1) * hb, nblk8 - 1), OFF_D_QKV // D_QKV))
    sm_spec = pl.BlockSpec((None, rows, LANES), lambda b, i: (b, i, OFF_SMALL // LANES))
    full = lambda a: pl.BlockSpec(a.shape, lambda b, i: (0,) * a.ndim)
    consts = [conv_w, na_slot, dtb_slot, ones_bd, trif, trib, eg, eb, bd]
    out_blk = pl.BlockSpec((None, 2, rows, GROUP_W), lambda b, i: (b, 0, i, 0))
    ge_blk = pl.BlockSpec((None, 2, SUBLANES * cb, GROUP_W), lambda b, i: (b, 0, i, 0))
    shp = lambda dt: jax.ShapeDtypeStruct((bsz, 2, seq, GROUP_W), dt)
    return pl.pallas_call(
        functools.partial(_delta_local_kernel, nblk=nblk, cb=cb, parts=parts),
        grid=(bsz, nblk),
        in_specs=[xp_spec, x_spec, xn_spec, sm_spec] + [full(a) for a in consts],
        out_specs=[out_blk] * 5 + [ge_blk],
        out_shape=[shp(F32), shp(BF16), shp(BF16), shp(BF16), shp(BF16),
                   jax.ShapeDtypeStruct((bsz, 2, seq // q_len * SUBLANES, GROUP_W), F32)],
        compiler_params=_cparams("parallel", "parallel"),
        name="delta_local",
    )(p3, p3, p3, p3, *consts)


def _delta_scan_kernel(*refs, bsz):
    ins, (bd_ref, of_ref, ob_ref, s_ref) = refs[:12], refs[12:]
    o_refs = (of_ref, ob_ref)

    @pl.when(pl.program_id(0) == 0)
    def _():
        s_ref[...] = jnp.zeros_like(s_ref)

    q_len = D_CHUNK
    bd_mask = bd_ref[...] > 0
    chains = [(d, b) for d in range(2) for b in range(bsz)]
    get = lambda k, d, b: ins[6 * d + k][b]
    s = [s_ref[bsz * d + b] for d, b in chains]
    r = [jnp.dot(jnp.concatenate([get(1, d, b), get(3, d, b)], axis=0), s[i].astype(BF16),
                 preferred_element_type=F32) for i, (d, b) in enumerate(chains)]
    v_new = [get(0, d, b) - r[i][:q_len] for i, (d, b) in enumerate(chains)]
    upd = [lax.dot_general(get(4, d, b), v_new[i].astype(BF16), (((0,), (0,)), ((), ())),
                           preferred_element_type=F32) for i, (d, b) in enumerate(chains)]
    for i, (d, b) in enumerate(chains):
        s_ref[bsz * d + b] = s[i] * get(5, d, b)[0:1, :] + jnp.where(bd_mask, upd[i], 0.0)
    for i, (d, b) in enumerate(chains):
        o_refs[d][b] = r[i][q_len:] + jnp.dot(get(2, d, b), _block_diag(v_new[i], bd_mask),
                                              preferred_element_type=F32)


def _delta_scan(u, w, qk, qd, kd, ge):
    bsz, _, seq, _ = u.shape
    q_len = D_CHUNK
    nc = seq // q_len
    bd = jnp.asarray(np.kron(np.eye(HEADS), np.ones((q_len, HDIM))).astype(np.float32))
    ch = (lambda c: c, lambda c: nc - 1 - c)
    blk = lambda d, rows: pl.BlockSpec((bsz, None, rows, GROUP_W), lambda c: (0, d, ch[d](c), 0))
    in_specs = [blk(d, rows) for d in range(2) for rows in (q_len,) * 5 + (SUBLANES,)]
    out_blk = lambda d: pl.BlockSpec((bsz, q_len, GROUP_W), lambda c: (0, ch[d](c), 0))
    out_shape = jax.ShapeDtypeStruct((bsz, seq, GROUP_W), F32)
    return pl.pallas_call(
        functools.partial(_delta_scan_kernel, bsz=bsz),
        grid=(nc,),
        in_specs=in_specs + [pl.BlockSpec(bd.shape, lambda c: (0, 0))],
        out_specs=[out_blk(0), out_blk(1)],
        out_shape=[out_shape, out_shape],
        scratch_shapes=[pltpu.VMEM((2 * bsz, GROUP_W, GROUP_W), F32)],
        compiler_params=_cparams("arbitrary"),
        name="delta_scan",
    )(*([u, w, qk, qd, kd, ge] * 2), bd)


def _out_proj_kernel(oa_ref, ob_ref, ycf_ref, ycb_ref, zc_ref, odf_ref, odb_ref, zd_ref, x_ref,
                     wa_ref, wb_ref, wc_ref, wd_ref, ga_ref, gb_ref, gc_ref, gd_ref, ones_ref, gp_ref, o_ref):
    oa = (_rms(oa_ref[...].astype(F32), GROUP_W) * ga_ref[...]).astype(BF16)
    ob = (_rms(ob_ref[...].astype(F32), GROUP_W) * gb_ref[...]).astype(BF16)
    oc = (_rms((ycf_ref[...] + ycb_ref[...]) * _silu(zc_ref[...]), GROUP_W) * gc_ref[...]).astype(BF16)
    od = odf_ref[...] + odb_ref[...]
    ms = _dot_sel_rhs(od * od, ones_ref[...]) * (1.0 / HDIM)
    od = od * lax.rsqrt(ms + EPS) * gd_ref[...] * _silu(zd_ref[...])
    acc = jnp.dot(oa, wa_ref[...], preferred_element_type=F32)
    acc += jnp.dot(ob, wb_ref[...], preferred_element_type=F32)
    acc += jnp.dot(oc, wc_ref[...], preferred_element_type=F32)
    acc += jnp.dot(od.astype(BF16), wd_ref[...], preferred_element_type=F32)
    o_ref[...] = x_ref[...] + _rms(acc, D_MODEL) * gp_ref[...]


def _out_proj(oa, ob, ycf, ycb, odf, odb, p3, x3, wa, wb, wc, wd, ga, gb, gc, gd, gpost):
    bsz, seq, _ = x3.shape
    tm = min(512, seq)
    ones_bd = _np_bf16(np.kron(np.eye(HEADS), np.ones((HDIM, HDIM))))
    rows = lambda w: pl.BlockSpec((None, tm, w), lambda b, i: (b, i, 0))
    full = lambda a: pl.BlockSpec(a.shape, lambda b, i: (0,) * a.ndim)
    z_spec = lambda off: pl.BlockSpec((None, tm, GROUP_W), lambda b, i: (b, i, off // GROUP_W))
    consts = [wa, wb, wc, wd, ga, gb, gc, gd, ones_bd, gpost]
    return pl.pallas_call(
        _out_proj_kernel,
        grid=(bsz, seq // tm),
        in_specs=[rows(HEADS * LANES), rows(HEADS * LANES), rows(GROUP_W), rows(GROUP_W), z_spec(OFF_C_Z),
                  rows(GROUP_W), rows(GROUP_W), z_spec(OFF_D_Z), rows(D_MODEL)] + [full(a) for a in consts],
        out_specs=rows(D_MODEL),
        out_shape=jax.ShapeDtypeStruct(x3.shape, F32),
        compiler_params=_cparams("parallel", "parallel"),
        name="out_proj",
    )(oa, ob, ycf, ycb, p3, odf, odb, p3, x3, *consts)


def _ffn_kernel(xp_ref, x_ref, xn_ref, gpre_ref, win_ref, cw_ref, cb_ref, wout_ref, gpost_ref, o_ref,
                gate_ref, up_ref, act_ref, *, tiles_per_seq, nj):
    tm = x_ref.shape[0]
    halo = BF16_ROWS
    pos = lax.rem(pl.program_id(0), tiles_per_seq)
    norm = lambda t: _rms(t, D_MODEL) * gpre_ref[...]
    h = jnp.concatenate([(norm(xp_ref[...]) * jnp.where(pos == 0, 0.0, 1.0)).astype(BF16),
                         norm(x_ref[...]).astype(BF16),
                         (norm(xn_ref[...]) * jnp.where(pos == tiles_per_seq - 1, 0.0, 1.0)).astype(BF16)], axis=0)
    n = tm + 2 * halo
    body = slice(halo, halo + tm)

    def project(j):
        for ref, off in ((gate_ref, 0), (up_ref, D_FF)):
            ref[j % 2] = jnp.dot(h, win_ref[:, off + FF_TILE * j:off + FF_TILE * (j + 1)],
                                 preferred_element_type=F32)

    def conv(ref, off, j):
        t = ref[j % 2]
        cs = slice(off + FF_TILE * j, off + FF_TILE * (j + 1))
        return (pltpu.roll(t, 1, axis=0)[body] * cw_ref[0:1, cs] + t[body] * cw_ref[1:2, cs]
                + pltpu.roll(t, n - 1, axis=0)[body] * cw_ref[2:3, cs] + cb_ref[:, cs])

    project(0)
    for j in range(nj):
        if j + 1 < nj:
            project(j + 1)
        act_ref[:, FF_TILE * j:FF_TILE * (j + 1)] = (
            _silu(conv(gate_ref, 0, j)) * conv(up_ref, D_FF, j)).astype(BF16)
    acc = jnp.dot(act_ref[...], wout_ref[...], preferred_element_type=F32)
    o_ref[...] = x_ref[...] + _rms(acc, D_MODEL) * gpost_ref[...]


def _ffn(x2d, seq, layer, gpre, w_in, conv_w, conv_b, w_out, gpost):
    m = x2d.shape[0]
    tm = min(512, seq)
    nj = D_FF // FF_TILE
    hb = tm // BF16_ROWS
    nblk = m // BF16_ROWS
    kern = functools.partial(_ffn_kernel, tiles_per_seq=seq // tm, nj=nj)
    const = lambda a: pl.BlockSpec((None,) + a.shape[1:], lambda i: (layer, 0, 0), pipeline_mode=pl.Buffered(1))
    return pl.pallas_call(
        kern,
        grid=(m // tm,),
        in_specs=[pl.BlockSpec((BF16_ROWS, D_MODEL), lambda i: (jnp.maximum(i * hb - 1, 0), 0)),
                  pl.BlockSpec((tm, D_MODEL), lambda i: (i, 0)),
                  pl.BlockSpec((BF16_ROWS, D_MODEL), lambda i: (jnp.minimum((i + 1) * hb, nblk - 1), 0)),
                  const(gpre), const(w_in), const(conv_w), const(conv_b), const(w_out), const(gpost)],
        out_specs=pl.BlockSpec((tm, D_MODEL), lambda i: (i, 0)),
        out_shape=jax.ShapeDtypeStruct(x2d.shape, F32),
        scratch_shapes=[pltpu.VMEM((2, tm + 2 * BF16_ROWS, FF_TILE), F32),
                        pltpu.VMEM((2, tm + 2 * BF16_ROWS, FF_TILE), F32),
                        pltpu.VMEM((tm, D_FF), BF16)],
        compiler_params=_cparams("parallel"),
        name="conv_ffn",
    )(x2d, x2d, x2d, gpre, w_in, conv_w, conv_b, w_out, gpost)


def _slot_vec(off, vals):
    n = vals.shape[-1]
    return jnp.pad(vals.astype(F32), ((0, 0), (off, LANES - off - n)))[:, None, :]


def _prep_layers(prm):
    nl = prm["w_in"].shape[0]
    row = lambda g: g[:, None, :]
    wuq = prm["a_w_uq"].reshape(nl, A_Q_LORA, HEADS, A_NOPE + A_ROPE)
    rope_w = wuq[..., A_NOPE:]
    zpad = jnp.zeros((nl, A_Q_LORA, HEADS, LANES - A_NOPE - A_ROPE), F32)
    flat_q = lambda parts: jnp.pad(jnp.concatenate(parts, axis=-1).reshape(nl, A_Q_LORA, HEADS * LANES),
                                   ((0, 0), (0, 256 - A_Q_LORA), (0, 0))).astype(BF16)
    w1 = flat_q([wuq[..., :A_NOPE], rope_w, zpad])
    w2 = flat_q([jnp.zeros_like(wuq[..., :A_NOPE]), _rot_last(rope_w), zpad])
    wukv = prm["a_w_ukv"].reshape(nl, A_KV_LORA, HEADS, A_NOPE + HDIM)
    hpad = jnp.zeros((nl, A_KV_LORA, HEADS, LANES - HDIM), F32)
    flat_kv = lambda t: jnp.concatenate([t, hpad], axis=-1).reshape(nl, A_KV_LORA, HEADS * LANES).astype(BF16)
    c_b = (HDIM ** -0.5) * LOG2E
    pad64 = lambda g: row(jnp.pad(g, ((0, 0), (0, LANES - HDIM))))
    wo = prm["w_out"]
    return dict(
        w_in=_arrange_w_in(prm["w_in"]),
        pre_mix_norm=prm["pre_mix_norm"],
        mla=(row(jnp.pad(prm["a_q_norm"], ((0, 0), (0, 256 - A_Q_LORA)))), row(prm["a_kv_norm"]),
             w1, w2, flat_kv(wukv[..., :A_NOPE]), flat_kv(wukv[..., A_NOPE:])),
        gqa=(pad64(prm["b_q_norm"] * c_b), pad64(_swap_last(prm["b_q_norm"]) * c_b),
             pad64(prm["b_k_norm"]), pad64(_swap_last(prm["b_k_norm"]))),
        ssd=(prm["c_conv_w"], row(prm["c_conv_b"]), _slot_vec(SM_DT, prm["c_dt_bias"].reshape(nl, -1)),
             _slot_vec(SM_DT, -jnp.exp(prm["c_a_log"].astype(F32)).reshape(nl, -1)),
             row(jnp.repeat(prm["c_d_skip"].astype(F32), HDIM, axis=-1))),
        delta=(prm["d_conv_w"], _slot_vec(SM_AB + 8, -jnp.exp(prm["d_a_log"].astype(F32)).reshape(nl, -1)),
               _slot_vec(SM_AB + 8, prm["d_dt_bias"].reshape(nl, -1))),
        out=(_pad_heads(wo[:, 0:256], 1, HEADS, HDIM, LANES).astype(BF16),
             _pad_heads(wo[:, 256:512], 1, HEADS, HDIM, LANES).astype(BF16),
             wo[:, 512:768].astype(BF16), wo[:, 768:1024].astype(BF16),
             row(_pad_heads(prm["a_out_norm"], 1, HEADS, HDIM, LANES)),
             row(_pad_heads(prm["b_out_norm"], 1, HEADS, HDIM, LANES)),
             row(prm["c_out_norm"]), row(jnp.tile(prm["d_out_norm"], (1, HEADS))),
             row(prm["post_mix_norm"])),
    )


def _layer(x3, tabs, w, layer, ffn_prm):
    bsz, seq, _ = x3.shape
    p2d, qa, ka, va, qb, kb, vb = _in_proj(x3, w["pre_mix_norm"], w["w_in"], tabs, w["mla"], w["gqa"])
    p3 = p2d.reshape(bsz, seq, P_HBM_COLS)
    oa = _attention(qa, ka, va)
    ob = _attention(qb, kb, vb)
    ycf, ycb = _ssd_scan(p3, *w["ssd"])
    odf, odb = _delta_scan(*_delta_local(p3, *w["delta"]))
    x3 = _out_proj(oa, ob, ycf, ycb, odf, odb, p3, x3, *w["out"])
    x2d = _ffn(x3.reshape(bsz * seq, D_MODEL), seq, layer, *ffn_prm)
    return x2d.reshape(bsz, seq, D_MODEL)


def _attn_tables(seq):
    cos_a, sin_a = _rope_tables(seq, A_ROPE)
    cos_b, sin_b = _rope_tables(seq, HDIM)
    pada = lambda t: jnp.pad(t, ((0, 0), (A_NOPE, LANES - A_NOPE - A_ROPE)))
    padb = lambda t: jnp.pad(t, ((0, 0), (0, LANES - HDIM)))
    return (pada(cos_a), pada(sin_a)), (padb(cos_b), padb(sin_b))


def kernel(x, pre_mix_norm, w_in, a_q_norm, a_w_uq, a_kv_norm, a_w_ukv, a_out_norm, b_q_norm, b_k_norm, b_out_norm, c_conv_w, c_conv_b, c_a_log, c_dt_bias, c_d_skip, c_out_norm, d_conv_w, d_a_log, d_dt_bias, d_out_norm, w_out, post_mix_norm, pre_ffn_norm, f_w_in, f_conv_w, f_conv_b, f_w_out, post_ffn_norm):
    params = dict(pre_mix_norm=pre_mix_norm, w_in=w_in, a_q_norm=a_q_norm, a_w_uq=a_w_uq, a_kv_norm=a_kv_norm,
                  a_w_ukv=a_w_ukv, a_out_norm=a_out_norm, b_q_norm=b_q_norm, b_k_norm=b_k_norm,
                  b_out_norm=b_out_norm, c_conv_w=c_conv_w, c_conv_b=c_conv_b, c_a_log=c_a_log,
                  c_dt_bias=c_dt_bias, c_d_skip=c_d_skip, c_out_norm=c_out_norm, d_conv_w=d_conv_w,
                  d_a_log=d_a_log, d_dt_bias=d_dt_bias, d_out_norm=d_out_norm, w_out=w_out,
                  post_mix_norm=post_mix_norm)
    ffn_prm = (pre_ffn_norm[:, None, :], f_w_in.astype(BF16), f_conv_w, f_conv_b[:, None, :],
               f_w_out.astype(BF16), post_ffn_norm[:, None, :])
    tabs_a, tabs_b = _attn_tables(x.shape[1])
    prepared = _prep_layers(params)
    for layer in range(w_in.shape[0]):
        w = jax.tree_util.tree_map(lambda a: a[layer], prepared)
        x = _layer(x, tabs_a + tabs_b, w, layer, ffn_prm)
    return x
```

```python
import functools
import math

import numpy as np
import jax
import jax.numpy as jnp
from jax import lax
from jax.experimental import pallas as pl
from jax.experimental.pallas import tpu as pltpu

F32 = jnp.float32
BF16 = jnp.bfloat16

LANES = 128
SUBLANES = 8
BF16_ROWS = 16
VMEM_LIMIT = 56 * 1024 * 1024

EPS = 1e-6
ROPE_BASE = 10000.0
GRID_W = 64
D_MODEL = 1024
GROUP_W = 256
HEADS = 4
HDIM = 64

A_NOPE, A_ROPE, A_Q_LORA, A_KV_LORA = 64, 32, 192, 128
B_KV_HEADS = 2
C_STATE, C_CHUNK, C_XBC = 64, 128, 512
D_CHUNK, D_QKV = 64, 768
D_FF = 2816
FF_TILE = 256

OFF_C_XBC, OFF_D_Z, OFF_D_QKV, OFF_C_Z, OFF_SMALL = 0, 512, 768, 1536, 1792
P_HBM_COLS = 1920
OFF_A_CQ, OFF_B_Q, OFF_B_QROT = 1920, 2176, 2432
OFF_A_CKV, OFF_B_K, OFF_B_KROT, OFF_B_V = 2688, 2816, 2944, 3072
IN_COLS_PADDED = 3200
SM_DT, SM_AB, SM_KR, SM_KRROT = 0, 16, 64, 96

LOG2E = math.log2(math.e)


def _cparams(*sem):
    return pltpu.CompilerParams(dimension_semantics=sem, vmem_limit_bytes=VMEM_LIMIT)


def _dot(a, b):
    return jnp.dot(a.astype(BF16), b.astype(BF16), preferred_element_type=F32)


def _dot_nt(a, b):
    return lax.dot_general(a.astype(BF16), b.astype(BF16), (((1,), (1,)), ((), ())),
                           preferred_element_type=F32)


def _dot_tn(a, b):
    return lax.dot_general(a.astype(BF16), b.astype(BF16), (((0,), (0,)), ((), ())),
                           preferred_element_type=F32)


def _split3(x):
    hi = x.astype(BF16)
    r1 = x - hi.astype(F32)
    mid = r1.astype(BF16)
    lo = (r1 - mid.astype(F32)).astype(BF16)
    return hi, mid, lo


def _dot_sel_rhs(x, sel):
    hi, mid, lo = _split3(x)
    d = lambda p: jnp.dot(p, sel, preferred_element_type=F32)
    return d(hi) + d(mid) + d(lo)


def _dot_sel_lhs(sel, x):
    hi, mid, lo = _split3(x)
    d = lambda p: jnp.dot(sel, p, preferred_element_type=F32)
    return d(hi) + d(mid) + d(lo)


def _softplus(x):
    return jnp.maximum(x, 0.0) + jnp.log1p(jnp.exp(-jnp.abs(x)))


def _silu(x):
    return x * jax.nn.sigmoid(x)


def _rms(x, n):
    return x * lax.rsqrt(jnp.sum(x * x, axis=-1, keepdims=True) * (1.0 / n) + EPS)


def _conv3(x, prev_row, next_row, w_ref, bias):
    n = x.shape[0]
    row = lax.broadcasted_iota(jnp.int32, x.shape, 0)
    xm1 = jnp.where(row == 0, prev_row, pltpu.roll(x, 1, axis=0))
    xp1 = jnp.where(row == n - 1, next_row, pltpu.roll(x, n - 1, axis=0))
    y = xm1 * w_ref[0:1, :] + x * w_ref[1:2, :] + xp1 * w_ref[2:3, :]
    return y if bias is None else y + bias


def _block_diag(x, mask):
    return jnp.where(mask, jnp.concatenate([x] * HEADS, axis=0), 0.0).astype(BF16)


def _np_bf16(a):
    return jnp.asarray(np.asarray(a, np.float32), BF16)


def _head_expand(first_lane, width):
    e = np.zeros((LANES, HEADS * width), np.float32)
    for h in range(HEADS):
        e[first_lane + h, h * width:(h + 1) * width] = 1.0
    return e


def _tri_blocks(n, blk, upper):
    i = np.arange(n)
    same = (i[:, None] // blk) == (i[None, :] // blk)
    tri = (i[:, None] <= i[None, :]) if upper else (i[:, None] >= i[None, :])
    return (same & tri).astype(np.float32)


def _rot_last(w):
    r = w.shape[-1]
    xs = w.reshape(w.shape[:-1] + (2, 2, r // 4))
    return jnp.stack([-xs[..., 1, :], xs[..., 0, :]], axis=-2).reshape(w.shape)


def _swap_last(w):
    r = w.shape[-1]
    xs = w.reshape(w.shape[:-1] + (2, 2, r // 4))
    return jnp.stack([xs[..., 1, :], xs[..., 0, :]], axis=-2).reshape(w.shape)


def _rope_tables(seq_len, rot_dim):
    rows = seq_len // GRID_W
    sec = rot_dim // 2
    inv_freq = ROPE_BASE ** (-jnp.arange(0, sec, 2, dtype=F32) / sec)
    ang_r = jnp.arange(rows).astype(F32)[:, None] * inv_freq
    ang_c = jnp.arange(GRID_W).astype(F32)[:, None] * inv_freq

    def table(fn):
        t_r = jnp.repeat(fn(ang_r), GRID_W, axis=0)
        t_c = jnp.tile(fn(ang_c), (rows, 1))
        return jnp.concatenate([t_r, t_r, t_c, t_c], axis=-1)

    return table(jnp.cos), table(jnp.sin)


def _pad_heads(a, axis, n_heads, real, padded):
    shp = list(a.shape)
    a = a.reshape(shp[:axis] + [n_heads, real] + shp[axis + 1:])
    pad = [(0, 0)] * a.ndim
    pad[axis + 1] = (0, padded - real)
    a = jnp.pad(a, pad)
    return a.reshape(shp[:axis] + [n_heads * padded] + shp[axis + 1:])


def _arrange_w_in(w):
    a0, b0, c0, d0 = 0, 352, 864, 1640
    lead = w.shape[:-1]
    zeros = lambda n: jnp.zeros(lead + (n,), w.dtype)
    cols_of = lambda lo, hi: w[..., lo:hi]
    cq, ckv, kr = cols_of(a0, a0 + 192), cols_of(a0 + 192, a0 + 320), cols_of(a0 + 320, a0 + 352)
    bq, bk, bv = cols_of(b0, b0 + 256), cols_of(b0 + 256, b0 + 384), cols_of(b0 + 384, b0 + 512)
    cz, cxbc, cdt = cols_of(c0, c0 + 256), cols_of(c0 + 256, c0 + 768), cols_of(c0 + 768, c0 + 776)
    dqkv, dz, dab = cols_of(d0, d0 + 768), cols_of(d0 + 768, d0 + 1024), cols_of(d0 + 1024, d0 + 1040)
    bq_rot = _rot_last(bq.reshape(lead + (HEADS, HDIM))).reshape(lead + (256,))
    bk_rot = _rot_last(bk.reshape(lead + (B_KV_HEADS, HDIM))).reshape(lead + (128,))
    small = jnp.concatenate([cdt, zeros(8), dab, zeros(32), kr, _rot_last(kr)], axis=-1)
    cols = [cxbc, dz, dqkv, cz, small, cq, zeros(64), bq, bq_rot, ckv, bk, bk_rot, bv]
    out = jnp.concatenate(cols, axis=-1)
    assert out.shape[-1] == IN_COLS_PADDED
    return out.astype(BF16)


def _mla_heads(cq, ckv, sm, ckt, skt, gq_ref, gkv_ref, w1_ref, w2_ref, wk_ref, wv_ref, q_ref, k_ref, v_ref, c_a):
    cqn = (_rms(cq, A_Q_LORA) * gq_ref[...]).astype(BF16)
    q1 = jnp.dot(cqn, w1_ref[...], preferred_element_type=F32)
    q2 = jnp.dot(cqn, w2_ref[...], preferred_element_type=F32)
    kvn = (_rms(ckv, A_KV_LORA) * gkv_ref[...]).astype(BF16)
    k1 = jnp.dot(kvn, wk_ref[...], preferred_element_type=F32)
    v1 = jnp.dot(kvn, wv_ref[...], preferred_element_type=F32)
    k_rope = sm * ckt + pltpu.roll(sm, LANES - A_ROPE, axis=1) * skt
    lane = lax.broadcasted_iota(jnp.int32, sm.shape, 1)
    cqt = jnp.where(lane < A_NOPE, c_a, c_a * ckt)
    sqt = c_a * skt
    ones_lane = lane == HDIM
    for h in range(HEADS):
        sl = slice(LANES * h, LANES * (h + 1))
        q_ref[h] = (q1[:, sl] * cqt + q2[:, sl] * sqt).astype(BF16)
        k_ref[h] = (k1[:, sl] + k_rope).astype(BF16)
        v_ref[h] = jnp.where(ones_lane, 1.0, v1[:, sl]).astype(BF16)


def _head_slot(x, h, lo):
    grp = x[:, LANES * (h // 2):LANES * (h // 2 + 1)]
    if h % 2:
        grp = pltpu.roll(grp, HDIM, axis=1)
    return jnp.where(lo, grp, 0.0)


def _gqa_heads(q, qr, k, kr, v, cos, sin, gq_ref, gqs_ref, gk_ref, gks_ref, qo_ref, ko_ref, vo_ref):
    lane = lax.broadcasted_iota(jnp.int32, cos.shape, 1)
    lo = lane < HDIM
    ones_lane = lane == HDIM

    def normed_rope(x, xr, h, g, gs):
        xh, xrh = _head_slot(x, h, lo), _head_slot(xr, h, lo)
        r = lax.rsqrt(jnp.sum(xh * xh, axis=-1, keepdims=True) * (1.0 / HDIM) + EPS)
        return (r * (xh * (cos * g) + xrh * (sin * gs))).astype(BF16)

    for h in range(HEADS):
        qo_ref[h] = normed_rope(q, qr, h, gq_ref[...], gqs_ref[...])
    for h in range(B_KV_HEADS):
        ko_ref[h] = normed_rope(k, kr, h, gk_ref[...], gks_ref[...])
        vo_ref[h] = jnp.where(ones_lane, 1.0, _head_slot(v, h, lo)).astype(BF16)


def _in_proj_kernel(x_ref, g_ref, w_ref, ckt_ref, skt_ref, cosb_ref, sinb_ref, gq_ref, gkv_ref, w1_ref, w2_ref,
                    wk_ref, wv_ref, gbq_ref, gbqs_ref, gbk_ref, gbks_ref,
                    p_ref, qa_ref, ka_ref, va_ref, qb_ref, kb_ref, vb_ref, *, c_a):
    h = (_rms(x_ref[...], D_MODEL) * g_ref[...]).astype(BF16)
    p_att = jnp.dot(h, w_ref[:, P_HBM_COLS:], preferred_element_type=F32)
    p_hbm = jnp.dot(h, w_ref[:, :P_HBM_COLS], preferred_element_type=F32)
    p_ref[...] = p_hbm

    def col(off, w):
        src, base = (p_hbm, 0) if off < P_HBM_COLS else (p_att, P_HBM_COLS)
        return src[:, off - base:off - base + w]

    _mla_heads(col(OFF_A_CQ, 256), col(OFF_A_CKV, LANES), col(OFF_SMALL, LANES), ckt_ref[...], skt_ref[...],
               gq_ref, gkv_ref, w1_ref, w2_ref, wk_ref, wv_ref, qa_ref, ka_ref, va_ref, c_a)
    _gqa_heads(col(OFF_B_Q, 256), col(OFF_B_QROT, 256), col(OFF_B_K, LANES), col(OFF_B_KROT, LANES),
               col(OFF_B_V, LANES), cosb_ref[...], sinb_ref[...], gbq_ref, gbqs_ref, gbk_ref, gbks_ref,
               qb_ref, kb_ref, vb_ref)


def _in_proj(x3, gain, w, tabs, mla_prm, gqa_prm):
    bsz, seq, _ = x3.shape
    m = bsz * seq
    tm = min(512, seq)
    tps = seq // tm
    c_a = ((A_NOPE + A_ROPE) ** -0.5) * LOG2E
    consts = list(mla_prm) + list(gqa_prm)
    tab = pl.BlockSpec((tm, LANES), lambda i: (i % tps, 0))
    full = lambda a: pl.BlockSpec(a.shape, lambda i: (0,) * a.ndim)
    heads = lambda n: pl.BlockSpec((None, n, tm, LANES), lambda i: (i // tps, 0, i % tps, 0))
    shp = lambda n: jax.ShapeDtypeStruct((bsz, n, seq, LANES), BF16)
    return pl.pallas_call(
        functools.partial(_in_proj_kernel, c_a=c_a),
        grid=(m // tm,),
        in_specs=[pl.BlockSpec((tm, D_MODEL), lambda i: (i, 0)),
                  pl.BlockSpec((1, D_MODEL), lambda i: (0, 0)),
                  pl.BlockSpec((D_MODEL, IN_COLS_PADDED), lambda i: (0, 0), pipeline_mode=pl.Buffered(1)),
                  tab, tab, tab, tab] + [full(a) for a in consts],
        out_specs=[pl.BlockSpec((tm, P_HBM_COLS), lambda i: (i, 0)),
                   heads(HEADS), heads(HEADS), heads(HEADS), heads(HEADS), heads(B_KV_HEADS), heads(B_KV_HEADS)],
        out_shape=[jax.ShapeDtypeStruct((m, P_HBM_COLS), F32),
                   shp(HEADS), shp(HEADS), shp(HEADS), shp(HEADS), shp(B_KV_HEADS), shp(B_KV_HEADS)],
        compiler_params=_cparams("parallel"),
        name="in_proj",
    )(x3.reshape(m, D_MODEL), gain.reshape(1, -1), w, *tabs, *consts)


def _attn_kernel(q_ref, k_ref, v_ref, o_ref, s0_ref, s1_ref, p0_ref, p1_ref, acc_ref, *, tk, nk):
    q = q_ref[...]
    tq = q.shape[0]

    s_refs, p_refs = (s0_ref, s1_ref), (p0_ref, p1_ref)

    def scores(c):
        s_refs[c % 2][...] = lax.dot_general(q, k_ref[tk * c:tk * (c + 1), :], (((1,), (1,)), ((), ())),
                                             preferred_element_type=F32)

    def probs(c, m):
        s = s_refs[c % 2][...]
        m_new = jnp.maximum(m, jnp.max(s, axis=-1, keepdims=True))
        p_refs[c % 2][...] = jnp.exp2(s - m_new).astype(BF16)
        return jnp.exp2(m - m_new), m_new

    def accumulate(c, alpha):
        pv = jnp.dot(p_refs[c % 2][...], v_ref[tk * c:tk * (c + 1), :], preferred_element_type=F32)
        acc_ref[...] = pv if c == 0 else alpha * acc_ref[...] + pv

    m = jnp.full((tq, 1), -1e30, F32)
    alphas = {}
    scores(0)
    for c in range(nk):
        if c + 1 < nk:
            scores(c + 1)
        alphas[c], m = probs(c, m)
        if c >= 1:
            accumulate(c - 1, alphas[c - 1])
    accumulate(nk - 1, alphas[nk - 1])
    acc = acc_ref[...]
    lane = lax.broadcasted_iota(jnp.int32, acc.shape, 1)
    o_ref[...] = jnp.where(lane < HDIM, acc / acc[:, HDIM:HDIM + 1], 0.0).astype(BF16)


def _attention(q, k, v):
    bsz, nh, seq, _ = q.shape
    rep = nh // k.shape[1]
    tq = min(1024, seq)
    tk = min(256, seq // 2)
    nk = seq // tk
    kern = functools.partial(_attn_kernel, tk=tk, nk=nk)
    kv_spec = pl.BlockSpec((None, None, seq, LANES), lambda b, h, i: (b, h // rep, 0, 0))
    return pl.pallas_call(
        kern,
        grid=(bsz, nh, seq // tq),
        in_specs=[pl.BlockSpec((None, None, tq, LANES), lambda b, h, i: (b, h, i, 0)), kv_spec, kv_spec],
        out_specs=pl.BlockSpec((None, tq, LANES), lambda b, h, i: (b, i, h)),
        out_shape=jax.ShapeDtypeStruct((bsz, seq, nh * LANES), BF16),
        scratch_shapes=[pltpu.VMEM((tq, tk), F32), pltpu.VMEM((tq, tk), F32),
                        pltpu.VMEM((tq, tk), BF16), pltpu.VMEM((tq, tk), BF16),
                        pltpu.VMEM((tq, LANES), F32)],
        compiler_params=_cparams("parallel", "parallel", "arbitrary"),
        name="attention",
    )(q, k, v)


def _ssd_kernel(*refs, bsz, nc):
    ins = refs[:8]
    (cw_ref, cb_ref, dtb_ref, na_ref, trif_ref, trib_ref, e2_ref, ew_ref, smask_ref, dsk_ref,
     yf_ref, yb_ref, s_ref) = refs[8:]
    step = pl.program_id(0)
    q_len = C_CHUNK

    @pl.when(step == 0)
    def _():
        s_ref[...] = jnp.zeros_like(s_ref)

    chains = [(d, b) for d in range(2) for b in range(bsz)]
    chunk = (step, nc - 1 - step)
    tri = (trif_ref[...], trib_ref[...])
    lane = lax.broadcasted_iota(jnp.int32, (q_len, LANES), 1)
    ri = lax.broadcasted_iota(jnp.int32, (q_len, q_len), 0)
    ci = lax.broadcasted_iota(jnp.int32, (q_len, q_len), 1)
    causal = (ri >= ci, ri <= ci)
    lane_w = lax.broadcasted_iota(jnp.int32, (q_len, GROUP_W), 1)
    smask = smask_ref[...] > 0

    xs, bm, cm, dt_slot, cum_slot = [], [], [], [], []
    for d, b in chains:
        xp_ref, x_ref, xn_ref, sm_ref = ins[4 * d:4 * d + 4]
        prev_row = xp_ref[b, SUBLANES - 1:SUBLANES, :] * jnp.where(chunk[d] > 0, 1.0, 0.0)
        next_row = xn_ref[b, 0:1, :] * jnp.where(chunk[d] < nc - 1, 1.0, 0.0)
        act = _silu(_conv3(x_ref[b], prev_row, next_row, cw_ref, cb_ref[...]))
        xs.append(act[:, :GROUP_W])
        bm.append(act[:, GROUP_W:GROUP_W + LANES])
        cm.append(act[:, GROUP_W + LANES:])
        dt = _softplus(sm_ref[b] + dtb_ref[...])
        dt_slot.append(dt)
        cum_slot.append(_dot_sel_lhs(tri[d], dt * na_ref[...]))
    cum_w = [_dot_sel_rhs(cum_slot[i], ew_ref[d]) for i, (d, b) in enumerate(chains)]
    cum_2 = [_dot_sel_rhs(cum_slot[i], e2_ref[d]) for i, (d, b) in enumerate(chains)]
    dt_w = [_dot_sel_rhs(dt_slot[i], ew_ref[d]) for i, (d, b) in enumerate(chains)]
    cb = [_dot_nt(jnp.concatenate([jnp.where(lane < C_STATE, c, 0.0), jnp.where(lane >= C_STATE, c, 0.0)], axis=0), bk)
          for c, bk in zip(cm, bm)]
    s_in = [s_ref[bsz * d + b] for d, b in chains]
    y_off = [_dot(cm[i], s_in[i]) * jnp.exp(cum_w[i]) for i in range(len(chains))]
    xdt = [xs[i] * dt_w[i] for i in range(len(chains))]
    for i, (d, b) in enumerate(chains):
        edge = 0 if d else q_len - 1
        cum_edge = cum_w[i][edge:edge + 1, :]
        s_new = jnp.where(smask, _dot_tn(bm[i], xdt[i] * jnp.exp(cum_edge - cum_w[i])), 0.0)
        s_ref[bsz * d + b] = s_in[i] * jnp.exp(cum_edge) + s_new
    y = list(y_off)
    for h in range(HEADS):
        g = h // 2
        head_lanes = (lane_w >= HDIM * h) & (lane_w < HDIM * (h + 1))
        for i, (d, b) in enumerate(chains):
            a_bc = cum_2[i][:, LANES * h:LANES * (h + 1)]
            decay = jnp.exp(jnp.where(causal[d], a_bc - a_bc.T, -1e30))
            scores = cb[i][q_len * g:q_len * (g + 1), :] * decay
            yh = jnp.dot(scores.astype(BF16), xdt[i].astype(BF16), preferred_element_type=F32)
            y[i] = y[i] + jnp.where(head_lanes, yh, 0.0)
    for i, (d, b) in enumerate(chains):
        if d == 0:
            yf_ref[b] = y[i] + xs[i] * dsk_ref[...]
        else:
            yb_ref[b] = y[i]


def _ssd_scan(p3, conv_w, conv_b, dtb_slot, na_slot, dsk_w):
    bsz, seq, _ = p3.shape
    q_len = C_CHUNK
    nc = seq // q_len
    hb = q_len // SUBLANES
    nblk8 = seq // SUBLANES
    ch = (lambda s: s, lambda s: nc - 1 - s)
    trif = _np_bf16(_tri_blocks(q_len, q_len, upper=False))
    trib = _np_bf16(_tri_blocks(q_len, q_len, upper=True))
    e2 = _np_bf16(np.stack([_head_expand(SM_DT + HEADS * d, LANES) for d in range(2)]))
    ew = _np_bf16(np.stack([_head_expand(SM_DT + HEADS * d, HDIM) for d in range(2)]))
    gi = np.arange(LANES)[:, None] // C_STATE
    hi = np.arange(GROUP_W)[None, :] // HDIM
    smask = jnp.asarray((gi == hi // 2).astype(np.float32))

    def dir_specs(d):
        return [pl.BlockSpec((bsz, SUBLANES, C_XBC), lambda s: (0, jnp.maximum(ch[d](s) * hb - 1, 0), 0)),
                pl.BlockSpec((bsz, q_len, C_XBC), lambda s: (0, ch[d](s), OFF_C_XBC // C_XBC)),
                pl.BlockSpec((bsz, SUBLANES, C_XBC), lambda s: (0, jnp.minimum((ch[d](s) + 1) * hb, nblk8 - 1), 0)),
                pl.BlockSpec((bsz, q_len, LANES), lambda s: (0, ch[d](s), OFF_SMALL // LANES))]

    full = lambda a: pl.BlockSpec(a.shape, lambda s: (0,) * a.ndim)
    consts = [conv_w, conv_b, dtb_slot, na_slot, trif, trib, e2, ew, smask, dsk_w]
    out_blk = lambda d: pl.BlockSpec((bsz, q_len, GROUP_W), lambda s: (0, ch[d](s), 0))
    out_shape = jax.ShapeDtypeStruct((bsz, seq, GROUP_W), F32)
    return pl.pallas_call(
        functools.partial(_ssd_kernel, bsz=bsz, nc=nc),
        grid=(nc,),
        in_specs=dir_specs(0) + dir_specs(1) + [full(a) for a in consts],
        out_specs=[out_blk(0), out_blk(1)],
        out_shape=[out_shape, out_shape],
        scratch_shapes=[pltpu.VMEM((2 * bsz, LANES, GROUP_W), F32)],
        compiler_params=_cparams("arbitrary"),
        name="ssd_scan",
    )(*([p3] * 8), *consts)


def _delta_local_kernel(xp_ref, x_ref, xn_ref, sm_ref, cw_ref, na_ref, dtb_ref, ones_ref, trif_ref, trib_ref,
                        eg_ref, eb_ref, bd_ref, u_ref, w_ref, qk_ref, qd_ref, kd_ref, ge_ref, *, nblk, cb, parts):
    blk = pl.program_id(1)
    q_len = D_CHUNK
    rows = x_ref.shape[0]
    part_rows = rows // parts
    part_cb = cb // parts
    ones_bd = ones_ref[...]
    bd_mask = bd_ref[...] > 0
    lane = lax.broadcasted_iota(jnp.int32, (q_len, GROUP_W), 1) & (q_len - 1)
    row = lax.broadcasted_iota(jnp.int32, (q_len, GROUP_W), 0)
    eye_w = jnp.where(row == lane, 1.0, 0.0)
    incl = (row >= lane, row <= lane)
    strict = (row > lane, row < lane)
    tri = (trif_ref[...], trib_ref[...])
    rows_of = lambda c: slice(q_len * c, q_len * (c + 1))

    def l2n(t):
        ss = _dot_sel_rhs(t * t, ones_bd)
        return t * lax.rsqrt(ss + 1e-6)

    def front(g, out):
        r0, r1 = part_rows * g, part_rows * (g + 1)
        prev_row = (xp_ref[SUBLANES - 1:SUBLANES, :] * jnp.where(blk > 0, 1.0, 0.0) if g == 0
                    else x_ref[r0 - 1:r0, :])
        next_row = (xn_ref[0:1, :] * jnp.where(blk < nblk - 1, 1.0, 0.0) if g == parts - 1
                    else x_ref[r1:r1 + 1, :])
        x = x_ref[r0:r1, :]
        n = part_rows
        ri = lax.broadcasted_iota(jnp.int32, (n, GROUP_W), 0)

        def conv_act(cs):
            xc = x[:, cs]
            xm1 = jnp.where(ri == 0, prev_row[:, cs], pltpu.roll(xc, 1, axis=0))
            xp1 = jnp.where(ri == n - 1, next_row[:, cs], pltpu.roll(xc, n - 1, axis=0))
            return _silu(xm1 * cw_ref[0:1, cs] + xc * cw_ref[1:2, cs] + xp1 * cw_ref[2:3, cs])

        out["q"] = l2n(conv_act(slice(0, GROUP_W))) * (HDIM ** -0.5)
        yield
        out["k"] = l2n(conv_act(slice(GROUP_W, 2 * GROUP_W)))
        yield
        out["v"] = conv_act(slice(2 * GROUP_W, 3 * GROUP_W))
        sm = sm_ref[r0:r1, :]
        beta_slot = jax.nn.sigmoid(sm)
        g_slot = na_ref[...] * _softplus(sm + dtb_ref[...])
        yield
        out["g_w"] = [_dot_sel_rhs(g_slot, eg_ref[d]) for d in range(2)]
        out["beta_w"] = [_dot_sel_rhs(beta_slot, eb_ref[d]) for d in range(2)]
        yield

    def back(g, f):
        q_all, k_all, v_all, g_w, beta_w = f["q"], f["k"], f["v"], f["g_w"], f["beta_w"]
        units = [(d, c) for d in range(2) for c in range(part_cb)]
        seg, g_cum = [], []
        for d, c in units:
            gu = g_w[d][rows_of(c)]
            sg = _dot_sel_lhs(tri[d], jnp.concatenate([jnp.where(strict[d], gu, 0.0), gu], axis=1))
            seg.append(sg[:, :GROUP_W])
            g_cum.append(sg[:, GROUP_W:])
        yield
        kq = []
        for d, c in units:
            kc = k_all[rows_of(c)]
            lhs = jnp.concatenate([kc * beta_w[d][rows_of(c)], q_all[rows_of(c)]], axis=0)
            kq.append(_dot_nt(lhs, _block_diag(kc, bd_mask)))
        yield
        decay = [jnp.exp(jnp.where(incl[d], seg[i], -1e30)) for i, (d, c) in enumerate(units)]
        n_w = [jnp.where(strict[d], -(kq[i][:q_len] * decay[i]), 0.0) for i, (d, c) in enumerate(units)]
        p_w = [eye_w + n for n in n_w]
        m_w = [_dot(n, _block_diag(n, bd_mask)) for n in n_w]
        yield
        for _ in range(4):
            r = [_dot(jnp.concatenate([m, p], axis=0), _block_diag(m, bd_mask)) for m, p in zip(m_w, p_w)]
            m_w = [ri[:q_len] for ri in r]
            p_w = [p + ri[q_len:] for p, ri in zip(p_w, r)]
            yield
        t_w = [p + _dot(p, _block_diag(m, bd_mask)) for m, p in zip(m_w, p_w)]
        yield
        for i, (d, c) in enumerate(units):
            rs = rows_of(c)
            orow = slice(part_rows * g + q_len * c, part_rows * g + q_len * (c + 1))
            oc = part_cb * g + c
            kc, bw = k_all[rs], beta_w[d][rs]
            exp_g = jnp.exp(g_cum[i])
            edge = 0 if d else q_len - 1
            g_edge = g_cum[i][edge:edge + 1, :]
            rhs = jnp.concatenate([_block_diag(v_all[rs] * bw, bd_mask), _block_diag(kc * bw * exp_g, bd_mask)],
                                  axis=1)
            uw = _dot(t_w[i], rhs)
            u_ref[d, orow, :] = uw[:, :GROUP_W]
            w_ref[d, orow, :] = uw[:, GROUP_W:].astype(BF16)
            qk_ref[d, orow, :] = (kq[i][q_len:] * decay[i]).astype(BF16)
            qd_ref[d, orow, :] = (q_all[rs] * exp_g).astype(BF16)
            kd_ref[d, orow, :] = (kc * jnp.exp(g_edge - g_cum[i])).astype(BF16)
            ge_ref[d, SUBLANES * oc:SUBLANES * (oc + 1), :] = jnp.broadcast_to(jnp.exp(g_edge), (SUBLANES, GROUP_W))
        yield

    fronts = [dict() for _ in range(parts)]
    for _ in front(0, fronts[0]):
        pass
    for g in range(parts):
        nxt = front(g + 1, fronts[g + 1]) if g + 1 < parts else iter(())
        for _ in back(g, fronts[g]):
            next(nxt, None)
        for _ in nxt:
            pass


def _delta_local(p3, conv_w, na_slot, dtb_slot):
    bsz, seq, _ = p3.shape
    q_len = D_CHUNK
    rows = min(512, seq)
    cb = rows // q_len
    parts = 2
    nblk = seq // rows
    hb = rows // SUBLANES
    nblk8 = seq // SUBLANES
    ones_bd = _np_bf16(np.kron(np.eye(HEADS), np.ones((HDIM, HDIM))))
    trif = _np_bf16(_tri_blocks(q_len, q_len, upper=False))
    trib = _np_bf16(_tri_blocks(q_len, q_len, upper=True))
    eg = _np_bf16(np.stack([_head_expand(SM_AB + 8 + HEADS * d, HDIM) for d in range(2)]))
    eb = _np_bf16(np.stack([_head_expand(SM_AB + HEADS * d, HDIM) for d in range(2)]))
    bd = jnp.asarray(np.kron(np.eye(HEADS), np.ones((q_len, HDIM))).astype(np.float32))
    x_spec = pl.BlockSpec((None, rows, D_QKV), lambda b, i: (b, i, OFF_D_QKV // D_QKV))
    xp_spec = pl.BlockSpec((None, SUBLANES, D_QKV), lambda b, i: (b, jnp.maximum(i * hb - 1, 0), OFF_D_QKV // D_QKV))
    xn_spec = pl.BlockSpec((None, SUBLANES, D_QKV), lambda b, i: (b, jnp.minimum((i + 1) * hb, nblk8 - 1), OFF_D_QKV // D_QKV))
    sm_spec = pl.BlockSpec((None, rows, LANES), lambda b, i: (b, i, OFF_SMALL // LANES))
    full = lambda a: pl.BlockSpec(a.shape, lambda b, i: (0,) * a.ndim)
    consts = [conv_w, na_slot, dtb_slot, ones_bd, trif, trib, eg, eb, bd]
    out_blk = pl.BlockSpec((None, 2, rows, GROUP_W), lambda b, i: (b, 0, i, 0))
    ge_blk = pl.BlockSpec((None, 2, SUBLANES * cb, GROUP_W), lambda b, i: (b, 0, i, 0))
    shp = lambda dt: jax.ShapeDtypeStruct((bsz, 2, seq, GROUP_W), dt)
    return pl.pallas_call(
        functools.partial(_delta_local_kernel, nblk=nblk, cb=cb, parts=parts),
        grid=(bsz, nblk),
        in_specs=[xp_spec, x_spec, xn_spec, sm_spec] + [full(a) for a in consts],
        out_specs=[out_blk] * 5 + [ge_blk],
        out_shape=[shp(F32), shp(BF16), shp(BF16), shp(BF16), shp(BF16),
                   jax.ShapeDtypeStruct((bsz, 2, seq // q_len * SUBLANES, GROUP_W), F32)],
        compiler_params=_cparams("parallel", "parallel"),
        name="delta_local",
    )(p3, p3, p3, p3, *consts)


def _delta_scan_kernel(*refs, bsz):
    ins, (bd_ref, of_ref, ob_ref, s_ref) = refs[:12], refs[12:]
    o_refs = (of_ref, ob_ref)

    @pl.when(pl.program_id(0) == 0)
    def _():
        s_ref[...] = jnp.zeros_like(s_ref)

    q_len = D_CHUNK
    bd_mask = bd_ref[...] > 0
    chains = [(d, b) for d in range(2) for b in range(bsz)]
    get = lambda k, d, b: ins[6 * d + k][b]
    s = [s_ref[bsz * d + b] for d, b in chains]
    r = [jnp.dot(jnp.concatenate([get(1, d, b), get(3, d, b)], axis=0), s[i].astype(BF16),
                 preferred_element_type=F32) for i, (d, b) in enumerate(chains)]
    v_new = [get(0, d, b) - r[i][:q_len] for i, (d, b) in enumerate(chains)]
    upd = [lax.dot_general(get(4, d, b), v_new[i].astype(BF16), (((0,), (0,)), ((), ())),
                           preferred_element_type=F32) for i, (d, b) in enumerate(chains)]
    for i, (d, b) in enumerate(chains):
        s_ref[bsz * d + b] = s[i] * get(5, d, b)[0:1, :] + jnp.where(bd_mask, upd[i], 0.0)
    for i, (d, b) in enumerate(chains):
        o_refs[d][b] = r[i][q_len:] + jnp.dot(get(2, d, b), _block_diag(v_new[i], bd_mask),
                                              preferred_element_type=F32)


def _delta_scan(u, w, qk, qd, kd, ge):
    bsz, _, seq, _ = u.shape
    q_len = D_CHUNK
    nc = seq // q_len
    bd = jnp.asarray(np.kron(np.eye(HEADS), np.ones((q_len, HDIM))).astype(np.float32))
    ch = (lambda c: c, lambda c: nc - 1 - c)
    blk = lambda d, rows: pl.BlockSpec((bsz, None, rows, GROUP_W), lambda c: (0, d, ch[d](c), 0))
    in_specs = [blk(d, rows) for d in range(2) for rows in (q_len,) * 5 + (SUBLANES,)]
    out_blk = lambda d: pl.BlockSpec((bsz, q_len, GROUP_W), lambda c: (0, ch[d](c), 0))
    out_shape = jax.ShapeDtypeStruct((bsz, seq, GROUP_W), F32)
    return pl.pallas_call(
        functools.partial(_delta_scan_kernel, bsz=bsz),
        grid=(nc,),
        in_specs=in_specs + [pl.BlockSpec(bd.shape, lambda c: (0, 0))],
        out_specs=[out_blk(0), out_blk(1)],
        out_shape=[out_shape, out_shape],
        scratch_shapes=[pltpu.VMEM((2 * bsz, GROUP_W, GROUP_W), F32)],
        compiler_params=_cparams("arbitrary"),
        name="delta_scan",
    )(*([u, w, qk, qd, kd, ge] * 2), bd)


def _out_proj_kernel(oa_ref, ob_ref, ycf_ref, ycb_ref, zc_ref, odf_ref, odb_ref, zd_ref, x_ref,
                     wa_ref, wb_ref, wc_ref, wd_ref, ga_ref, gb_ref, gc_ref, gd_ref, ones_ref, gp_ref, o_ref):
    oa = (_rms(oa_ref[...].astype(F32), GROUP_W) * ga_ref[...]).astype(BF16)
    ob = (_rms(ob_ref[...].astype(F32), GROUP_W) * gb_ref[...]).astype(BF16)
    oc = (_rms((ycf_ref[...] + ycb_ref[...]) * _silu(zc_ref[...]), GROUP_W) * gc_ref[...]).astype(BF16)
    od = odf_ref[...] + odb_ref[...]
    ms = _dot_sel_rhs(od * od, ones_ref[...]) * (1.0 / HDIM)
    od = od * lax.rsqrt(ms + EPS) * gd_ref[...] * _silu(zd_ref[...])
    acc = jnp.dot(oa, wa_ref[...], preferred_element_type=F32)
    acc += jnp.dot(ob, wb_ref[...], preferred_element_type=F32)
    acc += jnp.dot(oc, wc_ref[...], preferred_element_type=F32)
    acc += jnp.dot(od.astype(BF16), wd_ref[...], preferred_element_type=F32)
    o_ref[...] = x_ref[...] + _rms(acc, D_MODEL) * gp_ref[...]


def _out_proj(oa, ob, ycf, ycb, odf, odb, p3, x3, wa, wb, wc, wd, ga, gb, gc, gd, gpost):
    bsz, seq, _ = x3.shape
    tm = min(512, seq)
    ones_bd = _np_bf16(np.kron(np.eye(HEADS), np.ones((HDIM, HDIM))))
    rows = lambda w: pl.BlockSpec((None, tm, w), lambda b, i: (b, i, 0))
    full = lambda a: pl.BlockSpec(a.shape, lambda b, i: (0,) * a.ndim)
    z_spec = lambda off: pl.BlockSpec((None, tm, GROUP_W), lambda b, i: (b, i, off // GROUP_W))
    consts = [wa, wb, wc, wd, ga, gb, gc, gd, ones_bd, gpost]
    return pl.pallas_call(
        _out_proj_kernel,
        grid=(bsz, seq // tm),
        in_specs=[rows(HEADS * LANES), rows(HEADS * LANES), rows(GROUP_W), rows(GROUP_W), z_spec(OFF_C_Z),
                  rows(GROUP_W), rows(GROUP_W), z_spec(OFF_D_Z), rows(D_MODEL)] + [full(a) for a in consts],
        out_specs=rows(D_MODEL),
        out_shape=jax.ShapeDtypeStruct(x3.shape, F32),
        compiler_params=_cparams("parallel", "parallel"),
        name="out_proj",
    )(oa, ob, ycf, ycb, p3, odf, odb, p3, x3, *consts)


def _ffn_kernel(xp_ref, x_ref, xn_ref, gpre_ref, win_ref, cw_ref, cb_ref, wout_ref, gpost_ref, o_ref,
                gate_ref, up_ref, act_ref, *, tiles_per_seq, nj):
    tm = x_ref.shape[0]
    halo = BF16_ROWS
    pos = lax.rem(pl.program_id(0), tiles_per_seq)
    norm = lambda t: _rms(t, D_MODEL) * gpre_ref[...]
    h = jnp.concatenate([(norm(xp_ref[...]) * jnp.where(pos == 0, 0.0, 1.0)).astype(BF16),
                         norm(x_ref[...]).astype(BF16),
                         (norm(xn_ref[...]) * jnp.where(pos == tiles_per_seq - 1, 0.0, 1.0)).astype(BF16)], axis=0)
    n = tm + 2 * halo
    body = slice(halo, halo + tm)

    def project(j):
        for ref, off in ((gate_ref, 0), (up_ref, D_FF)):
            ref[j % 2] = jnp.dot(h, win_ref[:, off + FF_TILE * j:off + FF_TILE * (j + 1)],
                                 preferred_element_type=F32)

    def conv(ref, off, j):
        t = ref[j % 2]
        cs = slice(off + FF_TILE * j, off + FF_TILE * (j + 1))
        return (pltpu.roll(t, 1, axis=0)[body] * cw_ref[0:1, cs] + t[body] * cw_ref[1:2, cs]
                + pltpu.roll(t, n - 1, axis=0)[body] * cw_ref[2:3, cs] + cb_ref[:, cs])

    project(0)
    for j in range(nj):
        if j + 1 < nj:
            project(j + 1)
        act_ref[:, FF_TILE * j:FF_TILE * (j + 1)] = (
            _silu(conv(gate_ref, 0, j)) * conv(up_ref, D_FF, j)).astype(BF16)
    acc = jnp.dot(act_ref[...], wout_ref[...], preferred_element_type=F32)
    o_ref[...] = x_ref[...] + _rms(acc, D_MODEL) * gpost_ref[...]


def _ffn(x2d, seq, layer, gpre, w_in, conv_w, conv_b, w_out, gpost):
    m = x2d.shape[0]
    tm = min(512, seq)
    nj = D_FF // FF_TILE
    hb = tm // BF16_ROWS
    nblk = m // BF16_ROWS
    kern = functools.partial(_ffn_kernel, tiles_per_seq=seq // tm, nj=nj)
    const = lambda a: pl.BlockSpec((None,) + a.shape[1:], lambda i: (layer, 0, 0), pipeline_mode=pl.Buffered(1))
    return pl.pallas_call(
        kern,
        grid=(m // tm,),
        in_specs=[pl.BlockSpec((BF16_ROWS, D_MODEL), lambda i: (jnp.maximum(i * hb - 1, 0), 0)),
                  pl.BlockSpec((tm, D_MODEL), lambda i: (i, 0)),
                  pl.BlockSpec((BF16_ROWS, D_MODEL), lambda i: (jnp.minimum((i + 1) * hb, nblk - 1), 0)),
                  const(gpre), const(w_in), const(conv_w), const(conv_b), const(w_out), const(gpost)],
        out_specs=pl.BlockSpec((tm, D_MODEL), lambda i: (i, 0)),
        out_shape=jax.ShapeDtypeStruct(x2d.shape, F32),
        scratch_shapes=[pltpu.VMEM((2, tm + 2 * BF16_ROWS, FF_TILE), F32),
                        pltpu.VMEM((2, tm + 2 * BF16_ROWS, FF_TILE), F32),
                        pltpu.VMEM((tm, D_FF), BF16)],
        compiler_params=_cparams("parallel"),
        name="conv_ffn",
    )(x2d, x2d, x2d, gpre, w_in, conv_w, conv_b, w_out, gpost)


def _slot_vec(off, vals):
    n = vals.shape[-1]
    return jnp.pad(vals.astype(F32), ((0, 0), (off, LANES - off - n)))[:, None, :]


def _prep_layers(prm):
    nl = prm["w_in"].shape[0]
    row = lambda g: g[:, None, :]
    wuq = prm["a_w_uq"].reshape(nl, A_Q_LORA, HEADS, A_NOPE + A_ROPE)
    rope_w = wuq[..., A_NOPE:]
    zpad = jnp.zeros((nl, A_Q_LORA, HEADS, LANES - A_NOPE - A_ROPE), F32)
    flat_q = lambda parts: jnp.pad(jnp.concatenate(parts, axis=-1).reshape(nl, A_Q_LORA, HEADS * LANES),
                                   ((0, 0), (0, 256 - A_Q_LORA), (0, 0))).astype(BF16)
    w1 = flat_q([wuq[..., :A_NOPE], rope_w, zpad])
    w2 = flat_q([jnp.zeros_like(wuq[..., :A_NOPE]), _rot_last(rope_w), zpad])
    wukv = prm["a_w_ukv"].reshape(nl, A_KV_LORA, HEADS, A_NOPE + HDIM)
    hpad = jnp.zeros((nl, A_KV_LORA, HEADS, LANES - HDIM), F32)
    flat_kv = lambda t: jnp.concatenate([t, hpad], axis=-1).reshape(nl, A_KV_LORA, HEADS * LANES).astype(BF16)
    c_b = (HDIM ** -0.5) * LOG2E
    pad64 = lambda g: row(jnp.pad(g, ((0, 0), (0, LANES - HDIM))))
    wo = prm["w_out"]
    return dict(
        w_in=_arrange_w_in(prm["w_in"]),
        pre_mix_norm=prm["pre_mix_norm"],
        mla=(row(jnp.pad(prm["a_q_norm"], ((0, 0), (0, 256 - A_Q_LORA)))), row(prm["a_kv_norm"]),
             w1, w2, flat_kv(wukv[..., :A_NOPE]), flat_kv(wukv[..., A_NOPE:])),
        gqa=(pad64(prm["b_q_norm"] * c_b), pad64(_swap_last(prm["b_q_norm"]) * c_b),
             pad64(prm["b_k_norm"]), pad64(_swap_last(prm["b_k_norm"]))),
        ssd=(prm["c_conv_w"], row(prm["c_conv_b"]), _slot_vec(SM_DT, prm["c_dt_bias"].reshape(nl, -1)),
             _slot_vec(SM_DT, -jnp.exp(prm["c_a_log"].astype(F32)).reshape(nl, -1)),
             row(jnp.repeat(prm["c_d_skip"].astype(F32), HDIM, axis=-1))),
        delta=(prm["d_conv_w"], _slot_vec(SM_AB + 8, -jnp.exp(prm["d_a_log"].astype(F32)).reshape(nl, -1)),
               _slot_vec(SM_AB + 8, prm["d_dt_bias"].reshape(nl, -1))),
        out=(_pad_heads(wo[:, 0:256], 1, HEADS, HDIM, LANES).astype(BF16),
             _pad_heads(wo[:, 256:512], 1, HEADS, HDIM, LANES).astype(BF16),
             wo[:, 512:768].astype(BF16), wo[:, 768:1024].astype(BF16),
             row(_pad_heads(prm["a_out_norm"], 1, HEADS, HDIM, LANES)),
             row(_pad_heads(prm["b_out_norm"], 1, HEADS, HDIM, LANES)),
             row(prm["c_out_norm"]), row(jnp.tile(prm["d_out_norm"], (1, HEADS))),
             row(prm["post_mix_norm"])),
    )


def _layer(x3, tabs, w, layer, ffn_prm):
    bsz, seq, _ = x3.shape
    p2d, qa, ka, va, qb, kb, vb = _in_proj(x3, w["pre_mix_norm"], w["w_in"], tabs, w["mla"], w["gqa"])
    p3 = p2d.reshape(bsz, seq, P_HBM_COLS)
    oa = _attention(qa, ka, va)
    ob = _attention(qb, kb, vb)
    ycf, ycb = _ssd_scan(p3, *w["ssd"])
    odf, odb = _delta_scan(*_delta_local(p3, *w["delta"]))
    x3 = _out_proj(oa, ob, ycf, ycb, odf, odb, p3, x3, *w["out"])
    x2d = _ffn(x3.reshape(bsz * seq, D_MODEL), seq, layer, *ffn_prm)
    return x2d.reshape(bsz, seq, D_MODEL)


def _attn_tables(seq):
    cos_a, sin_a = _rope_tables(seq, A_ROPE)
    cos_b, sin_b = _rope_tables(seq, HDIM)
    pada = lambda t: jnp.pad(t, ((0, 0), (A_NOPE, LANES - A_NOPE - A_ROPE)))
    padb = lambda t: jnp.pad(t, ((0, 0), (0, LANES - HDIM)))
    return (pada(cos_a), pada(sin_a)), (padb(cos_b), padb(sin_b))


def kernel(x, pre_mix_norm, w_in, a_q_norm, a_w_uq, a_kv_norm, a_w_ukv, a_out_norm, b_q_norm, b_k_norm, b_out_norm, c_conv_w, c_conv_b, c_a_log, c_dt_bias, c_d_skip, c_out_norm, d_conv_w, d_a_log, d_dt_bias, d_out_norm, w_out, post_mix_norm, pre_ffn_norm, f_w_in, f_conv_w, f_conv_b, f_w_out, post_ffn_norm):
    params = dict(pre_mix_norm=pre_mix_norm, w_in=w_in, a_q_norm=a_q_norm, a_w_uq=a_w_uq, a_kv_norm=a_kv_norm,
                  a_w_ukv=a_w_ukv, a_out_norm=a_out_norm, b_q_norm=b_q_norm, b_k_norm=b_k_norm,
                  b_out_norm=b_out_norm, c_conv_w=c_conv_w, c_conv_b=c_conv_b, c_a_log=c_a_log,
                  c_dt_bias=c_dt_bias, c_d_skip=c_d_skip, c_out_norm=c_out_norm, d_conv_w=d_conv_w,
                  d_a_log=d_a_log, d_dt_bias=d_dt_bias, d_out_norm=d_out_norm, w_out=w_out,
                  post_mix_norm=post_mix_norm)
    ffn_prm = (pre_ffn_norm[:, None, :], f_w_in.astype(BF16), f_conv_w, f_conv_b[:, None, :],
               f_w_out.astype(BF16), post_ffn_norm[:, None, :])
    tabs_a, tabs_b = _attn_tables(x.shape[1])
    prepared = _prep_layers(params)
    for layer in range(w_in.shape[0]):
        w = jax.tree_util.tree_map(lambda a: a[layer], prepared)
        x = _layer(x, tabs_a + tabs_b, w, layer, ffn_prm)
    return x
```
